```python
import math
import jax
import jax.numpy as jnp
from jax import lax
import numpy as np

D_MODEL = 1024
BATCH = 8
SEQ = 4096
DEPTH = 4

GRID_W = 64
CTX_LEN = 256
N_MIXERS = 4
N_MOD = 6
ROPE_BASE = 10000.0
NORM_EPS = 1e-6
Q_BLOCK = 128

MLA_HEADS = 16
MLA_Q_RANK = 512
MLA_KV_RANK = 256
MLA_NOPE = 64
MLA_ROPE = 32
MLA_V = 64

GQA_Q_HEADS = 8
GQA_KV_HEADS = 4
GQA_HEAD_DIM = 128

MLSTM_HEADS = 4
MLSTM_QK = 128
MLSTM_V = 256
MLSTM_CHUNK = 64

GDN_QK_HEADS = 8
GDN_V_HEADS = 16
GDN_HEAD_DIM = 128
GDN_CONV = 5
GDN_CHUNK = 64

N_EXPERTS = 32
TOP_K = 4
D_EXPERT = 1024
SWIGLU_ALPHA = 1.702
SWIGLU_LIMIT = 7.0

kernel_name = 'hybrid_mla_gqa_mlstm_gdn_moe_dit'


def rms_norm(x, g):
    xf = x.astype(jnp.float32)
    y = xf * lax.rsqrt(jnp.mean(xf * xf, axis=-1, keepdims=True) + NORM_EPS)
    return (y * g.astype(jnp.float32)).astype(x.dtype)


def l2_normalize(x):
    return x * lax.rsqrt(jnp.sum(x * x, axis=-1, keepdims=True) + NORM_EPS)


def split_heads(a, n_heads):
    b, t, w = a.shape
    return a.reshape(b, t, n_heads, w // n_heads).transpose(0, 2, 1, 3)


def merge_heads(o):
    b, n, t, d = o.shape
    return o.transpose(0, 2, 1, 3).reshape(b, t, n * d)


def reverse_time(a, flip):
    return jnp.flip(a, axis=2) if flip else a


def axial_rope(n_tokens, rot_dim):
    t = jnp.arange(n_tokens, dtype=jnp.int32)
    rows = (t // GRID_W).astype(jnp.float32)
    cols = (t % GRID_W).astype(jnp.float32)
    d_axis = rot_dim // 2
    inv_freq = ROPE_BASE ** (-jnp.arange(0, d_axis, 2, dtype=jnp.float32) / d_axis)
    ang = jnp.concatenate([rows[:, None] * inv_freq, cols[:, None] * inv_freq], axis=-1)
    return jnp.cos(ang), jnp.sin(ang)


def apply_rope(x, rope_cs):
    cos, sin = rope_cs
    half = x.shape[-1] // 2
    x1 = x[..., :half].astype(jnp.float32)
    x2 = x[..., half:].astype(jnp.float32)
    return jnp.concatenate([x1 * cos - x2 * sin, x1 * sin + x2 * cos], axis=-1).astype(x.dtype)


def softmax_attention(q, k, v, scale):
    s = jnp.einsum('bhgqd,bhkd->bhgqk', q, k).astype(jnp.float32) * scale
    p = jax.nn.softmax(s, axis=-1).astype(v.dtype)
    return jnp.einsum('bhgqk,bhkd->bhgqd', p, v)


def dense_attention(q, k, v, scale):
    o = softmax_attention(q, k, v, scale)
    b, hk, g, t, dv = o.shape
    return o.reshape(b, hk * g, t, dv)


def blocked_attention(q, k, v, scale):
    b, hk, g, t, dq = q.shape
    nb = t // Q_BLOCK
    qb = jnp.moveaxis(q.reshape(b, hk, g, nb, Q_BLOCK, dq), 3, 0)
    ob = lax.map(lambda q_blk: softmax_attention(q_blk, k, v, scale), qb)
    return jnp.moveaxis(ob, 0, 3).reshape(b, hk * g, t, v.shape[-1])


def centred_depthwise_conv(x, w):
    pad = w.shape[0] // 2
    return lax.conv_general_dilated(x, w[:, None, :].astype(x.dtype), window_strides=(1,), padding=[(pad, pad)],
                                    dimension_numbers=('NWC', 'WIO', 'NWC'), feature_group_count=x.shape[-1])


def mla_mixer(h_ctx, h_lat, w_in, q_norm_g, kv_norm_g, w_uq, w_ukv, w_o, with_ctx_out):
    scale = (MLA_NOPE + MLA_ROPE) ** -0.5
    rope = axial_rope(h_lat.shape[1], MLA_ROPE)

    def compress(h):
        z = h @ w_in
        return z[..., :MLA_Q_RANK], z[..., MLA_Q_RANK:MLA_Q_RANK + MLA_KV_RANK], z[..., MLA_Q_RANK + MLA_KV_RANK:]

    def queries(c_q, rope_cs):
        q = split_heads(rms_norm(c_q, q_norm_g) @ w_uq, MLA_HEADS)
        q_rot = q[..., MLA_NOPE:]
        if rope_cs is not None:
            q_rot = apply_rope(q_rot, rope_cs)
        return jnp.concatenate([q[..., :MLA_NOPE], q_rot], axis=-1)[:, :, None]

    def keys_values(c_kv, k_rot, rope_cs):
        kv = split_heads(rms_norm(c_kv, kv_norm_g) @ w_ukv, MLA_HEADS)
        k_nope, v = kv[..., :MLA_NOPE], kv[..., MLA_NOPE:]
        k_rot = k_rot[:, None]
        if rope_cs is not None:
            k_rot = apply_rope(k_rot, rope_cs)
        k_rot = jnp.broadcast_to(k_rot, k_nope.shape[:3] + (MLA_ROPE,))
        return jnp.concatenate([k_nope, k_rot], axis=-1), v

    cq_c, ckv_c, kr_c = compress(h_ctx)
    cq_l, ckv_l, kr_l = compress(h_lat)
    k_c, v_c = keys_values(ckv_c, kr_c, None)
    k_l, v_l = keys_values(ckv_l, kr_l, rope)
    q_l = queries(cq_l, rope)
    o_l = blocked_attention(q_l, jnp.concatenate([k_l, k_c], axis=2), jnp.concatenate([v_l, v_c], axis=2), scale)
    y_l = merge_heads(o_l) @ w_o
    y_c = None
    if with_ctx_out:
        y_c = merge_heads(dense_attention(queries(cq_c, None), k_c, v_c, scale)) @ w_o
    return y_c, y_l


def gqa_mixer(h_ctx, h_lat, w_in, q_norm_g, k_norm_g, w_o, with_ctx_out):
    group = GQA_Q_HEADS // GQA_KV_HEADS
    scale = GQA_HEAD_DIM ** -0.5
    rope = axial_rope(h_lat.shape[1], GQA_HEAD_DIM)

    def project(h, rope_cs):
        z = split_heads(h @ w_in, GQA_Q_HEADS + 2 * GQA_KV_HEADS)
        q = rms_norm(z[:, :GQA_Q_HEADS], q_norm_g)
        k = rms_norm(z[:, GQA_Q_HEADS:GQA_Q_HEADS + GQA_KV_HEADS], k_norm_g)
        v = z[:, GQA_Q_HEADS + GQA_KV_HEADS:]
        if rope_cs is not None:
            q, k = apply_rope(q, rope_cs), apply_rope(k, rope_cs)
        b, _, t, _ = q.shape
        return q.reshape(b, GQA_KV_HEADS, group, t, GQA_HEAD_DIM), k, v

    q_c, k_c, v_c = project(h_ctx, None)
    q_l, k_l, v_l = project(h_lat, rope)
    o_l = blocked_attention(q_l, jnp.concatenate([k_l, k_c], axis=2), jnp.concatenate([v_l, v_c], axis=2), scale)
    y_l = merge_heads(o_l) @ w_o
    y_c = None
    if with_ctx_out:
        y_c = merge_heads(dense_attention(q_c, k_c, v_c, scale)) @ w_o
    return y_c, y_l


def mlstm_chunked(q, k, v, i_pre, log_f, state, with_out):
    b, h, t, dk = k.shape
    n_chunks = t // MLSTM_CHUNK
    idx = jnp.arange(MLSTM_CHUNK)
    incl = idx[:, None] >= idx[None, :]

    def to_chunks(a):
        return jnp.moveaxis(a.reshape(a.shape[:2] + (n_chunks, MLSTM_CHUNK) + a.shape[3:]), 2, 0)

    def step(carry, xc):
        c_st, n_st, m_st = carry
        if with_out:
            k_c, v_c, i_c, f_c, q_c = xc
        else:
            k_c, v_c, i_c, f_c = xc
        bcum = jnp.cumsum(f_c, axis=-1)
        b_end = bcum[..., -1]
        w_log = b_end[..., None] - bcum + i_c
        m_new = jnp.maximum(b_end + m_st, jnp.max(w_log, axis=-1))
        carry_scale = jnp.exp(b_end + m_st - m_new)
        w = jnp.exp(w_log - m_new[..., None])
        c_new = carry_scale[..., None, None] * c_st + jnp.einsum('bhl,bhld,bhle->bhde', w, k_c, v_c)
        n_new = carry_scale[..., None] * n_st + jnp.einsum('bhl,bhld->bhd', w, k_c)
        if not with_out:
            return (c_new, n_new, m_new), None
        g_log = bcum + m_st[..., None]
        d_log = jnp.where(incl, bcum[..., :, None] - bcum[..., None, :] + i_c[..., None, :], -jnp.inf)
        m_t = jnp.maximum(g_log, jnp.max(d_log, axis=-1))
        inter = jnp.exp(g_log - m_t)
        s = jnp.einsum('bhld,bhsd->bhls', q_c, k_c) * jnp.exp(d_log - m_t[..., None])
        num = inter[..., None] * jnp.einsum('bhld,bhde->bhle', q_c, c_st) + jnp.einsum('bhls,bhse->bhle', s, v_c)
        den = inter * jnp.einsum('bhld,bhd->bhl', q_c, n_st) + jnp.sum(s, axis=-1)
        h_out = num / jnp.maximum(jnp.abs(den), jnp.exp(-m_t))[..., None]
        return (c_new, n_new, m_new), h_out

    xs = (to_chunks(k), to_chunks(v), to_chunks(i_pre), to_chunks(log_f)) + ((to_chunks(q),) if with_out else ())
    state, hs = lax.scan(step, state, xs)
    if with_out:
        hs = jnp.moveaxis(hs, 0, 2).reshape(b, h, t, v.shape[-1])
    return state, hs


def mlstm_mixer(h_ctx, h_lat, w_in, gate_b, norm_g, w_o, with_ctx_out):
    nh = MLSTM_HEADS
    cuts = [nh * MLSTM_QK, 2 * nh * MLSTM_QK, 2 * nh * MLSTM_QK + nh * MLSTM_V, 2 * nh * MLSTM_QK + 2 * nh * MLSTM_V]

    def project(h):
        b, t, _ = h.shape
        q, k, v, o, gates = jnp.split(h @ w_in, cuts, axis=-1)
        q = split_heads(q, nh).astype(jnp.float32)
        k = split_heads(k, nh).astype(jnp.float32) * MLSTM_QK ** -0.5
        v = split_heads(v, nh).astype(jnp.float32)
        gates = gates.astype(jnp.float32).reshape(b, t, 2, 2, nh) + gate_b.astype(jnp.float32)
        gates = jnp.transpose(gates, (2, 3, 0, 4, 1))
        return q, k, v, o, gates[:, 0], jax.nn.log_sigmoid(gates[:, 1])

    def finish(hs, o):
        hn = rms_norm(jnp.transpose(hs, (0, 2, 1, 3)), norm_g)
        b, t = hn.shape[:2]
        y = hn.reshape(b, t, nh * MLSTM_V) * jax.nn.sigmoid(o.astype(jnp.float32))
        return y.astype(o.dtype) @ w_o

    q_c, k_c, v_c, o_c, i_c, f_c = project(h_ctx)
    q_l, k_l, v_l, o_l, i_l, f_l = project(h_lat)
    b = h_lat.shape[0]
    hs_c = 0.0
    hs_l = 0.0
    for d in range(2):
        flip = d == 1
        init = (jnp.zeros((b, nh, MLSTM_QK, MLSTM_V), jnp.float32), jnp.zeros((b, nh, MLSTM_QK), jnp.float32),
                jnp.zeros((b, nh), jnp.float32))
        st, h_cd = mlstm_chunked(reverse_time(q_c, flip), reverse_time(k_c, flip), reverse_time(v_c, flip),
                                 reverse_time(i_c[d], flip), reverse_time(f_c[d], flip), init, with_ctx_out)
        _, h_ld = mlstm_chunked(reverse_time(q_l, flip), reverse_time(k_l, flip), reverse_time(v_l, flip),
                                reverse_time(i_l[d], flip), reverse_time(f_l[d], flip), st, True)
        hs_l = hs_l + reverse_time(h_ld, flip)
        if with_ctx_out:
            hs_c = hs_c + reverse_time(h_cd, flip)
    y_l = finish(hs_l, o_l)
    y_c = finish(hs_c, o_c) if with_ctx_out else None
    return y_c, y_l


def gated_delta_chunked(q, k, v, log_decay, beta, state, with_out):
    b, h, t, dk = q.shape
    dv = v.shape[-1]
    n_chunks = t // GDN_CHUNK

    def chunk(a):
        return a.reshape((b, h, n_chunks, GDN_CHUNK) + a.shape[3:])

    q, k, v, log_decay, beta = chunk(q), chunk(k), chunk(v), chunk(log_decay), chunk(beta)
    gc = jnp.cumsum(log_decay, axis=-1)
    idx = jnp.arange(GDN_CHUNK)
    incl = idx[:, None] >= idx[None, :]
    strict = idx[:, None] > idx[None, :]
    decay = jnp.exp(jnp.where(incl, gc[..., :, None] - gc[..., None, :], -jnp.inf))
    k_beta = k * beta[..., None]
    m_kk = jnp.where(strict, jnp.einsum('bhcid,bhcjd->bhcij', k_beta, k) * decay, 0.0)
    a_mat = m_kk + jnp.eye(GDN_CHUNK, dtype=m_kk.dtype)
    rhs = jnp.concatenate([v * beta[..., None], k_beta * jnp.exp(gc)[..., None]], axis=-1)
    uw = lax.linalg.triangular_solve(a_mat, rhs, left_side=True, lower=True)
    u, w = uw[..., :dv], uw[..., dv:]
    k_dec = k * jnp.exp(gc[..., -1:] - gc)[..., None]
    g_last = jnp.exp(gc[..., -1])
    xs = (u, w, k_dec, g_last)
    if with_out:
        a_qk = jnp.einsum('bhcid,bhcjd->bhcij', q, k) * decay
        q_dec = q * jnp.exp(gc)[..., None]
        xs = xs + (a_qk, q_dec)
    xs = tuple(jnp.moveaxis(a, 2, 0) for a in xs)

    def step(s_st, xc):
        u_c, w_c, kd_c, gl_c = xc[:4]
        v_new = u_c - jnp.einsum('bhld,bhde->bhle', w_c, s_st)
        s_new = s_st * gl_c[..., None, None] + jnp.einsum('bhld,bhle->bhde', kd_c, v_new)
        if not with_out:
            return s_new, None
        aq_c, qd_c = xc[4:]
        o = jnp.einsum('bhld,bhde->bhle', qd_c, s_st) + jnp.einsum('bhls,bhse->bhle', aq_c, v_new)
        return s_new, o

    s_fin, o = lax.scan(step, state, xs)
    if with_out:
        o = jnp.moveaxis(o, 0, 2).reshape(b, h, t, dv)
    return s_fin, o


def gdn_mixer(h_ctx, h_lat, w_in, conv_w, a_log, dt_bias, norm_g, w_o, with_ctx_out):
    qk_w = GDN_QK_HEADS * GDN_HEAD_DIM
    v_w = GDN_V_HEADS * GDN_HEAD_DIM
    rep = GDN_V_HEADS // GDN_QK_HEADS
    a_scale = jnp.exp(a_log.astype(jnp.float32))[:, None, :, None]
    dt_b = dt_bias.astype(jnp.float32)[:, None, :, None]

    def project(h):
        b, t, _ = h.shape
        z = h @ w_in
        qkv = jax.nn.silu(centred_depthwise_conv(z[..., :2 * qk_w + v_w], conv_w)).astype(jnp.float32)
        q = jnp.repeat(l2_normalize(split_heads(qkv[..., :qk_w], GDN_QK_HEADS)), rep, axis=1) * GDN_HEAD_DIM ** -0.5
        k = jnp.repeat(l2_normalize(split_heads(qkv[..., qk_w:2 * qk_w], GDN_QK_HEADS)), rep, axis=1)
        v = split_heads(qkv[..., 2 * qk_w:], GDN_V_HEADS)
        gate = z[..., 2 * qk_w + v_w:2 * qk_w + 2 * v_w]
        ba = z[..., 2 * qk_w + 2 * v_w:].astype(jnp.float32).reshape(b, t, 2, 2, GDN_V_HEADS)
        ba = jnp.transpose(ba, (2, 3, 0, 4, 1))
        beta = jax.nn.sigmoid(ba[:, 0])
        log_decay = -a_scale * jax.nn.softplus(ba[:, 1] + dt_b)
        return q, k, v, gate, beta, log_decay

    def finish(o, gate):
        b, hv, t, d = o.shape
        y = rms_norm(jnp.transpose(o, (0, 2, 1, 3)), norm_g) * jax.nn.silu(gate.astype(jnp.float32).reshape(b, t, hv, d))
        return y.reshape(b, t, hv * d).astype(gate.dtype) @ w_o

    q_c, k_c, v_c, z_c, beta_c, g_c = project(h_ctx)
    q_l, k_l, v_l, z_l, beta_l, g_l = project(h_lat)
    b = h_lat.shape[0]
    o_c = 0.0
    o_l = 0.0
    for d in range(2):
        flip = d == 1
        init = jnp.zeros((b, GDN_V_HEADS, GDN_HEAD_DIM, GDN_HEAD_DIM), jnp.float32)
        st, oc = gated_delta_chunked(reverse_time(q_c, flip), reverse_time(k_c, flip), reverse_time(v_c, flip),
                                     reverse_time(g_c[d], flip), reverse_time(beta_c[d], flip), init, with_ctx_out)
        _, ol = gated_delta_chunked(reverse_time(q_l, flip), reverse_time(k_l, flip), reverse_time(v_l, flip),
                                    reverse_time(g_l[d], flip), reverse_time(beta_l[d], flip), st, True)
        o_l = o_l + reverse_time(ol, flip)
        if with_ctx_out:
            o_c = o_c + reverse_time(oc, flip)
    y_l = finish(o_l, z_l)
    y_c = finish(o_c, z_c) if with_ctx_out else None
    return y_c, y_l


def clamped_swiglu(u):
    glu = jnp.minimum(u[..., 0::2], SWIGLU_LIMIT)
    lin = jnp.clip(u[..., 1::2], -SWIGLU_LIMIT, SWIGLU_LIMIT)
    return glu * jax.nn.sigmoid(SWIGLU_ALPHA * glu) * (lin + 1.0)


def moe_ffn(h, router_w, router_b, w1, b1, w2, b2):
    logits = (h @ router_w).astype(jnp.float32) + router_b.astype(jnp.float32)
    top_val, top_idx = lax.top_k(logits, TOP_K)
    weights = jax.nn.softmax(top_val, axis=-1)
    combine = jnp.einsum('tk,tke->te', weights, jax.nn.one_hot(top_idx, N_EXPERTS, dtype=jnp.float32))
    out = jnp.zeros(h.shape, jnp.float32)
    for e in range(N_EXPERTS):
        y = clamped_swiglu(h @ w1[e] + b1[e]) @ w2[e] + b2[e]
        out = out + combine[:, e:e + 1] * y
    return out.astype(h.dtype)


def setup_inputs(seed: int = 0) -> dict:
    key = jax.random.key(seed)
    ks = jax.random.split(key, 48)
    counter = [0]

    def next_key():
        counter[0] += 1
        return ks[counter[0] - 1]

    def normal(shape, scale):
        return jax.random.normal(next_key(), shape, jnp.float32) * scale

    def gain(shape):
        return 1.0 + normal(shape, 0.05)

    def uniform(shape, lo, hi):
        return jax.random.uniform(next_key(), shape, jnp.float32, lo, hi)

    d = D_MODEL
    n_a, n_b, n_c, n_d = [len(range(m, DEPTH, N_MIXERS)) for m in range(N_MIXERS)]
    mla_in = MLA_Q_RANK + MLA_KV_RANK + MLA_ROPE
    gqa_in = (GQA_Q_HEADS + 2 * GQA_KV_HEADS) * GQA_HEAD_DIM
    mlstm_in = 2 * MLSTM_HEADS * MLSTM_QK + 2 * MLSTM_HEADS * MLSTM_V + 4 * MLSTM_HEADS
    gdn_conv_ch = 2 * GDN_QK_HEADS * GDN_HEAD_DIM + GDN_V_HEADS * GDN_HEAD_DIM
    gdn_in = gdn_conv_ch + GDN_V_HEADS * GDN_HEAD_DIM + 4 * GDN_V_HEADS
    dt = jnp.exp(uniform((n_d, 2, GDN_V_HEADS), math.log(1e-3), math.log(1e-1)))
    return {
        'x': normal((BATCH, SEQ, d), 1.0),
        'c': normal((BATCH, d), 1.0),
        'ctx': normal((BATCH, CTX_LEN, d), 1.0),
        'c_ctx': normal((d,), 1.0),
        'ada_w': normal((DEPTH, d, N_MOD * d), 0.5 * d ** -0.5),
        'ada_b': normal((DEPTH, N_MOD * d), 0.02),
        'norm1_g': gain((DEPTH, d)),
        'norm2_g': gain((DEPTH, d)),
        'router_w': normal((DEPTH, d, N_EXPERTS), d ** -0.5),
        'router_b': normal((DEPTH, N_EXPERTS), 0.01),
        'moe_w1': normal((DEPTH, N_EXPERTS, d, 2 * D_EXPERT), d ** -0.5),
        'moe_b1': normal((DEPTH, N_EXPERTS, 2 * D_EXPERT), 0.02),
        'moe_w2': normal((DEPTH, N_EXPERTS, D_EXPERT, d), D_EXPERT ** -0.5),
        'moe_b2': normal((DEPTH, N_EXPERTS, d), 0.02),
        'mla_w_in': normal((n_a, d, mla_in), d ** -0.5),
        'mla_q_norm_g': gain((n_a, MLA_Q_RANK)),
        'mla_kv_norm_g': gain((n_a, MLA_KV_RANK)),
        'mla_w_uq': normal((n_a, MLA_Q_RANK, MLA_HEADS * (MLA_NOPE + MLA_ROPE)), MLA_Q_RANK ** -0.5),
        'mla_w_ukv': normal((n_a, MLA_KV_RANK, MLA_HEADS * (MLA_NOPE + MLA_V)), MLA_KV_RANK ** -0.5),
        'mla_w_o': normal((n_a, MLA_HEADS * MLA_V, d), (MLA_HEADS * MLA_V) ** -0.5),
        'gqa_w_in': normal((n_b, d, gqa_in), d ** -0.5),
        'gqa_q_norm_g': gain((n_b, GQA_HEAD_DIM)),
        'gqa_k_norm_g': gain((n_b, GQA_HEAD_DIM)),
        'gqa_w_o': normal((n_b, GQA_Q_HEADS * GQA_HEAD_DIM, d), (GQA_Q_HEADS * GQA_HEAD_DIM) ** -0.5),
        'mlstm_w_in': normal((n_c, d, mlstm_in), d ** -0.5),
        'mlstm_gate_b': normal((n_c, 2, 2, MLSTM_HEADS), 0.1) + jnp.array([0.0, 3.0], jnp.float32)[:, None],
        'mlstm_norm_g': gain((n_c, MLSTM_V)),
        'mlstm_w_o': normal((n_c, MLSTM_HEADS * MLSTM_V, d), (MLSTM_HEADS * MLSTM_V) ** -0.5),
        'gdn_w_in': normal((n_d, d, gdn_in), d ** -0.5),
        'gdn_conv_w': normal((n_d, GDN_CONV, gdn_conv_ch), GDN_CONV ** -0.5),
        'gdn_a_log': jnp.log(uniform((n_d, 2, GDN_V_HEADS), 1.0, 16.0)),
        'gdn_dt_bias': dt + jnp.log(-jnp.expm1(-dt)),
        'gdn_norm_g': gain((n_d, GDN_HEAD_DIM)),
        'gdn_w_o': normal((n_d, GDN_V_HEADS * GDN_HEAD_DIM, d), (GDN_V_HEADS * GDN_HEAD_DIM) ** -0.5),
        'final_norm_g': gain((d,)),
    }


def reference(x, c, ctx, c_ctx, ada_w, ada_b, norm1_g, norm2_g, router_w, router_b,
              moe_w1, moe_b1, moe_w2, moe_b2,
              mla_w_in, mla_q_norm_g, mla_kv_norm_g, mla_w_uq, mla_w_ukv, mla_w_o,
              gqa_w_in, gqa_q_norm_g, gqa_k_norm_g, gqa_w_o,
              mlstm_w_in, mlstm_gate_b, mlstm_norm_g, mlstm_w_o,
              gdn_w_in, gdn_conv_w, gdn_a_log, gdn_dt_bias, gdn_norm_g, gdn_w_o,
              final_norm_g):
    x_lat, x_ctx = x, ctx
    b, s, d = x.shape
    n_ctx = ctx.shape[1]
    silu_c = jax.nn.silu(c)
    silu_c_ctx = jax.nn.silu(c_ctx)
    for i in range(DEPTH):
        last = i == DEPTH - 1
        kind, j = i % N_MIXERS, i // N_MIXERS
        mod_l = jnp.split((silu_c @ ada_w[i] + ada_b[i])[:, None, :], N_MOD, axis=-1)
        mod_c = jnp.split(silu_c_ctx @ ada_w[i] + ada_b[i], N_MOD, axis=-1)
        h_l = rms_norm(x_lat, norm1_g[i]) * (1.0 + mod_l[1]) + mod_l[0]
        h_c = rms_norm(x_ctx, norm1_g[i]) * (1.0 + mod_c[1]) + mod_c[0]
        if kind == 0:
            y_c, y_l = mla_mixer(h_c, h_l, mla_w_in[j], mla_q_norm_g[j], mla_kv_norm_g[j], mla_w_uq[j],
                                 mla_w_ukv[j], mla_w_o[j], not last)
        elif kind == 1:
            y_c, y_l = gqa_mixer(h_c, h_l, gqa_w_in[j], gqa_q_norm_g[j], gqa_k_norm_g[j], gqa_w_o[j], not last)
        elif kind == 2:
            y_c, y_l = mlstm_mixer(h_c, h_l, mlstm_w_in[j], mlstm_gate_b[j], mlstm_norm_g[j], mlstm_w_o[j], not last)
        else:
            y_c, y_l = gdn_mixer(h_c, h_l, gdn_w_in[j], gdn_conv_w[j], gdn_a_log[j], gdn_dt_bias[j],
                                 gdn_norm_g[j], gdn_w_o[j], not last)
        x_lat = x_lat + mod_l[2] * y_l
        h_l = (rms_norm(x_lat, norm2_g[i]) * (1.0 + mod_l[4]) + mod_l[3]).reshape(b * s, d)
        if last:
            y_l = moe_ffn(h_l, router_w[i], router_b[i], moe_w1[i], moe_b1[i], moe_w2[i], moe_b2[i])
        else:
            x_ctx = x_ctx + mod_c[2] * y_c
            h_c = (rms_norm(x_ctx, norm2_g[i]) * (1.0 + mod_c[4]) + mod_c[3]).reshape(b * n_ctx, d)
            y = moe_ffn(jnp.concatenate([h_c, h_l], axis=0), router_w[i], router_b[i], moe_w1[i], moe_b1[i],
                        moe_w2[i], moe_b2[i])
            x_ctx = x_ctx + mod_c[5] * y[:b * n_ctx].reshape(b, n_ctx, d)
            y_l = y[b * n_ctx:]
        x_lat = x_lat + mod_l[5] * y_l.reshape(b, s, d)
    return rms_norm(x_lat, final_norm_g)
```

```python
import functools
import math

import jax
import jax.numpy as jnp
from jax import lax
from jax.experimental import pallas as pl
from jax.experimental.pallas import tpu as pltpu

F32 = jnp.float32
BF16 = jnp.bfloat16
HIGHEST = lax.Precision.HIGHEST

D_MODEL = 1024
N_MOD = 6
GRID_W = 64
ROPE_BASE = 10000.0
NORM_EPS = 1e-6

MLA_HEADS = 16
MLA_Q_RANK = 512
MLA_KV_RANK = 256
MLA_NOPE = 64
MLA_ROPE = 32
MLA_V = 64

GQA_Q_HEADS = 8
GQA_KV_HEADS = 4
GQA_HEAD_DIM = 128

MLSTM_HEADS = 4
MLSTM_QK = 128
MLSTM_V = 256

GDN_QK_HEADS = 8
GDN_V_HEADS = 16
GDN_HEAD_DIM = 128
GDN_CONV = 5

N_EXPERTS = 32
TOP_K = 4
D_EXPERT = 1024
SWIGLU_ALPHA = 1.702
SWIGLU_LIMIT = 7.0

LANES = 128
ROW_TILE = 256
CHUNK = 64
Q_TILE = 256
GROUP_TILE = 512
RANK_TILE = 512
MOD_ROWS = 16
HALO = 16
VMEM_LIMIT = 56 * 1024 * 1024


def _params(sem):
    return pltpu.CompilerParams(dimension_semantics=sem, vmem_limit_bytes=VMEM_LIMIT)


class _Cfg:
    def __init__(self, b, s, nc):
        assert s % ROW_TILE == 0 and nc % ROW_TILE == 0 and (b * s) % nc == 0 and b + 1 <= MOD_ROWS
        self.b, self.s, self.nc = b, s, nc
        self.n_lat = b * s
        self.m = b * s + b * nc
        self.lat_tiles = self.n_lat // ROW_TILE
        self.tiles = self.m // ROW_TILE
        self.tpb = s // ROW_TILE
        self.cpb = nc // ROW_TILE


def _rms(x, g):
    return x * lax.rsqrt(jnp.mean(x * x, axis=-1, keepdims=True) + NORM_EPS) * g


def _norm_mod(x, g, scale, shift):
    return _rms(x, g) * (1.0 + scale) + shift


def _sigmoid(x):
    return 1.0 / (1.0 + jnp.exp(-x))


def _softplus(x):
    return jnp.maximum(x, 0.0) + jnp.log(1.0 + jnp.exp(-jnp.abs(x)))


def _mm(a, b):
    return jnp.dot(a.astype(BF16), b.astype(BF16), preferred_element_type=F32)


def _mm_nt(a, b):
    return lax.dot_general(a.astype(BF16), b.astype(BF16), (((1,), (1,)), ((), ())), preferred_element_type=F32)


def _mm_f32(a, b):
    return jnp.dot(a, b, preferred_element_type=F32, precision=HIGHEST)


def _ada_kernel(c_ref, w_ref, b_ref, o_ref):
    cond = c_ref[...]
    a = cond * _sigmoid(cond)
    o_ref[...] = _mm(a, w_ref[...]) + b_ref[...]


def _ada_table(cond, ada_w, ada_b):
    depth, d, nd = ada_w.shape
    return pl.pallas_call(
        _ada_kernel,
        grid=(depth, nd // d),
        in_specs=[pl.BlockSpec((MOD_ROWS, d), lambda l, n: (0, 0)),
                  pl.BlockSpec((None, d, d), lambda l, n: (l, 0, n)),
                  pl.BlockSpec((None, 1, d), lambda l, n: (l, 0, n))],
        out_specs=pl.BlockSpec((None, MOD_ROWS, d), lambda l, n: (l, 0, n)),
        out_shape=jax.ShapeDtypeStruct((depth, MOD_ROWS, nd), F32),
        compiler_params=_params(("parallel", "parallel")),
        name="ada_table",
    )(cond, ada_w, ada_b.reshape(depth, 1, nd))


def _row_call(body, cfg, rows, consts, mods, tabs, outs, n_tiles=None, name=None):
    n_tiles = cfg.tiles if n_tiles is None else n_tiles
    lat_tiles, tpb = cfg.lat_tiles, cfg.tpb
    arrays, specs = [], []
    for r in rows:
        if isinstance(r, tuple):
            arr, g = r
            arrays.append(arr)
            specs.append(pl.BlockSpec((None, ROW_TILE, arr.shape[-1]), lambda i, g=g: (g, i, 0)))
        else:
            arrays.append(r)
            specs.append(pl.BlockSpec((ROW_TILE, r.shape[-1]), lambda i: (i, 0)))
    for a in consts:
        arrays.append(a)
        specs.append(pl.BlockSpec(a.shape, lambda i, nd=a.ndim: (0,) * nd))
    for a in mods:
        arrays.append(a)
        specs.append(pl.BlockSpec((None, 1, a.shape[-1]),
                                  lambda i: (jnp.where(i < lat_tiles, 1 + i // tpb, 0), 0, 0)))
    for a in tabs:
        arrays.append(a)
        specs.append(pl.BlockSpec((None, ROW_TILE, a.shape[-1]),
                                  lambda i: (jnp.where(i < lat_tiles, 1 + i % tpb, 0), 0, 0)))
    out_shapes, out_specs = [], []
    for shape, dtype in outs:
        out_shapes.append(jax.ShapeDtypeStruct(shape, dtype))
        if shape[0] == TOP_K:
            out_specs.append(pl.BlockSpec((shape[0], ROW_TILE), lambda i: (0, i)))
        else:
            out_specs.append(pl.BlockSpec((ROW_TILE, shape[-1]), lambda i: (i, 0)))
    n_r, n_c, n_m, n_t = len(rows), len(consts), len(mods), len(tabs)

    def kern(*refs):
        r = refs[:n_r]
        c = refs[n_r:n_r + n_c]
        m = refs[n_r + n_c:n_r + n_c + n_m]
        t = refs[n_r + n_c + n_m:n_r + n_c + n_m + n_t]
        o = refs[n_r + n_c + n_m + n_t:]
        body(r, c, m, t, o)

    return pl.pallas_call(
        kern, grid=(n_tiles,), in_specs=specs, out_specs=out_specs, out_shape=out_shapes,
        compiler_params=_params(("parallel",)), name=name,
    )(*arrays)


def _residual_in(r, m, has_prev):
    if has_prev:
        return r[0][...] + m[0][...] * r[1][...]
    return r[0][...]


def _rope_pair(x, cos, sin_a, sin_b, shift):
    return x * cos + pltpu.roll(x, shift, 1) * sin_a + pltpu.roll(x, LANES - shift, 1) * sin_b


def _mla_pre_body(has_prev, r, c, m, t, o):
    g1, w_in, qg, kvg, wq, wk, wv, place = c
    shift1, scale1 = m[-2], m[-1]
    cos_q, sa_q, sb_q, cos_k, sa_k, sb_k = [a[...] for a in t]
    x = _residual_in(r, m, has_prev)
    h = _norm_mod(x, g1[...], scale1[...], shift1[...])
    z = _mm(h, w_in[...])
    cq = _rms(z[:, :MLA_Q_RANK], qg[...]).astype(BF16)
    ckv = _rms(z[:, MLA_Q_RANK:MLA_Q_RANK + MLA_KV_RANK], kvg[...]).astype(BF16)
    kr = z[:, MLA_Q_RANK + MLA_KV_RANK:]
    kr = _rope_pair(kr, cos_k, sa_k, sb_k, MLA_ROPE // 2)
    scale = (MLA_NOPE + MLA_ROPE) ** -0.5
    q = _mm(cq, wq[...])
    for hh in range(MLA_HEADS):
        qs = q[:, hh * LANES:(hh + 1) * LANES]
        qs = _rope_pair(qs, cos_q, sa_q, sb_q, MLA_ROPE // 2)
        o[1][:, hh * LANES:(hh + 1) * LANES] = (qs * scale).astype(BF16)
    o[0][...] = x
    o[2][...] = (_mm(ckv, wk[...]) + _mm(kr, place[...])).astype(BF16)
    o[3][...] = _mm(ckv, wv[...]).astype(BF16)


def _gqa_pre_body(has_prev, r, c, m, t, o):
    g1, w_in, qg, kg = c
    shift1, scale1 = m[-2], m[-1]
    cos, sin = t[0][...], t[1][...]
    x = _residual_in(r, m, has_prev)
    h = _norm_mod(x, g1[...], scale1[...], shift1[...])
    z = _mm(h, w_in[...])
    scale = GQA_HEAD_DIM ** -0.5
    hd = GQA_HEAD_DIM
    for hh in range(GQA_Q_HEADS + GQA_KV_HEADS):
        zs = z[:, hh * hd:(hh + 1) * hd]
        is_q = hh < GQA_Q_HEADS
        zs = _rms(zs, qg[...] if is_q else kg[...])
        zs = zs * cos + pltpu.roll(zs, hd // 2, 1) * sin
        if is_q:
            o[1][:, hh * hd:(hh + 1) * hd] = (zs * scale).astype(BF16)
        else:
            kk = hh - GQA_Q_HEADS
            o[2][:, kk * hd:(kk + 1) * hd] = zs.astype(BF16)
    o[0][...] = x
    o[3][...] = z[:, (GQA_Q_HEADS + GQA_KV_HEADS) * hd:].astype(BF16)


def _mlstm_pre_body(has_prev, r, c, m, t, o):
    g1, w_in, gate_b, fmask = c
    shift1, scale1 = m[-2], m[-1]
    x = _residual_in(r, m, has_prev)
    h = _norm_mod(x, g1[...], scale1[...], shift1[...])
    z = _mm(h, w_in[...])
    nqk = MLSTM_HEADS * MLSTM_QK
    nv = MLSTM_HEADS * MLSTM_V
    o[0][...] = x
    o[1][...] = z[:, :nqk].astype(BF16)
    o[2][...] = (z[:, nqk:2 * nqk] * MLSTM_QK ** -0.5).astype(BF16)
    o[3][...] = z[:, 2 * nqk:2 * nqk + nv].astype(BF16)
    o[4][...] = z[:, 2 * nqk + nv:2 * nqk + 2 * nv].astype(BF16)
    gt = z[:, 2 * nqk + 2 * nv:] + gate_b[...]
    log_sig = jnp.minimum(gt, 0.0) - jnp.log(1.0 + jnp.exp(-jnp.abs(gt)))
    o[5][...] = jnp.where(fmask[...] > 0.0, log_sig, gt)


def _gdn_pre_body(has_prev, r, c, m, t, o):
    g1, w_in, w_ba, a_scale, dt_b, dmask = c
    shift1, scale1 = m[-2], m[-1]
    x = _residual_in(r, m, has_prev)
    h = _norm_mod(x, g1[...], scale1[...], shift1[...]).astype(BF16)
    ncv = 2 * GDN_QK_HEADS * GDN_HEAD_DIM + GDN_V_HEADS * GDN_HEAD_DIM
    z = _mm(h, w_in[...])
    o[0][...] = x
    o[1][...] = z[:, :ncv].astype(BF16)
    o[2][...] = z[:, ncv:].astype(BF16)
    ba = _mm(h, w_ba[...])
    o[3][...] = jnp.where(dmask[...] > 0.0, -a_scale[...] * _softplus(ba + dt_b[...]), _sigmoid(ba))


def _route_tail(x, y, c, m, o):
    g2, rw_t, rb = c
    gate1, shift2, scale2 = m
    x1 = x + gate1[...] * y
    h2 = _norm_mod(x1, g2[...], scale2[...], shift2[...])
    o[0][...] = x1
    o[1][...] = h2.astype(BF16)
    logits = lax.dot_general(rw_t[...], h2, (((1,), (1,)), ((), ())), preferred_element_type=F32,
                             precision=HIGHEST) + rb[...]
    e_iota = lax.broadcasted_iota(jnp.int32, logits.shape, 0)
    vals = []
    for k in range(TOP_K):
        mx = jnp.max(logits, axis=0, keepdims=True)
        idx = jnp.min(jnp.where(logits == mx, e_iota, N_EXPERTS), axis=0, keepdims=True)
        o[2][k:k + 1, :] = idx
        vals.append(mx)
        logits = jnp.where(e_iota == idx, -jnp.inf, logits)
    es = [jnp.exp(v - vals[0]) for v in vals]
    tot = es[0] + es[1] + es[2] + es[3]
    for k in range(TOP_K):
        o[3][k:k + 1, :] = es[k] / tot


def _attn_post_body(r, c, m, t, o):
    x, a = r
    y = _mm(a[...], c[0][...])
    _route_tail(x[...], y, c[1:], m, o)


def _mlstm_post_body(r, c, m, t, o):
    x, hf, hb, og = r
    w_o, ng = c[0], c[1]
    hs = hf[...] + hb[...]
    gate = _sigmoid(og[...].astype(F32))
    parts = []
    for hh in range(MLSTM_HEADS):
        sl = slice(hh * MLSTM_V, (hh + 1) * MLSTM_V)
        parts.append((_rms(hs[:, sl], ng[...]) * gate[:, sl]).astype(BF16))
    y = _mm(jnp.concatenate(parts, axis=1), w_o[...])
    _route_tail(x[...], y, c[2:], m, o)


def _gdn_post_body(r, c, m, t, o):
    x, of, ob, zg = r
    w_o, ng = c[0], c[1]
    os_ = of[...] + ob[...]
    g = zg[...].astype(F32)
    gate = g * _sigmoid(g)
    parts = []
    for hh in range(GDN_V_HEADS):
        sl = slice(hh * GDN_HEAD_DIM, (hh + 1) * GDN_HEAD_DIM)
        parts.append((_rms(os_[:, sl], ng[...]) * gate[:, sl]).astype(BF16))
    y = _mm(jnp.concatenate(parts, axis=1), w_o[...])
    _route_tail(x[...], y, c[2:], m, o)


def _final_body(r, c, m, t, o):
    x = r[0][...] + m[0][...] * r[1][...]
    o[0][...] = _rms(x, c[0][...])


def _attn_kernel(*refs, n_seg, k_stride, v_width):
    q_ref = refs[0]
    segs = [(refs[1 + 2 * i], refs[2 + 2 * i]) for i in range(n_seg)]
    o_ref = refs[1 + 2 * n_seg]
    for j in range(2):
        q = q_ref[:, j * LANES:(j + 1) * LANES]
        ss = [_mm_nt(q, k_ref[:, j * k_stride:j * k_stride + LANES]) for k_ref, _ in segs]
        mx = ss[0].max(axis=-1, keepdims=True)
        for s in ss[1:]:
            mx = jnp.maximum(mx, s.max(axis=-1, keepdims=True))
        acc, den = 0.0, 0.0
        for s, (_, v_ref) in zip(ss, segs):
            p = jnp.exp(s - mx)
            den = den + p.sum(axis=-1, keepdims=True)
            voff = j * v_width if k_stride else 0
            acc = acc + _mm(p, v_ref[:, voff:voff + v_width])
        o_ref[:, j * v_width:(j + 1) * v_width] = (acc / den).astype(o_ref.dtype)


def _attention(cfg, q, k, v, n_groups, k_stride, v_width, latent):
    kw = 2 * LANES if k_stride else LANES
    vw = 2 * v_width if k_stride else v_width
    ow = 2 * v_width
    ctx_blk0 = cfg.n_lat // cfg.nc
    k_ctx = pl.BlockSpec((cfg.nc, kw), lambda b, g, i: (ctx_blk0 + b, g))
    v_ctx = pl.BlockSpec((cfg.nc, vw), lambda b, g, i: (ctx_blk0 + b, g))
    if latent:
        qpb = cfg.s // Q_TILE
        grid = (cfg.b, n_groups, qpb)
        q_spec = pl.BlockSpec((Q_TILE, 2 * LANES), lambda b, g, i: (b * qpb + i, g))
        o_spec = pl.BlockSpec((Q_TILE, ow), lambda b, g, i: (b * qpb + i, g))
        in_specs = [q_spec, pl.BlockSpec((cfg.s, kw), lambda b, g, i: (b, g)),
                    pl.BlockSpec((cfg.s, vw), lambda b, g, i: (b, g)), k_ctx, v_ctx]
        args = (q, k, v, k, v)
        n_seg = 2
        rows = cfg.n_lat
    else:
        qpb = cfg.nc // Q_TILE
        q0 = cfg.n_lat // Q_TILE
        grid = (cfg.b, n_groups, qpb)
        q_spec = pl.BlockSpec((Q_TILE, 2 * LANES), lambda b, g, i: (q0 + b * qpb + i, g))
        o_spec = pl.BlockSpec((Q_TILE, ow), lambda b, g, i: (b * qpb + i, g))
        in_specs = [q_spec, k_ctx, v_ctx]
        args = (q, k, v)
        n_seg = 1
        rows = cfg.b * cfg.nc
    return pl.pallas_call(
        functools.partial(_attn_kernel, n_seg=n_seg, k_stride=k_stride, v_width=v_width),
        grid=grid, in_specs=in_specs, out_specs=o_spec,
        out_shape=jax.ShapeDtypeStruct((rows, n_groups * ow), BF16),
        compiler_params=_params(("parallel", "parallel", "arbitrary")),
        name="attention_lat" if latent else "attention_ctx",
    )(*args)


def _chunk_masks(flip):
    li = lax.broadcasted_iota(jnp.int32, (CHUNK, CHUNK), 0)
    si = lax.broadcasted_iota(jnp.int32, (CHUNK, CHUNK), 1)
    eye = li == si
    incl = (si >= li) if flip else (si <= li)
    strict = (si > li) if flip else (si < li)
    return eye, incl, strict


def _to_row(col, eye):
    return jnp.sum(jnp.where(eye, col, 0.0), axis=0, keepdims=True)


def _cumsum_col(col, eye, incl):
    row = _to_row(col, eye)
    cum_col = jnp.sum(jnp.where(incl, row, 0.0), axis=1, keepdims=True)
    return cum_col, _to_row(cum_col, eye)


def _scan_block_index(cfg, flip):
    cpb, tpb, lat_tiles = cfg.cpb, cfg.tpb, cfg.lat_tiles

    def row_block(b, j):
        if flip:
            ctx = lat_tiles + b * cpb + (cpb - 1 - j)
            lat = b * tpb + (tpb - 1 - (j - cpb))
        else:
            ctx = lat_tiles + b * cpb + j
            lat = b * tpb + (j - cpb)
        return jnp.where(j < cpb, ctx, lat)

    return row_block


def _mlstm_kernel(q_ref, k_ref, v_ref, g_ref, o_ref, c_st, n_st, m_st, *, flip, d):
    @pl.when(pl.program_id(1) == 0)
    def _():
        c_st[...] = jnp.zeros_like(c_st)
        n_st[...] = jnp.zeros_like(n_st)
        m_st[...] = jnp.zeros_like(m_st)

    eye, incl, _ = _chunk_masks(flip)
    n_chunks = ROW_TILE // CHUNK
    for ci in range(n_chunks):
        cc = n_chunks - 1 - ci if flip else ci
        rows = slice(cc * CHUNK, (cc + 1) * CHUNK)
        for hh in range(MLSTM_HEADS):
            q = q_ref[rows, hh * MLSTM_QK:(hh + 1) * MLSTM_QK]
            k = k_ref[rows, hh * MLSTM_QK:(hh + 1) * MLSTM_QK]
            v = v_ref[rows, hh * MLSTM_V:(hh + 1) * MLSTM_V]
            icol = d * 2 * MLSTM_HEADS + hh
            fcol = icol + MLSTM_HEADS
            i_col = g_ref[rows, icol:icol + 1]
            f_col = g_ref[rows, fcol:fcol + 1]
            c_prev, n_prev, m_prev = c_st[hh], n_st[hh], m_st[hh]

            i_row = _to_row(i_col, eye)
            bcum_col, bcum_row = _cumsum_col(f_col, eye, incl)
            b_end = jnp.sum(f_col, axis=0, keepdims=True)
            w_log = b_end - bcum_col + i_col
            m_new = jnp.maximum(b_end + m_prev, jnp.max(w_log, axis=0, keepdims=True))
            carry = jnp.exp(b_end + m_prev - m_new)
            kw = k.astype(F32) * jnp.exp(w_log - m_new)

            g_log = bcum_col + m_prev
            d_log = jnp.where(incl, bcum_col - bcum_row + i_row, -jnp.inf)
            m_t = jnp.maximum(g_log, jnp.max(d_log, axis=1, keepdims=True))
            inter = jnp.exp(g_log - m_t)
            s = _mm_nt(q, k) * jnp.exp(d_log - m_t)
            num = inter * _mm(q, c_prev) + _mm(s, v)
            den = inter * jnp.sum(q.astype(F32) * n_prev, axis=1, keepdims=True) + jnp.sum(s, axis=1, keepdims=True)
            h_out = num / jnp.maximum(jnp.abs(den), jnp.exp(-m_t))
            o_ref[rows, hh * MLSTM_V:(hh + 1) * MLSTM_V] = h_out

            c_st[hh] = carry * c_prev + _mm(kw.T, v)
            n_st[hh] = carry * n_prev + jnp.sum(kw, axis=0, keepdims=True)
            m_st[hh] = m_new


def _mlstm_scan(cfg, q, k, v, gates):
    nqk, nv = MLSTM_HEADS * MLSTM_QK, MLSTM_HEADS * MLSTM_V
    outs = []
    for d in range(2):
        flip = d == 1
        rb = _scan_block_index(cfg, flip)
        row = lambda w, rb=rb: pl.BlockSpec((ROW_TILE, w), lambda b, j: (rb(b, j), 0))
        outs.append(pl.pallas_call(
            functools.partial(_mlstm_kernel, flip=flip, d=d),
            grid=(cfg.b, cfg.cpb + cfg.tpb),
            in_specs=[row(nqk), row(nqk), row(nv), row(LANES)],
            out_specs=row(nv),
            out_shape=jax.ShapeDtypeStruct((cfg.m, nv), F32),
            scratch_shapes=[pltpu.VMEM((MLSTM_HEADS, MLSTM_QK, MLSTM_V), F32),
                            pltpu.VMEM((MLSTM_HEADS, 1, MLSTM_QK), F32),
                            pltpu.VMEM((MLSTM_HEADS, 1, 1), F32)],
            compiler_params=_params(("parallel", "arbitrary")),
            name="mlstm_bwd" if flip else "mlstm_fwd",
        )(q, k, v, gates))
    return outs


def _unit_lower_inverse(mkk, eye):
    eye_f = eye.astype(F32)
    inv = eye_f - mkk
    pw = mkk
    steps = int(math.log2(CHUNK)) - 1
    for _ in range(steps):
        pw = _mm_f32(pw, pw)
        inv = inv + _mm_f32(inv, pw)
    return inv


def _gdn_kernel(q_ref, k_ref, v_ref, ba_ref, o_ref, s_st, *, flip, d):
    @pl.when(pl.program_id(2) == 0)
    def _():
        s_st[...] = jnp.zeros_like(s_st)

    eye, incl, strict = _chunk_masks(flip)
    rep = GDN_V_HEADS // GDN_QK_HEADS
    hd = GDN_HEAD_DIM
    n_chunks = ROW_TILE // CHUNK
    for ci in range(n_chunks):
        cc = n_chunks - 1 - ci if flip else ci
        rows = slice(cc * CHUNK, (cc + 1) * CHUNK)
        q = q_ref[rows, :]
        k = k_ref[rows, :]
        kf = k.astype(F32)
        qk = _mm_nt(q, k)
        for jj in range(rep):
            v = v_ref[rows, jj * hd:(jj + 1) * hd].astype(F32)
            bcol = d * 2 * rep + jj
            beta = ba_ref[rows, bcol:bcol + 1]
            ld = ba_ref[rows, bcol + rep:bcol + rep + 1]
            s_prev = s_st[jj]

            gc_col, gc_row = _cumsum_col(ld, eye, incl)
            gc_end = jnp.sum(ld, axis=0, keepdims=True)
            decay = jnp.exp(jnp.where(incl, gc_col - gc_row, -jnp.inf))
            kb = kf * beta
            mkk = jnp.where(strict, _mm_nt(kb, k) * decay, 0.0)
            inv = _unit_lower_inverse(mkk, eye)
            eg = jnp.exp(gc_col)
            u = _mm_f32(inv, v * beta)
            w = _mm_f32(inv, kb * eg)
            kdec = kf * jnp.exp(gc_end - gc_col)
            v_new = u - _mm(w, s_prev)
            o_ref[rows, jj * hd:(jj + 1) * hd] = _mm(q.astype(F32) * eg, s_prev) + _mm(qk * decay, v_new)
            s_st[jj] = s_prev * jnp.exp(gc_end) + _mm(kdec.T, v_new)


def _gdn_scan(cfg, qkv, ba):
    rep = GDN_V_HEADS // GDN_QK_HEADS
    hd = GDN_HEAD_DIM
    outs = []
    for d in range(2):
        flip = d == 1
        rb = _scan_block_index(cfg, flip)
        col = lambda w, off, rb=rb: pl.BlockSpec((ROW_TILE, w), lambda b, h, j: (rb(b, j), off + h))
        outs.append(pl.pallas_call(
            functools.partial(_gdn_kernel, flip=flip, d=d),
            grid=(cfg.b, GDN_QK_HEADS, cfg.cpb + cfg.tpb),
            in_specs=[col(hd, 0), col(hd, GDN_QK_HEADS), col(rep * hd, 2 * GDN_QK_HEADS // rep), col(LANES, 0)],
            out_specs=col(rep * hd, 0),
            out_shape=jax.ShapeDtypeStruct((cfg.m, GDN_V_HEADS * hd), F32),
            scratch_shapes=[pltpu.VMEM((rep, hd, hd), F32)],
            compiler_params=_params(("parallel", "parallel", "arbitrary")),
            name="gdn_bwd" if flip else "gdn_fwd",
        )(qkv, qkv, qkv, ba))
    return outs


def _gdn_conv_kernel(cur_ref, prev_ref, next_ref, w_ref, o_ref, pad_ref, *, lat_tiles, tpb, cpb):
    i = pl.program_id(0)
    cb = pl.program_id(1)
    lat = i < lat_tiles
    pos = jnp.where(lat, i % tpb, (i - lat_tiles) % cpb)
    per = jnp.where(lat, tpb, cpb)
    first = pos == 0
    last = pos == per - 1
    prev = prev_ref[...].astype(F32)
    nxt = next_ref[...].astype(F32)
    pad_ref[0:HALO, :] = jnp.where(first, 0.0, prev)
    pad_ref[HALO:HALO + ROW_TILE, :] = cur_ref[...].astype(F32)
    pad_ref[HALO + ROW_TILE:, :] = jnp.where(last, 0.0, nxt)
    half = GDN_CONV // 2
    acc = 0.0
    for tap in range(GDN_CONV):
        off = HALO - half + tap
        acc = acc + pad_ref[off:off + ROW_TILE, :] * w_ref[tap:tap + 1, :]
    act = acc * _sigmoid(acc)
    qscale = jnp.where(cb == 0, GDN_HEAD_DIM ** -0.5, 1.0)
    hd = GDN_HEAD_DIM
    for hh in range(act.shape[1] // hd):
        a = act[:, hh * hd:(hh + 1) * hd]
        nrm = a * lax.rsqrt(jnp.sum(a * a, axis=-1, keepdims=True) + NORM_EPS) * qscale
        o_ref[:, hh * hd:(hh + 1) * hd] = jnp.where(cb < 2, nrm, a).astype(o_ref.dtype)


def _gdn_conv(cfg, zqkv, conv_w):
    cw = GDN_QK_HEADS * GDN_HEAD_DIM
    n_cb = zqkv.shape[1] // cw
    hpt = ROW_TILE // HALO
    n_halo = cfg.m // HALO
    return pl.pallas_call(
        functools.partial(_gdn_conv_kernel, lat_tiles=cfg.lat_tiles, tpb=cfg.tpb, cpb=cfg.cpb),
        grid=(cfg.tiles, n_cb),
        in_specs=[pl.BlockSpec((ROW_TILE, cw), lambda i, c: (i, c)),
                  pl.BlockSpec((HALO, cw), lambda i, c: (jnp.maximum(i * hpt - 1, 0), c)),
                  pl.BlockSpec((HALO, cw), lambda i, c: (jnp.minimum((i + 1) * hpt, n_halo - 1), c)),
                  pl.BlockSpec((8, cw), lambda i, c: (0, c))],
        out_specs=pl.BlockSpec((ROW_TILE, cw), lambda i, c: (i, c)),
        out_shape=jax.ShapeDtypeStruct(zqkv.shape, BF16),
        scratch_shapes=[pltpu.VMEM((ROW_TILE + 2 * HALO, cw), F32)],
        compiler_params=_params(("parallel", "parallel")),
        name="gdn_conv",
    )(zqkv, zqkv, zqkv, conv_w)


def _rank_kernel(e_ref, rank_ref, cnt_ref, run_ref):
    @pl.when(pl.program_id(0) == 0)
    def _():
        run_ref[...] = jnp.zeros_like(run_ref)

    e = e_ref[...]
    onehot = (lax.broadcasted_iota(jnp.int32, (N_EXPERTS, RANK_TILE), 0) == e)
    oh = jnp.where(onehot, 1.0, 0.0)
    ji = lax.broadcasted_iota(jnp.int32, (RANK_TILE, RANK_TILE), 0)
    si = lax.broadcasted_iota(jnp.int32, (RANK_TILE, RANK_TILE), 1)
    upper = jnp.where(ji <= si, 1.0, 0.0).astype(BF16)
    cum = jnp.dot(oh.astype(BF16), upper, preferred_element_type=F32)
    run = run_ref[:, 0:1]
    rank = jnp.sum(oh * (cum - 1.0 + run), axis=0, keepdims=True)
    rank_ref[...] = rank.astype(jnp.int32)
    run_ref[...] = run_ref[...] + jnp.sum(oh, axis=1, keepdims=True)
    cnt_ref[...] = run_ref[...]


def _rank_pairs(expert_of_pair):
    n_pairs = expert_of_pair.shape[1]
    return pl.pallas_call(
        _rank_kernel,
        grid=(n_pairs // RANK_TILE,),
        in_specs=[pl.BlockSpec((1, RANK_TILE), lambda i: (0, i))],
        out_specs=[pl.BlockSpec((1, RANK_TILE), lambda i: (0, i)),
                   pl.BlockSpec((N_EXPERTS, LANES), lambda i: (0, 0))],
        out_shape=[jax.ShapeDtypeStruct((1, n_pairs), jnp.int32),
                   jax.ShapeDtypeStruct((N_EXPERTS, LANES), F32)],
        scratch_shapes=[pltpu.VMEM((N_EXPERTS, LANES), F32)],
        compiler_params=_params(("arbitrary",)),
        name="moe_rank",
    )(expert_of_pair)


def _moe_kernel(te_ref, nu_ref, x_ref, w1g_ref, w1l_ref, b1g_ref, b1l_ref, w2_ref, b2_ref, o_ref):
    used = pl.program_id(0) < nu_ref[0]

    @pl.when(used)
    def _():
        x = x_ref[...]
        glu = jnp.minimum(jnp.dot(x, w1g_ref[...], preferred_element_type=F32) + b1g_ref[...], SWIGLU_LIMIT)
        lin = jnp.clip(jnp.dot(x, w1l_ref[...], preferred_element_type=F32) + b1l_ref[...],
                       -SWIGLU_LIMIT, SWIGLU_LIMIT)
        act = glu * _sigmoid(SWIGLU_ALPHA * glu) * (lin + 1.0)
        y = jnp.dot(act.astype(BF16), w2_ref[...], preferred_element_type=F32) + b2_ref[...]
        o_ref[...] = y.astype(o_ref.dtype)

    @pl.when(jnp.logical_not(used))
    def _():
        o_ref[...] = jnp.zeros_like(o_ref)


def _moe_grouped(xs, tile_expert, n_used, w1g, w1l, b1g, b1l, w2, b2):
    n_slots, d = xs.shape
    de = w1g.shape[-1]
    wspec = lambda r, c: pl.BlockSpec((None, r, c), lambda j, te, nu: (te[j], 0, 0))
    return pl.pallas_call(
        _moe_kernel,
        grid_spec=pltpu.PrefetchScalarGridSpec(
            num_scalar_prefetch=2,
            grid=(n_slots // GROUP_TILE,),
            in_specs=[pl.BlockSpec((GROUP_TILE, d), lambda j, te, nu: (j, 0)),
                      wspec(d, de), wspec(d, de), wspec(1, de), wspec(1, de), wspec(de, d), wspec(1, d)],
            out_specs=pl.BlockSpec((GROUP_TILE, d), lambda j, te, nu: (j, 0))),
        out_shape=jax.ShapeDtypeStruct((n_slots, d), BF16),
        compiler_params=_params(("arbitrary",)),
        name="moe_grouped",
    )(tile_expert, n_used, xs, w1g, w1l, b1g, b1l, w2, b2)


def _moe_ffn(h2, topi, topw, w1, b1, w2, b2):
    n_tok = h2.shape[0]
    n_pairs = TOP_K * n_tok
    assert n_pairs % RANK_TILE == 0
    n_slots = -(-(n_pairs + N_EXPERTS * (GROUP_TILE - 1)) // GROUP_TILE) * GROUP_TILE
    n_tiles = n_slots // GROUP_TILE

    eid = topi.reshape(1, n_pairs)
    rank, counts = _rank_pairs(eid)
    counts = counts[:, 0].astype(jnp.int32)
    padded = (counts + GROUP_TILE - 1) // GROUP_TILE * GROUP_TILE
    ends = jnp.cumsum(padded)
    starts = ends - padded
    slot = starts[eid[0]] + rank[0]
    token = jnp.tile(jnp.arange(n_tok, dtype=jnp.int32), TOP_K)
    src = jnp.zeros((n_slots,), jnp.int32).at[slot].set(token)
    tile_start = jnp.arange(n_tiles, dtype=jnp.int32) * GROUP_TILE
    tile_expert = jnp.minimum(jnp.searchsorted(ends, tile_start, side="right"), N_EXPERTS - 1).astype(jnp.int32)
    n_used = (ends[-1] // GROUP_TILE).astype(jnp.int32).reshape(1)

    xs = jnp.take(h2, src, axis=0)
    w1g = w1[:, :, 0::2].astype(BF16)
    w1l = w1[:, :, 1::2].astype(BF16)
    b1g = b1[:, None, 0::2]
    b1l = b1[:, None, 1::2]
    ys = _moe_grouped(xs, tile_expert, n_used, w1g, w1l, b1g, b1l, w2.astype(BF16), b2[:, None, :])

    pos = slot.reshape(TOP_K, n_tok)
    y = jnp.zeros((n_tok, h2.shape[1]), F32)
    for k in range(TOP_K):
        y = y + topw[k][:, None] * jnp.take(ys, pos[k], axis=0).astype(F32)
    return y


def _axial_angles(n_tokens, rot_dim):
    t = jnp.arange(n_tokens, dtype=jnp.int32)
    rows = (t // GRID_W).astype(F32)
    cols = (t % GRID_W).astype(F32)
    d_axis = rot_dim // 2
    inv_freq = ROPE_BASE ** (-jnp.arange(0, d_axis, 2, dtype=F32) / d_axis)
    ang = jnp.concatenate([rows[:, None] * inv_freq, cols[:, None] * inv_freq], axis=-1)
    return jnp.cos(ang), jnp.sin(ang)


def _seq_table(cfg, lat_rows, ctx_value):
    c = lat_rows.shape[-1]
    ctx = jnp.full((1, ROW_TILE, c), ctx_value, F32)
    return jnp.concatenate([ctx, lat_rows.reshape(cfg.tpb, ROW_TILE, c)], axis=0)


def _slab_rope_tables(cfg, rot_dim, lane0):
    cos, sin = _axial_angles(cfg.s, rot_dim)
    half = rot_dim // 2
    s = cfg.s
    ones = lambda n: jnp.ones((s, n), F32)
    zeros = lambda n: jnp.zeros((s, n), F32)
    tail = LANES - lane0 - rot_dim
    cos_t = jnp.concatenate([ones(lane0), cos, cos, ones(tail)], axis=1)
    sin_a = jnp.concatenate([zeros(lane0 + half), sin, zeros(tail)], axis=1)
    sin_b = jnp.concatenate([zeros(lane0), -sin, zeros(half + tail)], axis=1)
    return _seq_table(cfg, cos_t, 1.0), _seq_table(cfg, sin_a, 0.0), _seq_table(cfg, sin_b, 0.0)


def _mod_slices(table, n_rows):
    d = D_MODEL
    return [table[:, k * d:(k + 1) * d].reshape(n_rows, 1, d) for k in range(N_MOD)]


def kernel(x, c, ctx, c_ctx, ada_w, ada_b, norm1_g, norm2_g, router_w, router_b, moe_w1, moe_b1, moe_w2, moe_b2,
           mla_w_in, mla_q_norm_g, mla_kv_norm_g, mla_w_uq, mla_w_ukv, mla_w_o, gqa_w_in, gqa_q_norm_g,
           gqa_k_norm_g, gqa_w_o, mlstm_w_in, mlstm_gate_b, mlstm_norm_g, mlstm_w_o, gdn_w_in, gdn_conv_w,
           gdn_a_log, gdn_dt_bias, gdn_norm_g, gdn_w_o, final_norm_g):
    b, s, d = x.shape
    nc = ctx.shape[1]
    depth = ada_w.shape[0]
    cfg = _Cfg(b, s, nc)
    m = cfg.m
    row = lambda a: a.reshape(1, -1).astype(F32)

    cond = jnp.concatenate([c_ctx[None, :], c, jnp.zeros((MOD_ROWS - 1 - b, d), F32)], axis=0)
    mod_all = _ada_table(cond, ada_w, ada_b)

    stream = jnp.concatenate([x.reshape(b * s, d), ctx.reshape(b * nc, d)], axis=0)
    y_prev = None
    gate2_prev = None
    for i in range(depth):
        kind, j = i % 4, i // 4
        shift1, scale1, gate1, shift2, scale2, gate2 = _mod_slices(mod_all[i], MOD_ROWS)
        has_prev = y_prev is not None
        rows_in = [stream, y_prev] if has_prev else [stream]
        mods_in = ([gate2_prev] if has_prev else []) + [shift1, scale1]
        g1 = row(norm1_g[i])
        route_consts = [row(norm2_g[i]), router_w[i].T, router_b[i].reshape(N_EXPERTS, 1)]
        route_mods = [gate1, shift2, scale2]
        route_outs = [((m, d), F32), ((m, d), BF16), ((TOP_K, m), jnp.int32), ((TOP_K, m), F32)]

        if kind == 0:
            qk = MLA_NOPE + MLA_ROPE
            w_in = jnp.pad(mla_w_in[j], ((0, 0), (0, LANES - MLA_ROPE))).astype(BF16)
            wq = jnp.pad(mla_w_uq[j].reshape(MLA_Q_RANK, MLA_HEADS, qk), ((0, 0), (0, 0), (0, LANES - qk)))
            wq = wq.reshape(MLA_Q_RANK, MLA_HEADS * LANES).astype(BF16)
            wkv = mla_w_ukv[j].reshape(MLA_KV_RANK, MLA_HEADS, MLA_NOPE + MLA_V)
            wk = jnp.pad(wkv[:, :, :MLA_NOPE], ((0, 0), (0, 0), (0, LANES - MLA_NOPE)))
            wk = wk.reshape(MLA_KV_RANK, MLA_HEADS * LANES).astype(BF16)
            wv = wkv[:, :, MLA_NOPE:].reshape(MLA_KV_RANK, MLA_HEADS * MLA_V).astype(BF16)
            place = jnp.zeros((LANES, LANES), F32).at[jnp.arange(MLA_ROPE), MLA_NOPE + jnp.arange(MLA_ROPE)].set(1.0)
            place = jnp.tile(place, (1, MLA_HEADS)).astype(BF16)
            tabs = list(_slab_rope_tables(cfg, MLA_ROPE, MLA_NOPE)) + list(_slab_rope_tables(cfg, MLA_ROPE, 0))
            stream, q, k, v = _row_call(
                functools.partial(_mla_pre_body, has_prev), cfg, rows_in,
                [g1, w_in, row(mla_q_norm_g[j]), row(mla_kv_norm_g[j]), wq, wk, wv, place], mods_in, tabs,
                [((m, d), F32), ((m, MLA_HEADS * LANES), BF16), ((m, MLA_HEADS * LANES), BF16),
                 ((m, MLA_HEADS * MLA_V), BF16)], name="mla_pre")
            a_lat = _attention(cfg, q, k, v, MLA_HEADS // 2, LANES, MLA_V, True)
            a_ctx = _attention(cfg, q, k, v, MLA_HEADS // 2, LANES, MLA_V, False)
            a = jnp.concatenate([a_lat, a_ctx], axis=0)
            stream, h2, topi, topw = _row_call(
                _attn_post_body, cfg, [stream, a], [mla_w_o[j].astype(BF16)] + route_consts, route_mods, [],
                route_outs, name="mla_post")
        elif kind == 1:
            cos, sin = _axial_angles(cfg.s, GQA_HEAD_DIM)
            tabs = [_seq_table(cfg, jnp.concatenate([cos, cos], axis=1), 1.0),
                    _seq_table(cfg, jnp.concatenate([-sin, sin], axis=1), 0.0)]
            stream, q, k, v = _row_call(
                functools.partial(_gqa_pre_body, has_prev), cfg, rows_in,
                [g1, gqa_w_in[j].astype(BF16), row(gqa_q_norm_g[j]), row(gqa_k_norm_g[j])], mods_in, tabs,
                [((m, d), F32), ((m, GQA_Q_HEADS * GQA_HEAD_DIM), BF16), ((m, GQA_KV_HEADS * GQA_HEAD_DIM), BF16),
                 ((m, GQA_KV_HEADS * GQA_HEAD_DIM), BF16)], name="gqa_pre")
            a_lat = _attention(cfg, q, k, v, GQA_KV_HEADS, 0, GQA_HEAD_DIM, True)
            a_ctx = _attention(cfg, q, k, v, GQA_KV_HEADS, 0, GQA_HEAD_DIM, False)
            a = jnp.concatenate([a_lat, a_ctx], axis=0)
            stream, h2, topi, topw = _row_call(
                _attn_post_body, cfg, [stream, a], [gqa_w_o[j].astype(BF16)] + route_consts, route_mods, [],
                route_outs, name="gqa_post")
        elif kind == 2:
            n_gate = 4 * MLSTM_HEADS
            w_in = jnp.pad(mlstm_w_in[j], ((0, 0), (0, LANES - n_gate))).astype(BF16)
            gate_b = jnp.pad(mlstm_gate_b[j].reshape(1, n_gate).astype(F32), ((0, 0), (0, LANES - n_gate)))
            col = jnp.arange(LANES)
            fmask = (((col // MLSTM_HEADS) % 2 == 1) & (col < n_gate)).astype(F32).reshape(1, LANES)
            nqk, nv = MLSTM_HEADS * MLSTM_QK, MLSTM_HEADS * MLSTM_V
            stream, q, k, v, og, gates = _row_call(
                functools.partial(_mlstm_pre_body, has_prev), cfg, rows_in, [g1, w_in, gate_b, fmask], mods_in, [],
                [((m, d), F32), ((m, nqk), BF16), ((m, nqk), BF16), ((m, nv), BF16), ((m, nv), BF16),
                 ((m, LANES), F32)], name="mlstm_pre")
            hf, hb = _mlstm_scan(cfg, q, k, v, gates)
            stream, h2, topi, topw = _row_call(
                _mlstm_post_body, cfg, [stream, hf, hb, og],
                [mlstm_w_o[j].astype(BF16), row(mlstm_norm_g[j])] + route_consts, route_mods, [], route_outs,
                name="mlstm_post")
        else:
            rep = GDN_V_HEADS // GDN_QK_HEADS
            qk_w = GDN_QK_HEADS * GDN_HEAD_DIM
            v_w = GDN_V_HEADS * GDN_HEAD_DIM
            ncv = 2 * qk_w + v_w
            w_main = gdn_w_in[j][:, :ncv + v_w].astype(BF16)
            w_ba = gdn_w_in[j][:, ncv + v_w:].reshape(d, 2, 2, GDN_QK_HEADS, rep).transpose(0, 3, 1, 2, 4)
            w_ba = jnp.pad(w_ba.reshape(d, GDN_QK_HEADS, 4 * rep), ((0, 0), (0, 0), (0, LANES - 4 * rep)))
            w_ba = w_ba.reshape(d, GDN_QK_HEADS * LANES).astype(BF16)

            def per_lane(p, fill):
                pv = p.astype(F32).reshape(2, GDN_QK_HEADS, rep).transpose(1, 0, 2)
                full = jnp.full((GDN_QK_HEADS, 2, 2, rep), fill, F32).at[:, :, 1, :].set(pv)
                full = jnp.pad(full.reshape(GDN_QK_HEADS, 4 * rep), ((0, 0), (0, LANES - 4 * rep)),
                               constant_values=fill)
                return full.reshape(1, GDN_QK_HEADS * LANES)

            a_scale = per_lane(jnp.exp(gdn_a_log[j].astype(F32)), 0.0)
            dt_b = per_lane(gdn_dt_bias[j], 0.0)
            dmask = per_lane(jnp.ones((2, GDN_V_HEADS), F32), 0.0)
            stream, zqkv, zg, ba = _row_call(
                functools.partial(_gdn_pre_body, has_prev), cfg, rows_in, [g1, w_main, w_ba, a_scale, dt_b, dmask],
                mods_in, [], [((m, d), F32), ((m, ncv), BF16), ((m, v_w), BF16), ((m, GDN_QK_HEADS * LANES), F32)],
                name="gdn_pre")
            conv_w = jnp.pad(gdn_conv_w[j].astype(F32), ((0, 8 - GDN_CONV), (0, 0)))
            qkv = _gdn_conv(cfg, zqkv, conv_w)
            of, ob = _gdn_scan(cfg, qkv, ba)
            stream, h2, topi, topw = _row_call(
                _gdn_post_body, cfg, [stream, of, ob, zg],
                [gdn_w_o[j].astype(BF16), row(gdn_norm_g[j])] + route_consts, route_mods, [], route_outs,
                name="gdn_post")

        y_prev = _moe_ffn(h2, topi, topw, moe_w1[i], moe_b1[i], moe_w2[i], moe_b2[i])
        gate2_prev = gate2

    out = _row_call(_final_body, cfg, [stream, y_prev], [row(final_norm_g)], [gate2_prev], [],
                    [((cfg.n_lat, d), F32)], n_tiles=cfg.lat_tiles, name="final_norm")[0]
    return out.reshape(b, s, d)
```

```python
import functools
import math

import jax
import jax.numpy as jnp
from jax import lax
from jax.experimental import pallas as pl
from jax.experimental.pallas import tpu as pltpu

F32 = jnp.float32
BF16 = jnp.bfloat16
HIGHEST = lax.Precision.HIGHEST

D_MODEL = 1024
N_MOD = 6
GRID_W = 64
ROPE_BASE = 10000.0
NORM_EPS = 1e-6

MLA_HEADS = 16
MLA_Q_RANK = 512
MLA_KV_RANK = 256
MLA_NOPE = 64
MLA_ROPE = 32
MLA_V = 64

GQA_Q_HEADS = 8
GQA_KV_HEADS = 4
GQA_HEAD_DIM = 128

MLSTM_HEADS = 4
MLSTM_QK = 128
MLSTM_V = 256

GDN_QK_HEADS = 8
GDN_V_HEADS = 16
GDN_HEAD_DIM = 128
GDN_CONV = 5

N_EXPERTS = 32
TOP_K = 4
D_EXPERT = 1024
SWIGLU_ALPHA = 1.702
SWIGLU_LIMIT = 7.0

LANES = 128
ROW_TILE = 256
CHUNK = 64
Q_TILE = 256
GROUP_TILE = 512
RANK_TILE = 512
MOD_ROWS = 16
HALO = 16
VMEM_LIMIT = 56 * 1024 * 1024


def _params(sem):
    return pltpu.CompilerParams(dimension_semantics=sem, vmem_limit_bytes=VMEM_LIMIT)


class _Cfg:
    def __init__(self, b, s, nc):
        assert s % ROW_TILE == 0 and nc % ROW_TILE == 0 and (b * s) % nc == 0 and b + 1 <= MOD_ROWS
        self.b, self.s, self.nc = b, s, nc
        self.n_lat = b * s
        self.m = b * s + b * nc
        self.lat_tiles = self.n_lat // ROW_TILE
        self.tiles = self.m // ROW_TILE
        self.tpb = s // ROW_TILE
        self.cpb = nc // ROW_TILE


def _rms(x, g):
    return x * lax.rsqrt(jnp.mean(x * x, axis=-1, keepdims=True) + NORM_EPS) * g


def _norm_mod(x, g, scale, shift):
    return _rms(x, g) * (1.0 + scale) + shift


def _sigmoid(x):
    return 1.0 / (1.0 + jnp.exp(-x))


def _softplus(x):
    return jnp.maximum(x, 0.0) + jnp.log(1.0 + jnp.exp(-jnp.abs(x)))


def _mm(a, b):
    return jnp.dot(a.astype(BF16), b.astype(BF16), preferred_element_type=F32)


def _mm_nt(a, b):
    return lax.dot_general(a.astype(BF16), b.astype(BF16), (((1,), (1,)), ((), ())), preferred_element_type=F32)


def _ada_kernel(c_ref, w_ref, b_ref, o_ref):
    cond = c_ref[...]
    a = cond * _sigmoid(cond)
    o_ref[...] = _mm(a, w_ref[...]) + b_ref[...]


def _ada_table(cond, ada_w, ada_b):
    depth, d, nd = ada_w.shape
    return pl.pallas_call(
        _ada_kernel,
        grid=(depth, nd // d),
        in_specs=[pl.BlockSpec((MOD_ROWS, d), lambda l, n: (0, 0)),
                  pl.BlockSpec((None, d, d), lambda l, n: (l, 0, n)),
                  pl.BlockSpec((None, 1, d), lambda l, n: (l, 0, n))],
        out_specs=pl.BlockSpec((None, MOD_ROWS, d), lambda l, n: (l, 0, n)),
        out_shape=jax.ShapeDtypeStruct((depth, MOD_ROWS, nd), F32),
        compiler_params=_params(("parallel", "parallel")),
        name="ada_table",
    )(cond, ada_w, ada_b.reshape(depth, 1, nd))


def _row_call(body, cfg, rows, consts, mods, tabs, outs, n_tiles=None, name=None):
    n_tiles = cfg.tiles if n_tiles is None else n_tiles
    lat_tiles, tpb = cfg.lat_tiles, cfg.tpb
    arrays, specs = [], []
    for r in rows:
        if isinstance(r, tuple):
            arr, g = r
            arrays.append(arr)
            specs.append(pl.BlockSpec((None, ROW_TILE, arr.shape[-1]), lambda i, g=g: (g, i, 0)))
        else:
            arrays.append(r)
            specs.append(pl.BlockSpec((ROW_TILE, r.shape[-1]), lambda i: (i, 0)))
    for a in consts:
        arrays.append(a)
        specs.append(pl.BlockSpec(a.shape, lambda i, nd=a.ndim: (0,) * nd))
    for a in mods:
        arrays.append(a)
        specs.append(pl.BlockSpec((None, 1, a.shape[-1]),
                                  lambda i: (jnp.where(i < lat_tiles, 1 + i // tpb, 0), 0, 0)))
    for a in tabs:
        arrays.append(a)
        specs.append(pl.BlockSpec((None, ROW_TILE, a.shape[-1]),
                                  lambda i: (jnp.where(i < lat_tiles, 1 + i % tpb, 0), 0, 0)))
    out_shapes, out_specs = [], []
    for shape, dtype in outs:
        out_shapes.append(jax.ShapeDtypeStruct(shape, dtype))
        if shape[0] == TOP_K:
            out_specs.append(pl.BlockSpec((shape[0], ROW_TILE), lambda i: (0, i)))
        else:
            out_specs.append(pl.BlockSpec((ROW_TILE, shape[-1]), lambda i: (i, 0)))
    n_r, n_c, n_m, n_t = len(rows), len(consts), len(mods), len(tabs)

    def kern(*refs):
        r = refs[:n_r]
        c = refs[n_r:n_r + n_c]
        m = refs[n_r + n_c:n_r + n_c + n_m]
        t = refs[n_r + n_c + n_m:n_r + n_c + n_m + n_t]
        o = refs[n_r + n_c + n_m + n_t:]
        body(r, c, m, t, o)

    return pl.pallas_call(
        kern, grid=(n_tiles,), in_specs=specs, out_specs=out_specs, out_shape=out_shapes,
        compiler_params=_params(("parallel",)), name=name,
    )(*arrays)


def _residual_in(r, m, has_prev):
    if has_prev:
        return r[0][...] + m[0][...] * r[1][...]
    return r[0][...]


def _rope_pair(x, cos, sin_a, sin_b, shift):
    return x * cos + pltpu.roll(x, shift, 1) * sin_a + pltpu.roll(x, LANES - shift, 1) * sin_b


def _mla_pre_body(has_prev, r, c, m, t, o):
    g1, w_in, qg, kvg, wq, wk, wv, place = c
    shift1, scale1 = m[-2], m[-1]
    cos_q, sa_q, sb_q, cos_k, sa_k, sb_k = [a[...] for a in t]
    x = _residual_in(r, m, has_prev)
    h = _norm_mod(x, g1[...], scale1[...], shift1[...])
    z = _mm(h, w_in[...])
    cq = _rms(z[:, :MLA_Q_RANK], qg[...]).astype(BF16)
    ckv = _rms(z[:, MLA_Q_RANK:MLA_Q_RANK + MLA_KV_RANK], kvg[...]).astype(BF16)
    kr = z[:, MLA_Q_RANK + MLA_KV_RANK:]
    kr = _rope_pair(kr, cos_k, sa_k, sb_k, MLA_ROPE // 2)
    scale = (MLA_NOPE + MLA_ROPE) ** -0.5
    q = _mm(cq, wq[...])
    for hh in range(MLA_HEADS):
        qs = q[:, hh * LANES:(hh + 1) * LANES]
        qs = _rope_pair(qs, cos_q, sa_q, sb_q, MLA_ROPE // 2)
        o[1][:, hh * LANES:(hh + 1) * LANES] = (qs * scale).astype(BF16)
    o[0][...] = x
    o[2][...] = (_mm(ckv, wk[...]) + _mm(kr, place[...])).astype(BF16)
    o[3][...] = _mm(ckv, wv[...]).astype(BF16)


def _gqa_pre_body(has_prev, r, c, m, t, o):
    g1, w_in, qg, kg = c
    shift1, scale1 = m[-2], m[-1]
    cos, sin = t[0][...], t[1][...]
    x = _residual_in(r, m, has_prev)
    h = _norm_mod(x, g1[...], scale1[...], shift1[...])
    z = _mm(h, w_in[...])
    scale = GQA_HEAD_DIM ** -0.5
    hd = GQA_HEAD_DIM
    for hh in range(GQA_Q_HEADS + GQA_KV_HEADS):
        zs = z[:, hh * hd:(hh + 1) * hd]
        is_q = hh < GQA_Q_HEADS
        zs = _rms(zs, qg[...] if is_q else kg[...])
        zs = zs * cos + pltpu.roll(zs, hd // 2, 1) * sin
        if is_q:
            o[1][:, hh * hd:(hh + 1) * hd] = (zs * scale).astype(BF16)
        else:
            kk = hh - GQA_Q_HEADS
            o[2][:, kk * hd:(kk + 1) * hd] = zs.astype(BF16)
    o[0][...] = x
    o[3][...] = z[:, (GQA_Q_HEADS + GQA_KV_HEADS) * hd:].astype(BF16)


def _mlstm_pre_body(has_prev, r, c, m, t, o):
    g1, w_in, gate_b, fmask = c
    shift1, scale1 = m[-2], m[-1]
    x = _residual_in(r, m, has_prev)
    h = _norm_mod(x, g1[...], scale1[...], shift1[...])
    z = _mm(h, w_in[...])
    nqk = MLSTM_HEADS * MLSTM_QK
    nv = MLSTM_HEADS * MLSTM_V
    o[0][...] = x
    o[1][...] = z[:, :nqk].astype(BF16)
    o[2][...] = (z[:, nqk:2 * nqk] * MLSTM_QK ** -0.5).astype(BF16)
    o[3][...] = z[:, 2 * nqk:2 * nqk + nv].astype(BF16)
    o[4][...] = z[:, 2 * nqk + nv:2 * nqk + 2 * nv].astype(BF16)
    gt = z[:, 2 * nqk + 2 * nv:] + gate_b[...]
    log_sig = jnp.minimum(gt, 0.0) - jnp.log(1.0 + jnp.exp(-jnp.abs(gt)))
    o[5][...] = jnp.where(fmask[...] > 0.0, log_sig, gt)


def _gdn_pre_body(has_prev, r, c, m, t, o):
    g1, w_in, w_ba, a_scale, dt_b, dmask = c
    shift1, scale1 = m[-2], m[-1]
    x = _residual_in(r, m, has_prev)
    h = _norm_mod(x, g1[...], scale1[...], shift1[...]).astype(BF16)
    ncv = 2 * GDN_QK_HEADS * GDN_HEAD_DIM + GDN_V_HEADS * GDN_HEAD_DIM
    z = _mm(h, w_in[...])
    o[0][...] = x
    o[1][...] = z[:, :ncv].astype(BF16)
    o[2][...] = z[:, ncv:].astype(BF16)
    ba = _mm(h, w_ba[...])
    o[3][...] = jnp.where(dmask[...] > 0.0, -a_scale[...] * _softplus(ba + dt_b[...]), _sigmoid(ba))


def _route_tail(x, y, c, m, o):
    g2, rw_t, rb = c
    gate1, shift2, scale2 = m
    x1 = x + gate1[...] * y
    h2 = _norm_mod(x1, g2[...], scale2[...], shift2[...])
    o[0][...] = x1
    o[1][...] = h2.astype(BF16)
    logits = lax.dot_general(rw_t[...], h2, (((1,), (1,)), ((), ())), preferred_element_type=F32,
                             precision=HIGHEST) + rb[...]
    e_iota = lax.broadcasted_iota(jnp.int32, logits.shape, 0)
    vals = []
    for k in range(TOP_K):
        mx = jnp.max(logits, axis=0, keepdims=True)
        idx = jnp.min(jnp.where(logits == mx, e_iota, N_EXPERTS), axis=0, keepdims=True)
        o[2][k:k + 1, :] = idx
        vals.append(mx)
        logits = jnp.where(e_iota == idx, -jnp.inf, logits)
    es = [jnp.exp(v - vals[0]) for v in vals]
    tot = es[0] + es[1] + es[2] + es[3]
    for k in range(TOP_K):
        o[3][k:k + 1, :] = es[k] / tot


def _attn_post_body(r, c, m, t, o):
    x, a = r
    y = _mm(a[...], c[0][...])
    _route_tail(x[...], y, c[1:], m, o)


def _mlstm_post_body(r, c, m, t, o):
    x, hf, hb, og = r
    w_o, ng = c[0], c[1]
    hs = hf[...] + hb[...]
    gate = _sigmoid(og[...].astype(F32))
    parts = []
    for hh in range(MLSTM_HEADS):
        sl = slice(hh * MLSTM_V, (hh + 1) * MLSTM_V)
        parts.append((_rms(hs[:, sl], ng[...]) * gate[:, sl]).astype(BF16))
    y = _mm(jnp.concatenate(parts, axis=1), w_o[...])
    _route_tail(x[...], y, c[2:], m, o)


def _gdn_post_body(r, c, m, t, o):
    x, of, ob, zg = r
    w_o, ng = c[0], c[1]
    os_ = of[...] + ob[...]
    g = zg[...].astype(F32)
    gate = g * _sigmoid(g)
    parts = []
    for hh in range(GDN_V_HEADS):
        sl = slice(hh * GDN_HEAD_DIM, (hh + 1) * GDN_HEAD_DIM)
        parts.append((_rms(os_[:, sl], ng[...]) * gate[:, sl]).astype(BF16))
    y = _mm(jnp.concatenate(parts, axis=1), w_o[...])
    _route_tail(x[...], y, c[2:], m, o)


def _final_body(r, c, m, t, o):
    x = r[0][...] + m[0][...] * r[1][...]
    o[0][...] = _rms(x, c[0][...])


def _attn_kernel(*refs, n_seg, k_stride, v_width):
    q_ref = refs[0]
    segs = [(refs[1 + 2 * i], refs[2 + 2 * i]) for i in range(n_seg)]
    o_ref = refs[1 + 2 * n_seg]
    for j in range(2):
        q = q_ref[:, j * LANES:(j + 1) * LANES]
        ss = [_mm_nt(q, k_ref[:, j * k_stride:j * k_stride + LANES]) for k_ref, _ in segs]
        mx = ss[0].max(axis=-1, keepdims=True)
        for s in ss[1:]:
            mx = jnp.maximum(mx, s.max(axis=-1, keepdims=True))
        acc, den = 0.0, 0.0
        for s, (_, v_ref) in zip(ss, segs):
            p = jnp.exp(s - mx)
            den = den + p.sum(axis=-1, keepdims=True)
            voff = j * v_width if k_stride else 0
            acc = acc + _mm(p, v_ref[:, voff:voff + v_width])
        o_ref[:, j * v_width:(j + 1) * v_width] = (acc / den).astype(o_ref.dtype)


def _attention(cfg, q, k, v, n_groups, k_stride, v_width, latent):
    kw = 2 * LANES if k_stride else LANES
    vw = 2 * v_width if k_stride else v_width
    ow = 2 * v_width
    ctx_blk0 = cfg.n_lat // cfg.nc
    k_ctx = pl.BlockSpec((cfg.nc, kw), lambda b, g, i: (ctx_blk0 + b, g))
    v_ctx = pl.BlockSpec((cfg.nc, vw), lambda b, g, i: (ctx_blk0 + b, g))
    if latent:
        qpb = cfg.s // Q_TILE
        grid = (cfg.b, n_groups, qpb)
        q_spec = pl.BlockSpec((Q_TILE, 2 * LANES), lambda b, g, i: (b * qpb + i, g))
        o_spec = pl.BlockSpec((Q_TILE, ow), lambda b, g, i: (b * qpb + i, g))
        in_specs = [q_spec, pl.BlockSpec((cfg.s, kw), lambda b, g, i: (b, g)),
                    pl.BlockSpec((cfg.s, vw), lambda b, g, i: (b, g)), k_ctx, v_ctx]
        args = (q, k, v, k, v)
        n_seg = 2
        rows = cfg.n_lat
    else:
        qpb = cfg.nc // Q_TILE
        q0 = cfg.n_lat // Q_TILE
        grid = (cfg.b, n_groups, qpb)
        q_spec = pl.BlockSpec((Q_TILE, 2 * LANES), lambda b, g, i: (q0 + b * qpb + i, g))
        o_spec = pl.BlockSpec((Q_TILE, ow), lambda b, g, i: (b * qpb + i, g))
        in_specs = [q_spec, k_ctx, v_ctx]
        args = (q, k, v)
        n_seg = 1
        rows = cfg.b * cfg.nc
    return pl.pallas_call(
        functools.partial(_attn_kernel, n_seg=n_seg, k_stride=k_stride, v_width=v_width),
        grid=grid, in_specs=in_specs, out_specs=o_spec,
        out_shape=jax.ShapeDtypeStruct((rows, n_groups * ow), BF16),
        compiler_params=_params(("parallel", "parallel", "arbitrary")),
        name="attention_lat" if latent else "attention_ctx",
    )(*args)


def _chunk_masks(flip):
    li = lax.broadcasted_iota(jnp.int32, (CHUNK, CHUNK), 0)
    si = lax.broadcasted_iota(jnp.int32, (CHUNK, CHUNK), 1)
    eye = li == si
    incl = (si >= li) if flip else (si <= li)
    strict = (si > li) if flip else (si < li)
    return eye, incl, strict


def _to_row(col, eye):
    return jnp.sum(jnp.where(eye, col, 0.0), axis=0, keepdims=True)


def _cumsum_col(col, eye, incl):
    row = _to_row(col, eye)
    cum_col = jnp.sum(jnp.where(incl, row, 0.0), axis=1, keepdims=True)
    return cum_col, _to_row(cum_col, eye)


def _scan_block_index(cfg, flip):
    cpb, tpb, lat_tiles = cfg.cpb, cfg.tpb, cfg.lat_tiles

    def row_block(b, j):
        if flip:
            ctx = lat_tiles + b * cpb + (cpb - 1 - j)
            lat = b * tpb + (tpb - 1 - (j - cpb))
        else:
            ctx = lat_tiles + b * cpb + j
            lat = b * tpb + (j - cpb)
        return jnp.where(j < cpb, ctx, lat)

    return row_block


def _mlstm_kernel(q_ref, k_ref, v_ref, g_ref, o_ref, c_st, n_st, m_st, *, flip, d):
    @pl.when(pl.program_id(1) == 0)
    def _():
        c_st[...] = jnp.zeros_like(c_st)
        n_st[...] = jnp.zeros_like(n_st)
        m_st[...] = jnp.zeros_like(m_st)

    eye, incl, _ = _chunk_masks(flip)
    n_chunks = ROW_TILE // CHUNK
    for ci in range(n_chunks):
        cc = n_chunks - 1 - ci if flip else ci
        rows = slice(cc * CHUNK, (cc + 1) * CHUNK)
        for hh in range(MLSTM_HEADS):
            q = q_ref[rows, hh * MLSTM_QK:(hh + 1) * MLSTM_QK]
            k = k_ref[rows, hh * MLSTM_QK:(hh + 1) * MLSTM_QK]
            v = v_ref[rows, hh * MLSTM_V:(hh + 1) * MLSTM_V]
            icol = d * 2 * MLSTM_HEADS + hh
            fcol = icol + MLSTM_HEADS
            i_col = g_ref[rows, icol:icol + 1]
            f_col = g_ref[rows, fcol:fcol + 1]
            c_prev, n_prev, m_prev = c_st[hh], n_st[hh], m_st[hh]

            i_row = _to_row(i_col, eye)
            bcum_col, bcum_row = _cumsum_col(f_col, eye, incl)
            b_end = jnp.sum(f_col, axis=0, keepdims=True)
            w_log = b_end - bcum_col + i_col
            m_new = jnp.maximum(b_end + m_prev, jnp.max(w_log, axis=0, keepdims=True))
            carry = jnp.exp(b_end + m_prev - m_new)
            kw = k.astype(F32) * jnp.exp(w_log - m_new)

            g_log = bcum_col + m_prev
            d_log = jnp.where(incl, bcum_col - bcum_row + i_row, -jnp.inf)
            m_t = jnp.maximum(g_log, jnp.max(d_log, axis=1, keepdims=True))
            inter = jnp.exp(g_log - m_t)
            s = _mm_nt(q, k) * jnp.exp(d_log - m_t)
            num = inter * _mm(q, c_prev) + _mm(s, v)
            den = inter * jnp.sum(q.astype(F32) * n_prev, axis=1, keepdims=True) + jnp.sum(s, axis=1, keepdims=True)
            h_out = num / jnp.maximum(jnp.abs(den), jnp.exp(-m_t))
            o_ref[rows, hh * MLSTM_V:(hh + 1) * MLSTM_V] = h_out

            c_st[hh] = carry * c_prev + _mm(kw.T, v)
            n_st[hh] = carry * n_prev + jnp.sum(kw, axis=0, keepdims=True)
            m_st[hh] = m_new


def _mlstm_scan(cfg, q, k, v, gates):
    nqk, nv = MLSTM_HEADS * MLSTM_QK, MLSTM_HEADS * MLSTM_V
    outs = []
    for d in range(2):
        flip = d == 1
        rb = _scan_block_index(cfg, flip)
        row = lambda w, rb=rb: pl.BlockSpec((ROW_TILE, w), lambda b, j: (rb(b, j), 0))
        outs.append(pl.pallas_call(
            functools.partial(_mlstm_kernel, flip=flip, d=d),
            grid=(cfg.b, cfg.cpb + cfg.tpb),
            in_specs=[row(nqk), row(nqk), row(nv), row(LANES)],
            out_specs=row(nv),
            out_shape=jax.ShapeDtypeStruct((cfg.m, nv), F32),
            scratch_shapes=[pltpu.VMEM((MLSTM_HEADS, MLSTM_QK, MLSTM_V), F32),
                            pltpu.VMEM((MLSTM_HEADS, 1, MLSTM_QK), F32),
                            pltpu.VMEM((MLSTM_HEADS, 1, 1), F32)],
            compiler_params=_params(("parallel", "arbitrary")),
            name="mlstm_bwd" if flip else "mlstm_fwd",
        )(q, k, v, gates))
    return outs


def _unit_lower_inverse(mkk, eye):
    eye_f = eye.astype(F32)
    inv = eye_f - mkk
    pw = mkk
    steps = int(math.log2(CHUNK)) - 1
    for _ in range(steps):
        pw = _mm(pw, pw)
        inv = inv + _mm(inv, pw)
    return inv


def _gdn_kernel(q_ref, k_ref, v_ref, ba_ref, o_ref, s_st, *, flip, d):
    @pl.when(pl.program_id(2) == 0)
    def _():
        s_st[...] = jnp.zeros_like(s_st)

    eye, incl, strict = _chunk_masks(flip)
    rep = GDN_V_HEADS // GDN_QK_HEADS
    hd = GDN_HEAD_DIM
    n_chunks = ROW_TILE // CHUNK
    for ci in range(n_chunks):
        cc = n_chunks - 1 - ci if flip else ci
        rows = slice(cc * CHUNK, (cc + 1) * CHUNK)
        q = q_ref[rows, :]
        k = k_ref[rows, :]
        kf = k.astype(F32)
        qk = _mm_nt(q, k)
        for jj in range(rep):
            v = v_ref[rows, jj * hd:(jj + 1) * hd].astype(F32)
            bcol = d * 2 * rep + jj
            beta = ba_ref[rows, bcol:bcol + 1]
            ld = ba_ref[rows, bcol + rep:bcol + rep + 1]
            s_prev = s_st[jj]

            gc_col, gc_row = _cumsum_col(ld, eye, incl)
            gc_end = jnp.sum(ld, axis=0, keepdims=True)
            decay = jnp.exp(jnp.where(incl, gc_col - gc_row, -jnp.inf))
            kb = kf * beta
            mkk = jnp.where(strict, _mm_nt(kb, k) * decay, 0.0)
            inv = _unit_lower_inverse(mkk, eye)
            eg = jnp.exp(gc_col)
            u = _mm(inv, v * beta)
            w = _mm(inv, kb * eg)
            kdec = kf * jnp.exp(gc_end - gc_col)
            v_new = u - _mm(w, s_prev)
            o_ref[rows, jj * hd:(jj + 1) * hd] = _mm(q.astype(F32) * eg, s_prev) + _mm(qk * decay, v_new)
            s_st[jj] = s_prev * jnp.exp(gc_end) + _mm(kdec.T, v_new)


def _gdn_scan(cfg, qkv, ba):
    rep = GDN_V_HEADS // GDN_QK_HEADS
    hd = GDN_HEAD_DIM
    outs = []
    for d in range(2):
        flip = d == 1
        rb = _scan_block_index(cfg, flip)
        col = lambda w, off, rb=rb: pl.BlockSpec((ROW_TILE, w), lambda b, h, j: (rb(b, j), off + h))
        outs.append(pl.pallas_call(
            functools.partial(_gdn_kernel, flip=flip, d=d),
            grid=(cfg.b, GDN_QK_HEADS, cfg.cpb + cfg.tpb),
            in_specs=[col(hd, 0), col(hd, GDN_QK_HEADS), col(rep * hd, 2 * GDN_QK_HEADS // rep), col(LANES, 0)],
            out_specs=col(rep * hd, 0),
            out_shape=jax.ShapeDtypeStruct((cfg.m, GDN_V_HEADS * hd), F32),
            scratch_shapes=[pltpu.VMEM((rep, hd, hd), F32)],
            compiler_params=_params(("parallel", "parallel", "arbitrary")),
            name="gdn_bwd" if flip else "gdn_fwd",
        )(qkv, qkv, qkv, ba))
    return outs


def _gdn_conv_kernel(cur_ref, prev_ref, next_ref, w_ref, o_ref, pad_ref, *, lat_tiles, tpb, cpb):
    i = pl.program_id(0)
    cb = pl.program_id(1)
    lat = i < lat_tiles
    pos = jnp.where(lat, i % tpb, (i - lat_tiles) % cpb)
    per = jnp.where(lat, tpb, cpb)
    first = pos == 0
    last = pos == per - 1
    prev = prev_ref[...].astype(F32)
    nxt = next_ref[...].astype(F32)
    pad_ref[0:HALO, :] = jnp.where(first, 0.0, prev)
    pad_ref[HALO:HALO + ROW_TILE, :] = cur_ref[...].astype(F32)
    pad_ref[HALO + ROW_TILE:, :] = jnp.where(last, 0.0, nxt)
    half = GDN_CONV // 2
    acc = 0.0
    for tap in range(GDN_CONV):
        off = HALO - half + tap
        acc = acc + pad_ref[off:off + ROW_TILE, :] * w_ref[tap:tap + 1, :]
    act = acc * _sigmoid(acc)
    qscale = jnp.where(cb == 0, GDN_HEAD_DIM ** -0.5, 1.0)
    hd = GDN_HEAD_DIM
    for hh in range(act.shape[1] // hd):
        a = act[:, hh * hd:(hh + 1) * hd]
        nrm = a * lax.rsqrt(jnp.sum(a * a, axis=-1, keepdims=True) + NORM_EPS) * qscale
        o_ref[:, hh * hd:(hh + 1) * hd] = jnp.where(cb < 2, nrm, a).astype(o_ref.dtype)


def _gdn_conv(cfg, zqkv, conv_w):
    cw = GDN_QK_HEADS * GDN_HEAD_DIM
    n_cb = zqkv.shape[1] // cw
    hpt = ROW_TILE // HALO
    n_halo = cfg.m // HALO
    return pl.pallas_call(
        functools.partial(_gdn_conv_kernel, lat_tiles=cfg.lat_tiles, tpb=cfg.tpb, cpb=cfg.cpb),
        grid=(cfg.tiles, n_cb),
        in_specs=[pl.BlockSpec((ROW_TILE, cw), lambda i, c: (i, c)),
                  pl.BlockSpec((HALO, cw), lambda i, c: (jnp.maximum(i * hpt - 1, 0), c)),
                  pl.BlockSpec((HALO, cw), lambda i, c: (jnp.minimum((i + 1) * hpt, n_halo - 1), c)),
                  pl.BlockSpec((8, cw), lambda i, c: (0, c))],
        out_specs=pl.BlockSpec((ROW_TILE, cw), lambda i, c: (i, c)),
        out_shape=jax.ShapeDtypeStruct(zqkv.shape, BF16),
        scratch_shapes=[pltpu.VMEM((ROW_TILE + 2 * HALO, cw), F32)],
        compiler_params=_params(("parallel", "parallel")),
        name="gdn_conv",
    )(zqkv, zqkv, zqkv, conv_w)


def _rank_kernel(e_ref, rank_ref, cnt_ref, run_ref):
    @pl.when(pl.program_id(0) == 0)
    def _():
        run_ref[...] = jnp.zeros_like(run_ref)

    e = e_ref[...]
    onehot = (lax.broadcasted_iota(jnp.int32, (N_EXPERTS, RANK_TILE), 0) == e)
    oh = jnp.where(onehot, 1.0, 0.0)
    ji = lax.broadcasted_iota(jnp.int32, (RANK_TILE, RANK_TILE), 0)
    si = lax.broadcasted_iota(jnp.int32, (RANK_TILE, RANK_TILE), 1)
    upper = jnp.where(ji <= si, 1.0, 0.0).astype(BF16)
    cum = jnp.dot(oh.astype(BF16), upper, preferred_element_type=F32)
    run = run_ref[:, 0:1]
    rank = jnp.sum(oh * (cum - 1.0 + run), axis=0, keepdims=True)
    rank_ref[...] = rank.astype(jnp.int32)
    run_ref[...] = run_ref[...] + jnp.sum(oh, axis=1, keepdims=True)
    cnt_ref[...] = run_ref[...]


def _rank_pairs(expert_of_pair):
    n_pairs = expert_of_pair.shape[1]
    return pl.pallas_call(
        _rank_kernel,
        grid=(n_pairs // RANK_TILE,),
        in_specs=[pl.BlockSpec((1, RANK_TILE), lambda i: (0, i))],
        out_specs=[pl.BlockSpec((1, RANK_TILE), lambda i: (0, i)),
                   pl.BlockSpec((N_EXPERTS, LANES), lambda i: (0, 0))],
        out_shape=[jax.ShapeDtypeStruct((1, n_pairs), jnp.int32),
                   jax.ShapeDtypeStruct((N_EXPERTS, LANES), F32)],
        scratch_shapes=[pltpu.VMEM((N_EXPERTS, LANES), F32)],
        compiler_params=_params(("arbitrary",)),
        name="moe_rank",
    )(expert_of_pair)


def _moe_kernel(te_ref, nu_ref, x_ref, w1_ref, b1g_ref, b1l_ref, w2_ref, b2_ref, o_ref, w1g_s, w1l_s, w2_s):
    j = pl.program_id(0)
    used = j < nu_ref[0]
    new_expert = jnp.logical_or(j == 0, te_ref[j] != te_ref[jnp.maximum(j - 1, 0)])

    @pl.when(jnp.logical_and(used, new_expert))
    def _():
        slab = 2 * LANES
        src = lax.broadcasted_iota(jnp.int32, (slab, slab), 0)
        dst = lax.broadcasted_iota(jnp.int32, (slab, slab), 1)
        want = jnp.where(dst < LANES, 2 * dst, 2 * (dst - LANES) + 1)
        perm = jnp.where(src == want, 1.0, 0.0).astype(BF16)
        for sb in range(w1_ref.shape[1] // slab):
            sorted_cols = jnp.dot(w1_ref[:, sb * slab:(sb + 1) * slab].astype(BF16), perm,
                                  preferred_element_type=F32)
            w1g_s[:, sb * LANES:(sb + 1) * LANES] = sorted_cols[:, :LANES].astype(BF16)
            w1l_s[:, sb * LANES:(sb + 1) * LANES] = sorted_cols[:, LANES:].astype(BF16)
        w2_s[...] = w2_ref[...].astype(BF16)

    @pl.when(used)
    def _():
        x = x_ref[...]
        glu = jnp.minimum(jnp.dot(x, w1g_s[...], preferred_element_type=F32) + b1g_ref[...], SWIGLU_LIMIT)
        lin = jnp.clip(jnp.dot(x, w1l_s[...], preferred_element_type=F32) + b1l_ref[...],
                       -SWIGLU_LIMIT, SWIGLU_LIMIT)
        act = glu * _sigmoid(SWIGLU_ALPHA * glu) * (lin + 1.0)
        y = jnp.dot(act.astype(BF16), w2_s[...], preferred_element_type=F32) + b2_ref[...]
        o_ref[...] = y.astype(o_ref.dtype)

    @pl.when(jnp.logical_not(used))
    def _():
        o_ref[...] = jnp.zeros_like(o_ref)


def _moe_grouped(xs, tile_expert, n_used, w1, b1g, b1l, w2, b2):
    n_slots, d = xs.shape
    de = w2.shape[1]
    wspec = lambda r, c: pl.BlockSpec((None, r, c), lambda j, te, nu: (te[j], 0, 0))
    return pl.pallas_call(
        _moe_kernel,
        grid_spec=pltpu.PrefetchScalarGridSpec(
            num_scalar_prefetch=2,
            grid=(n_slots // GROUP_TILE,),
            in_specs=[pl.BlockSpec((GROUP_TILE, d), lambda j, te, nu: (j, 0)),
                      wspec(d, 2 * de), wspec(1, de), wspec(1, de), wspec(de, d), wspec(1, d)],
            out_specs=pl.BlockSpec((GROUP_TILE, d), lambda j, te, nu: (j, 0)),
            scratch_shapes=[pltpu.VMEM((d, de), BF16), pltpu.VMEM((d, de), BF16), pltpu.VMEM((de, d), BF16)]),
        out_shape=jax.ShapeDtypeStruct((n_slots, d), BF16),
        compiler_params=_params(("arbitrary",)),
        name="moe_grouped",
    )(tile_expert, n_used, xs, w1, b1g, b1l, w2, b2)


def _moe_ffn(h2, topi, topw, w1, b1, w2, b2):
    n_tok = h2.shape[0]
    n_pairs = TOP_K * n_tok
    assert n_pairs % RANK_TILE == 0
    n_slots = -(-(n_pairs + N_EXPERTS * (GROUP_TILE - 1)) // GROUP_TILE) * GROUP_TILE
    n_tiles = n_slots // GROUP_TILE

    eid = topi.reshape(1, n_pairs)
    rank, counts = _rank_pairs(eid)
    counts = counts[:, 0].astype(jnp.int32)
    padded = (counts + GROUP_TILE - 1) // GROUP_TILE * GROUP_TILE
    ends = jnp.cumsum(padded)
    starts = ends - padded
    slot = starts[eid[0]] + rank[0]
    token = jnp.tile(jnp.arange(n_tok, dtype=jnp.int32), TOP_K)
    src = jnp.zeros((n_slots,), jnp.int32).at[slot].set(token)
    tile_start = jnp.arange(n_tiles, dtype=jnp.int32) * GROUP_TILE
    tile_expert = jnp.sum((ends[None, :] <= tile_start[:, None]).astype(jnp.int32), axis=1)
    tile_expert = jnp.minimum(tile_expert, N_EXPERTS - 1)
    n_used = (ends[-1] // GROUP_TILE).astype(jnp.int32).reshape(1)

    xs = jnp.take(h2, src, axis=0)
    b1g = b1[:, None, 0::2]
    b1l = b1[:, None, 1::2]
    ys = _moe_grouped(xs, tile_expert, n_used, w1, b1g, b1l, w2, b2[:, None, :])

    pos = slot.reshape(TOP_K, n_tok)
    y = jnp.zeros((n_tok, h2.shape[1]), F32)
    for k in range(TOP_K):
        y = y + topw[k][:, None] * jnp.take(ys, pos[k], axis=0).astype(F32)
    return y


def _axial_angles(n_tokens, rot_dim):
    t = jnp.arange(n_tokens, dtype=jnp.int32)
    rows = (t // GRID_W).astype(F32)
    cols = (t % GRID_W).astype(F32)
    d_axis = rot_dim // 2
    inv_freq = ROPE_BASE ** (-jnp.arange(0, d_axis, 2, dtype=F32) / d_axis)
    ang = jnp.concatenate([rows[:, None] * inv_freq, cols[:, None] * inv_freq], axis=-1)
    return jnp.cos(ang), jnp.sin(ang)


def _seq_table(cfg, lat_rows, ctx_value):
    c = lat_rows.shape[-1]
    ctx = jnp.full((1, ROW_TILE, c), ctx_value, F32)
    return jnp.concatenate([ctx, lat_rows.reshape(cfg.tpb, ROW_TILE, c)], axis=0)


def _slab_rope_tables(cfg, rot_dim, lane0):
    cos, sin = _axial_angles(cfg.s, rot_dim)
    half = rot_dim // 2
    s = cfg.s
    ones = lambda n: jnp.ones((s, n), F32)
    zeros = lambda n: jnp.zeros((s, n), F32)
    tail = LANES - lane0 - rot_dim
    cos_t = jnp.concatenate([ones(lane0), cos, cos, ones(tail)], axis=1)
    sin_a = jnp.concatenate([zeros(lane0 + half), sin, zeros(tail)], axis=1)
    sin_b = jnp.concatenate([zeros(lane0), -sin, zeros(half + tail)], axis=1)
    return _seq_table(cfg, cos_t, 1.0), _seq_table(cfg, sin_a, 0.0), _seq_table(cfg, sin_b, 0.0)


def _mod_slices(table, n_rows):
    d = D_MODEL
    return [table[:, k * d:(k + 1) * d].reshape(n_rows, 1, d) for k in range(N_MOD)]


def kernel(x, c, ctx, c_ctx, ada_w, ada_b, norm1_g, norm2_g, router_w, router_b, moe_w1, moe_b1, moe_w2, moe_b2,
           mla_w_in, mla_q_norm_g, mla_kv_norm_g, mla_w_uq, mla_w_ukv, mla_w_o, gqa_w_in, gqa_q_norm_g,
           gqa_k_norm_g, gqa_w_o, mlstm_w_in, mlstm_gate_b, mlstm_norm_g, mlstm_w_o, gdn_w_in, gdn_conv_w,
           gdn_a_log, gdn_dt_bias, gdn_norm_g, gdn_w_o, final_norm_g):
    b, s, d = x.shape
    nc = ctx.shape[1]
    depth = ada_w.shape[0]
    cfg = _Cfg(b, s, nc)
    m = cfg.m
    row = lambda a: a.reshape(1, -1).astype(F32)

    cond = jnp.concatenate([c_ctx[None, :], c, jnp.zeros((MOD_ROWS - 1 - b, d), F32)], axis=0)
    mod_all = _ada_table(cond, ada_w, ada_b)

    stream = jnp.concatenate([x.reshape(b * s, d), ctx.reshape(b * nc, d)], axis=0)
    y_prev = None
    gate2_prev = None
    for i in range(depth):
        kind, j = i % 4, i // 4
        shift1, scale1, gate1, shift2, scale2, gate2 = _mod_slices(mod_all[i], MOD_ROWS)
        has_prev = y_prev is not None
        rows_in = [stream, y_prev] if has_prev else [stream]
        mods_in = ([gate2_prev] if has_prev else []) + [shift1, scale1]
        g1 = row(norm1_g[i])
        route_consts = [row(norm2_g[i]), router_w[i].T, router_b[i].reshape(N_EXPERTS, 1)]
        route_mods = [gate1, shift2, scale2]
        route_outs = [((m, d), F32), ((m, d), BF16), ((TOP_K, m), jnp.int32), ((TOP_K, m), F32)]

        if kind == 0:
            qk = MLA_NOPE + MLA_ROPE
            w_in = jnp.pad(mla_w_in[j], ((0, 0), (0, LANES - MLA_ROPE))).astype(BF16)
            wq = jnp.pad(mla_w_uq[j].reshape(MLA_Q_RANK, MLA_HEADS, qk), ((0, 0), (0, 0), (0, LANES - qk)))
            wq = wq.reshape(MLA_Q_RANK, MLA_HEADS * LANES).astype(BF16)
            wkv = mla_w_ukv[j].reshape(MLA_KV_RANK, MLA_HEADS, MLA_NOPE + MLA_V)
            wk = jnp.pad(wkv[:, :, :MLA_NOPE], ((0, 0), (0, 0), (0, LANES - MLA_NOPE)))
            wk = wk.reshape(MLA_KV_RANK, MLA_HEADS * LANES).astype(BF16)
            wv = wkv[:, :, MLA_NOPE:].reshape(MLA_KV_RANK, MLA_HEADS * MLA_V).astype(BF16)
            place = jnp.zeros((LANES, LANES), F32).at[jnp.arange(MLA_ROPE), MLA_NOPE + jnp.arange(MLA_ROPE)].set(1.0)
            place = jnp.tile(place, (1, MLA_HEADS)).astype(BF16)
            tabs = list(_slab_rope_tables(cfg, MLA_ROPE, MLA_NOPE)) + list(_slab_rope_tables(cfg, MLA_ROPE, 0))
            stream, q, k, v = _row_call(
                functools.partial(_mla_pre_body, has_prev), cfg, rows_in,
                [g1, w_in, row(mla_q_norm_g[j]), row(mla_kv_norm_g[j]), wq, wk, wv, place], mods_in, tabs,
                [((m, d), F32), ((m, MLA_HEADS * LANES), BF16), ((m, MLA_HEADS * LANES), BF16),
                 ((m, MLA_HEADS * MLA_V), BF16)], name="mla_pre")
            a_lat = _attention(cfg, q, k, v, MLA_HEADS // 2, LANES, MLA_V, True)
            a_ctx = _attention(cfg, q, k, v, MLA_HEADS // 2, LANES, MLA_V, False)
            a = jnp.concatenate([a_lat, a_ctx], axis=0)
            stream, h2, topi, topw = _row_call(
                _attn_post_body, cfg, [stream, a], [mla_w_o[j].astype(BF16)] + route_consts, route_mods, [],
                route_outs, name="mla_post")
        elif kind == 1:
            cos, sin = _axial_angles(cfg.s, GQA_HEAD_DIM)
            tabs = [_seq_table(cfg, jnp.concatenate([cos, cos], axis=1), 1.0),
                    _seq_table(cfg, jnp.concatenate([-sin, sin], axis=1), 0.0)]
            stream, q, k, v = _row_call(
                functools.partial(_gqa_pre_body, has_prev), cfg, rows_in,
                [g1, gqa_w_in[j].astype(BF16), row(gqa_q_norm_g[j]), row(gqa_k_norm_g[j])], mods_in, tabs,
                [((m, d), F32), ((m, GQA_Q_HEADS * GQA_HEAD_DIM), BF16), ((m, GQA_KV_HEADS * GQA_HEAD_DIM), BF16),
                 ((m, GQA_KV_HEADS * GQA_HEAD_DIM), BF16)], name="gqa_pre")
            a_lat = _attention(cfg, q, k, v, GQA_KV_HEADS, 0, GQA_HEAD_DIM, True)
            a_ctx = _attention(cfg, q, k, v, GQA_KV_HEADS, 0, GQA_HEAD_DIM, False)
            a = jnp.concatenate([a_lat, a_ctx], axis=0)
            stream, h2, topi, topw = _row_call(
                _attn_post_body, cfg, [stream, a], [gqa_w_o[j].astype(BF16)] + route_consts, route_mods, [],
                route_outs, name="gqa_post")
        elif kind == 2:
            n_gate = 4 * MLSTM_HEADS
            w_in = jnp.pad(mlstm_w_in[j], ((0, 0), (0, LANES - n_gate))).astype(BF16)
            gate_b = jnp.pad(mlstm_gate_b[j].reshape(1, n_gate).astype(F32), ((0, 0), (0, LANES - n_gate)))
            col = jnp.arange(LANES)
            fmask = (((col // MLSTM_HEADS) % 2 == 1) & (col < n_gate)).astype(F32).reshape(1, LANES)
            nqk, nv = MLSTM_HEADS * MLSTM_QK, MLSTM_HEADS * MLSTM_V
            stream, q, k, v, og, gates = _row_call(
                functools.partial(_mlstm_pre_body, has_prev), cfg, rows_in, [g1, w_in, gate_b, fmask], mods_in, [],
                [((m, d), F32), ((m, nqk), BF16), ((m, nqk), BF16), ((m, nv), BF16), ((m, nv), BF16),
                 ((m, LANES), F32)], name="mlstm_pre")
            hf, hb = _mlstm_scan(cfg, q, k, v, gates)
            stream, h2, topi, topw = _row_call(
                _mlstm_post_body, cfg, [stream, hf, hb, og],
                [mlstm_w_o[j].astype(BF16), row(mlstm_norm_g[j])] + route_consts, route_mods, [], route_outs,
                name="mlstm_post")
        else:
            rep = GDN_V_HEADS // GDN_QK_HEADS
            qk_w = GDN_QK_HEADS * GDN_HEAD_DIM
            v_w = GDN_V_HEADS * GDN_HEAD_DIM
            ncv = 2 * qk_w + v_w
            w_main = gdn_w_in[j][:, :ncv + v_w].astype(BF16)
            w_ba = gdn_w_in[j][:, ncv + v_w:].reshape(d, 2, 2, GDN_QK_HEADS, rep).transpose(0, 3, 1, 2, 4)
            w_ba = jnp.pad(w_ba.reshape(d, GDN_QK_HEADS, 4 * rep), ((0, 0), (0, 0), (0, LANES - 4 * rep)))
            w_ba = w_ba.reshape(d, GDN_QK_HEADS * LANES).astype(BF16)

            def per_lane(p, fill):
                pv = p.astype(F32).reshape(2, GDN_QK_HEADS, rep).transpose(1, 0, 2)
                full = jnp.full((GDN_QK_HEADS, 2, 2, rep), fill, F32).at[:, :, 1, :].set(pv)
                full = jnp.pad(full.reshape(GDN_QK_HEADS, 4 * rep), ((0, 0), (0, LANES - 4 * rep)),
                               constant_values=fill)
                return full.reshape(1, GDN_QK_HEADS * LANES)

            a_scale = per_lane(jnp.exp(gdn_a_log[j].astype(F32)), 0.0)
            dt_b = per_lane(gdn_dt_bias[j], 0.0)
            dmask = per_lane(jnp.ones((2, GDN_V_HEADS), F32), 0.0)
            stream, zqkv, zg, ba = _row_call(
                functools.partial(_gdn_pre_body, has_prev), cfg, rows_in, [g1, w_main, w_ba, a_scale, dt_b, dmask],
                mods_in, [], [((m, d), F32), ((m, ncv), BF16), ((m, v_w), BF16), ((m, GDN_QK_HEADS * LANES), F32)],
                name="gdn_pre")
            conv_w = jnp.pad(gdn_conv_w[j].astype(F32), ((0, 8 - GDN_CONV), (0, 0)))
            qkv = _gdn_conv(cfg, zqkv, conv_w)
            of, ob = _gdn_scan(cfg, qkv, ba)
            stream, h2, topi, topw = _row_call(
                _gdn_post_body, cfg, [stream, of, ob, zg],
                [gdn_w_o[j].astype(BF16), row(gdn_norm_g[j])] + route_consts, route_mods, [], route_outs,
                name="gdn_post")

        y_prev = _moe_ffn(h2, topi, topw, moe_w1[i], moe_b1[i], moe_w2[i], moe_b2[i])
        gate2_prev = gate2

    out = _row_call(_final_body, cfg, [stream, y_prev], [row(final_norm_g)], [gate2_prev], [],
                    [((cfg.n_lat, d), F32)], n_tiles=cfg.lat_tiles, name="final_norm")[0]
    return out.reshape(b, s, d)
```

```python
import functools
import math

import jax
import jax.numpy as jnp
from jax import lax
from jax.experimental import pallas as pl
from jax.experimental.pallas import tpu as pltpu

F32 = jnp.float32
BF16 = jnp.bfloat16
HIGHEST = lax.Precision.HIGHEST

D_MODEL = 1024
N_MOD = 6
GRID_W = 64
ROPE_BASE = 10000.0
NORM_EPS = 1e-6
LOG2_E = math.log2(math.e)

MLA_HEADS = 16
MLA_Q_RANK = 512
MLA_KV_RANK = 256
MLA_NOPE = 64
MLA_ROPE = 32
MLA_V = 64

GQA_Q_HEADS = 8
GQA_KV_HEADS = 4
GQA_HEAD_DIM = 128

MLSTM_HEADS = 4
MLSTM_QK = 128
MLSTM_V = 256

GDN_QK_HEADS = 8
GDN_V_HEADS = 16
GDN_HEAD_DIM = 128
GDN_CONV = 5

N_EXPERTS = 32
TOP_K = 4
D_EXPERT = 1024
SWIGLU_ALPHA = 1.702
SWIGLU_LIMIT = 7.0

LANES = 128
ROW_TILE = 256
CHUNK = 64
Q_TILE = 256
GROUP_TILE = 512
RANK_TILE = 512
MOD_ROWS = 16
HALO = 16
VMEM_LIMIT = 56 * 1024 * 1024


def _params(sem):
    return pltpu.CompilerParams(dimension_semantics=sem, vmem_limit_bytes=VMEM_LIMIT)


class _Cfg:
    def __init__(self, b, s, nc):
        assert s % ROW_TILE == 0 and nc % ROW_TILE == 0 and (b * s) % nc == 0 and b + 1 <= MOD_ROWS
        self.b, self.s, self.nc = b, s, nc
        self.n_lat = b * s
        self.m = b * s + b * nc
        self.lat_tiles = self.n_lat // ROW_TILE
        self.tiles = self.m // ROW_TILE
        self.tpb = s // ROW_TILE
        self.cpb = nc // ROW_TILE


def _rms(x, g):
    return x * lax.rsqrt(jnp.mean(x * x, axis=-1, keepdims=True) + NORM_EPS) * g


def _norm_mod(x, g, scale, shift):
    return _rms(x, g) * (1.0 + scale) + shift


def _sigmoid(x):
    return 1.0 / (1.0 + jnp.exp(-x))


def _softplus(x):
    return jnp.maximum(x, 0.0) + jnp.log(1.0 + jnp.exp(-jnp.abs(x)))


def _mm(a, b):
    return jnp.dot(a.astype(BF16), b.astype(BF16), preferred_element_type=F32)


def _mm_nt(a, b):
    return lax.dot_general(a.astype(BF16), b.astype(BF16), (((1,), (1,)), ((), ())), preferred_element_type=F32)


def _ada_kernel(c_ref, w_ref, b_ref, o_ref):
    cond = c_ref[...]
    a = cond * _sigmoid(cond)
    o_ref[...] = _mm(a, w_ref[...]) + b_ref[...]


def _ada_table(cond, ada_w, ada_b):
    depth, d, nd = ada_w.shape
    return pl.pallas_call(
        _ada_kernel,
        grid=(depth, nd // d),
        in_specs=[pl.BlockSpec((MOD_ROWS, d), lambda l, n: (0, 0)),
                  pl.BlockSpec((None, d, d), lambda l, n: (l, 0, n)),
                  pl.BlockSpec((None, 1, d), lambda l, n: (l, 0, n))],
        out_specs=pl.BlockSpec((None, MOD_ROWS, d), lambda l, n: (l, 0, n)),
        out_shape=jax.ShapeDtypeStruct((depth, MOD_ROWS, nd), F32),
        compiler_params=_params(("parallel", "parallel")),
        name="ada_table",
    )(cond, ada_w, ada_b.reshape(depth, 1, nd))


def _row_call(body, cfg, rows, consts, mods, tabs, outs, n_tiles=None, name=None):
    n_tiles = cfg.tiles if n_tiles is None else n_tiles
    lat_tiles, tpb = cfg.lat_tiles, cfg.tpb
    arrays, specs = [], []
    for r in rows:
        if isinstance(r, tuple):
            arr, g = r
            arrays.append(arr)
            specs.append(pl.BlockSpec((None, ROW_TILE, arr.shape[-1]), lambda i, g=g: (g, i, 0)))
        else:
            arrays.append(r)
            specs.append(pl.BlockSpec((ROW_TILE, r.shape[-1]), lambda i: (i, 0)))
    for a in consts:
        arrays.append(a)
        specs.append(pl.BlockSpec(a.shape, lambda i, nd=a.ndim: (0,) * nd))
    for a in mods:
        arrays.append(a)
        specs.append(pl.BlockSpec((None, 1, a.shape[-1]),
                                  lambda i: (jnp.where(i < lat_tiles, 1 + i // tpb, 0), 0, 0)))
    for a in tabs:
        arrays.append(a)
        specs.append(pl.BlockSpec((None, ROW_TILE, a.shape[-1]),
                                  lambda i: (jnp.where(i < lat_tiles, 1 + i % tpb, 0), 0, 0)))
    out_shapes, out_specs = [], []
    for shape, dtype in outs:
        out_shapes.append(jax.ShapeDtypeStruct(shape, dtype))
        if shape[0] == TOP_K:
            out_specs.append(pl.BlockSpec((shape[0], ROW_TILE), lambda i: (0, i)))
        else:
            out_specs.append(pl.BlockSpec((ROW_TILE, shape[-1]), lambda i: (i, 0)))
    n_r, n_c, n_m, n_t = len(rows), len(consts), len(mods), len(tabs)

    def kern(*refs):
        r = refs[:n_r]
        c = refs[n_r:n_r + n_c]
        m = refs[n_r + n_c:n_r + n_c + n_m]
        t = refs[n_r + n_c + n_m:n_r + n_c + n_m + n_t]
        o = refs[n_r + n_c + n_m + n_t:]
        body(r, c, m, t, o)

    return pl.pallas_call(
        kern, grid=(n_tiles,), in_specs=specs, out_specs=out_specs, out_shape=out_shapes,
        compiler_params=_params(("parallel",)), name=name,
    )(*arrays)


def _residual_in(r, m, has_prev):
    if has_prev:
        return r[0][...] + m[0][...] * r[1][...]
    return r[0][...]


def _rope_pair(x, cos, sin_a, sin_b, shift):
    return x * cos + pltpu.roll(x, shift, 1) * sin_a + pltpu.roll(x, LANES - shift, 1) * sin_b


def _mla_pre_body(has_prev, r, c, m, t, o):
    g1, w_in, qg, kvg, wq, wk, wv, place, v_ones = c
    shift1, scale1 = m[-2], m[-1]
    cos_q, sa_q, sb_q, cos_k, sa_k, sb_k = [a[...] for a in t]
    x = _residual_in(r, m, has_prev)
    h = _norm_mod(x, g1[...], scale1[...], shift1[...])
    z = _mm(h, w_in[...])
    cq = _rms(z[:, :MLA_Q_RANK], qg[...]).astype(BF16)
    ckv = _rms(z[:, MLA_Q_RANK:MLA_Q_RANK + MLA_KV_RANK], kvg[...]).astype(BF16)
    kr = z[:, MLA_Q_RANK + MLA_KV_RANK:]
    kr = _rope_pair(kr, cos_k, sa_k, sb_k, MLA_ROPE // 2)
    scale = (MLA_NOPE + MLA_ROPE) ** -0.5 * LOG2_E
    q = _mm(cq, wq[...])
    for hh in range(MLA_HEADS):
        qs = q[:, hh * LANES:(hh + 1) * LANES]
        qs = _rope_pair(qs, cos_q, sa_q, sb_q, MLA_ROPE // 2)
        o[1][:, hh * LANES:(hh + 1) * LANES] = (qs * scale).astype(BF16)
    o[0][...] = x
    o[2][...] = (_mm(ckv, wk[...]) + _mm(kr, place[...])).astype(BF16)
    o[3][...] = (_mm(ckv, wv[...]) + v_ones[...]).astype(BF16)


def _gqa_pre_body(has_prev, r, c, m, t, o):
    g1, w_in, qg, kg = c
    shift1, scale1 = m[-2], m[-1]
    cos, sin = t[0][...], t[1][...]
    x = _residual_in(r, m, has_prev)
    h = _norm_mod(x, g1[...], scale1[...], shift1[...])
    z = _mm(h, w_in[...])
    scale = GQA_HEAD_DIM ** -0.5 * LOG2_E
    hd = GQA_HEAD_DIM
    for hh in range(GQA_Q_HEADS + GQA_KV_HEADS):
        zs = z[:, hh * hd:(hh + 1) * hd]
        is_q = hh < GQA_Q_HEADS
        zs = _rms(zs, qg[...] if is_q else kg[...])
        zs = zs * cos + pltpu.roll(zs, hd // 2, 1) * sin
        if is_q:
            o[1][:, hh * hd:(hh + 1) * hd] = (zs * scale).astype(BF16)
        else:
            kk = hh - GQA_Q_HEADS
            o[2][:, kk * hd:(kk + 1) * hd] = zs.astype(BF16)
    o[0][...] = x
    v0 = (GQA_Q_HEADS + GQA_KV_HEADS) * hd
    for kk in range(GQA_KV_HEADS):
        o[3][:, 2 * kk * hd:(2 * kk + 1) * hd] = z[:, v0 + kk * hd:v0 + (kk + 1) * hd].astype(BF16)
        o[3][:, (2 * kk + 1) * hd:(2 * kk + 2) * hd] = jnp.ones((z.shape[0], hd), BF16)


def _mlstm_pre_body(has_prev, r, c, m, t, o):
    g1, w_in, gate_b, fmask = c
    shift1, scale1 = m[-2], m[-1]
    x = _residual_in(r, m, has_prev)
    h = _norm_mod(x, g1[...], scale1[...], shift1[...])
    z = _mm(h, w_in[...])
    nqk = MLSTM_HEADS * MLSTM_QK
    nv = MLSTM_HEADS * MLSTM_V
    o[0][...] = x
    o[1][...] = z[:, :nqk].astype(BF16)
    o[2][...] = (z[:, nqk:2 * nqk] * MLSTM_QK ** -0.5).astype(BF16)
    o[3][...] = z[:, 2 * nqk:2 * nqk + nv].astype(BF16)
    o[4][...] = z[:, 2 * nqk + nv:2 * nqk + 2 * nv].astype(BF16)
    gt = z[:, 2 * nqk + 2 * nv:] + gate_b[...]
    log_sig = jnp.minimum(gt, 0.0) - jnp.log(1.0 + jnp.exp(-jnp.abs(gt)))
    o[5][...] = jnp.where(fmask[...] > 0.0, log_sig, gt)


def _gdn_pre_body(has_prev, r, c, m, t, o):
    g1, w_in, w_ba, a_scale, dt_b, dmask = c
    shift1, scale1 = m[-2], m[-1]
    x = _residual_in(r, m, has_prev)
    h = _norm_mod(x, g1[...], scale1[...], shift1[...]).astype(BF16)
    ncv = 2 * GDN_QK_HEADS * GDN_HEAD_DIM + GDN_V_HEADS * GDN_HEAD_DIM
    z = _mm(h, w_in[...])
    o[0][...] = x
    o[1][...] = z[:, :ncv].astype(BF16)
    o[2][...] = z[:, ncv:].astype(BF16)
    ba = _mm(h, w_ba[...])
    o[3][...] = jnp.where(dmask[...] > 0.0, -a_scale[...] * _softplus(ba + dt_b[...]), _sigmoid(ba))


def _route_tail(x, y, c, m, o):
    g2, rw_t, rb = c
    gate1, shift2, scale2 = m
    x1 = x + gate1[...] * y
    h2 = _norm_mod(x1, g2[...], scale2[...], shift2[...])
    o[0][...] = x1
    o[1][...] = h2.astype(BF16)
    logits = lax.dot_general(rw_t[...], h2, (((1,), (1,)), ((), ())), preferred_element_type=F32,
                             precision=HIGHEST) + rb[...]
    e_iota = lax.broadcasted_iota(jnp.int32, logits.shape, 0)
    vals = []
    for k in range(TOP_K):
        mx = jnp.max(logits, axis=0, keepdims=True)
        idx = jnp.min(jnp.where(logits == mx, e_iota, N_EXPERTS), axis=0, keepdims=True)
        o[2][k:k + 1, :] = idx
        vals.append(mx)
        logits = jnp.where(e_iota == idx, -jnp.inf, logits)
    es = [jnp.exp(v - vals[0]) for v in vals]
    tot = es[0] + es[1] + es[2] + es[3]
    for k in range(TOP_K):
        o[3][k:k + 1, :] = es[k] / tot


def _attn_post_body(r, c, m, t, o):
    x, a = r
    y = _mm(a[...], c[0][...])
    _route_tail(x[...], y, c[1:], m, o)


def _mlstm_post_body(r, c, m, t, o):
    x, hf, hb, og = r
    w_o, ng = c[0], c[1]
    hs = hf[...] + hb[...]
    gate = _sigmoid(og[...].astype(F32))
    parts = []
    for hh in range(MLSTM_HEADS):
        sl = slice(hh * MLSTM_V, (hh + 1) * MLSTM_V)
        parts.append((_rms(hs[:, sl], ng[...]) * gate[:, sl]).astype(BF16))
    y = _mm(jnp.concatenate(parts, axis=1), w_o[...])
    _route_tail(x[...], y, c[2:], m, o)


def _gdn_post_body(r, c, m, t, o):
    x, of, ob, zg = r
    w_o, ng = c[0], c[1]
    os_ = of[...] + ob[...]
    g = zg[...].astype(F32)
    gate = g * _sigmoid(g)
    parts = []
    for hh in range(GDN_V_HEADS):
        sl = slice(hh * GDN_HEAD_DIM, (hh + 1) * GDN_HEAD_DIM)
        parts.append((_rms(os_[:, sl], ng[...]) * gate[:, sl]).astype(BF16))
    y = _mm(jnp.concatenate(parts, axis=1), w_o[...])
    _route_tail(x[...], y, c[2:], m, o)


def _final_body(r, c, m, t, o):
    x = r[0][...] + m[0][...] * r[1][...]
    o[0][...] = _rms(x, c[0][...])


def _attn_kernel(*refs, n_seg, k_stride, v_width):
    q_ref = refs[0]
    segs = [(refs[1 + 2 * i], refs[2 + 2 * i]) for i in range(n_seg)]
    o_ref = refs[1 + 2 * n_seg]
    for j in range(2):
        q = q_ref[:, j * LANES:(j + 1) * LANES]
        ss = [_mm_nt(q, k_ref[:, j * k_stride:j * k_stride + LANES]) for k_ref, _ in segs]
        mx = ss[0].max(axis=-1, keepdims=True)
        for s in ss[1:]:
            mx = jnp.maximum(mx, s.max(axis=-1, keepdims=True))
        acc = 0.0
        voff = j * 2 * v_width if k_stride else 0
        for s, (_, v_ref) in zip(ss, segs):
            acc = acc + _mm(jnp.exp2(s - mx), v_ref[:, voff:voff + 2 * v_width])
        o_ref[:, j * v_width:(j + 1) * v_width] = (acc[:, :v_width] / acc[:, v_width:v_width + 1]).astype(o_ref.dtype)


def _attention(cfg, q, k, v, n_groups, k_stride, v_width, latent):
    kw = 2 * LANES if k_stride else LANES
    vw = 4 * v_width if k_stride else 2 * v_width
    ow = 2 * v_width
    ctx_blk0 = cfg.n_lat // cfg.nc
    k_ctx = pl.BlockSpec((cfg.nc, kw), lambda b, g, i: (ctx_blk0 + b, g))
    v_ctx = pl.BlockSpec((cfg.nc, vw), lambda b, g, i: (ctx_blk0 + b, g))
    if latent:
        qpb = cfg.s // Q_TILE
        grid = (cfg.b, n_groups, qpb)
        q_spec = pl.BlockSpec((Q_TILE, 2 * LANES), lambda b, g, i: (b * qpb + i, g))
        o_spec = pl.BlockSpec((Q_TILE, ow), lambda b, g, i: (b * qpb + i, g))
        in_specs = [q_spec, pl.BlockSpec((cfg.s, kw), lambda b, g, i: (b, g)),
                    pl.BlockSpec((cfg.s, vw), lambda b, g, i: (b, g)), k_ctx, v_ctx]
        args = (q, k, v, k, v)
        n_seg = 2
        rows = cfg.n_lat
    else:
        qpb = cfg.nc // Q_TILE
        q0 = cfg.n_lat // Q_TILE
        grid = (cfg.b, n_groups, qpb)
        q_spec = pl.BlockSpec((Q_TILE, 2 * LANES), lambda b, g, i: (q0 + b * qpb + i, g))
        o_spec = pl.BlockSpec((Q_TILE, ow), lambda b, g, i: (b * qpb + i, g))
        in_specs = [q_spec, k_ctx, v_ctx]
        args = (q, k, v)
        n_seg = 1
        rows = cfg.b * cfg.nc
    return pl.pallas_call(
        functools.partial(_attn_kernel, n_seg=n_seg, k_stride=k_stride, v_width=v_width),
        grid=grid, in_specs=in_specs, out_specs=o_spec,
        out_shape=jax.ShapeDtypeStruct((rows, n_groups * ow), BF16),
        compiler_params=_params(("parallel", "parallel", "arbitrary")),
        name="attention_lat" if latent else "attention_ctx",
    )(*args)


def _chunk_masks(flip):
    li = lax.broadcasted_iota(jnp.int32, (CHUNK, CHUNK), 0)
    si = lax.broadcasted_iota(jnp.int32, (CHUNK, CHUNK), 1)
    eye = li == si
    incl = (si >= li) if flip else (si <= li)
    strict = (si > li) if flip else (si < li)
    return eye, incl, strict


def _to_row(col, eye):
    return jnp.sum(jnp.where(eye, col, 0.0), axis=0, keepdims=True)


def _cumsum_col(col, eye, incl):
    row = _to_row(col, eye)
    cum_col = jnp.sum(jnp.where(incl, row, 0.0), axis=1, keepdims=True)
    return cum_col, _to_row(cum_col, eye)


def _scan_block_index(cfg, flip):
    cpb, tpb, lat_tiles = cfg.cpb, cfg.tpb, cfg.lat_tiles

    def row_block(b, j):
        if flip:
            ctx = lat_tiles + b * cpb + (cpb - 1 - j)
            lat = b * tpb + (tpb - 1 - (j - cpb))
        else:
            ctx = lat_tiles + b * cpb + j
            lat = b * tpb + (j - cpb)
        return jnp.where(j < cpb, ctx, lat)

    return row_block


def _mlstm_kernel(q_ref, k_ref, v_ref, g_ref, o_ref, c_st, n_st, m_st, *, flip, d):
    @pl.when(pl.program_id(1) == 0)
    def _():
        c_st[...] = jnp.zeros_like(c_st)
        n_st[...] = jnp.zeros_like(n_st)
        m_st[...] = jnp.zeros_like(m_st)

    eye, incl, _ = _chunk_masks(flip)
    n_chunks = ROW_TILE // CHUNK
    for ci in range(n_chunks):
        cc = n_chunks - 1 - ci if flip else ci
        rows = slice(cc * CHUNK, (cc + 1) * CHUNK)
        for hh in range(MLSTM_HEADS):
            q = q_ref[rows, hh * MLSTM_QK:(hh + 1) * MLSTM_QK]
            k = k_ref[rows, hh * MLSTM_QK:(hh + 1) * MLSTM_QK]
            v = v_ref[rows, hh * MLSTM_V:(hh + 1) * MLSTM_V]
            icol = d * 2 * MLSTM_HEADS + hh
            fcol = icol + MLSTM_HEADS
            i_col = g_ref[rows, icol:icol + 1]
            f_col = g_ref[rows, fcol:fcol + 1]
            c_prev, n_prev, m_prev = c_st[hh], n_st[hh], m_st[hh]

            i_row = _to_row(i_col, eye)
            bcum_col, bcum_row = _cumsum_col(f_col, eye, incl)
            b_end = jnp.sum(f_col, axis=0, keepdims=True)
            w_log = b_end - bcum_col + i_col
            m_new = jnp.maximum(b_end + m_prev, jnp.max(w_log, axis=0, keepdims=True))
            carry = jnp.exp(b_end + m_prev - m_new)
            kw = k.astype(F32) * jnp.exp(w_log - m_new)

            g_log = bcum_col + m_prev
            d_log = jnp.where(incl, bcum_col - bcum_row + i_row, -jnp.inf)
            m_t = jnp.maximum(g_log, jnp.max(d_log, axis=1, keepdims=True))
            inter = jnp.exp(g_log - m_t)
            s = _mm_nt(q, k) * jnp.exp(d_log - m_t)
            num = inter * _mm(q, c_prev) + _mm(s, v)
            den = inter * jnp.sum(q.astype(F32) * n_prev, axis=1, keepdims=True) + jnp.sum(s, axis=1, keepdims=True)
            h_out = num / jnp.maximum(jnp.abs(den), jnp.exp(-m_t))
            o_ref[rows, hh * MLSTM_V:(hh + 1) * MLSTM_V] = h_out

            c_st[hh] = carry * c_prev + _mm(kw.T, v)
            n_st[hh] = carry * n_prev + jnp.sum(kw, axis=0, keepdims=True)
            m_st[hh] = m_new


def _mlstm_scan(cfg, q, k, v, gates):
    nqk, nv = MLSTM_HEADS * MLSTM_QK, MLSTM_HEADS * MLSTM_V
    outs = []
    for d in range(2):
        flip = d == 1
        rb = _scan_block_index(cfg, flip)
        row = lambda w, rb=rb: pl.BlockSpec((ROW_TILE, w), lambda b, j: (rb(b, j), 0))
        outs.append(pl.pallas_call(
            functools.partial(_mlstm_kernel, flip=flip, d=d),
            grid=(cfg.b, cfg.cpb + cfg.tpb),
            in_specs=[row(nqk), row(nqk), row(nv), row(LANES)],
            out_specs=row(nv),
            out_shape=jax.ShapeDtypeStruct((cfg.m, nv), F32),
            scratch_shapes=[pltpu.VMEM((MLSTM_HEADS, MLSTM_QK, MLSTM_V), F32),
                            pltpu.VMEM((MLSTM_HEADS, 1, MLSTM_QK), F32),
                            pltpu.VMEM((MLSTM_HEADS, 1, 1), F32)],
            compiler_params=_params(("parallel", "arbitrary")),
            name="mlstm_bwd" if flip else "mlstm_fwd",
        )(q, k, v, gates))
    return outs


def _block_masks(flip):
    li = lax.broadcasted_iota(jnp.int32, (ROW_TILE, ROW_TILE), 0)
    si = lax.broadcasted_iota(jnp.int32, (ROW_TILE, ROW_TILE), 1)
    same = (li // CHUNK) == (si // CHUNK)
    eye = li == si
    incl = same & ((si >= li) if flip else (si <= li))
    strict = same & ((si > li) if flip else (si < li))
    return eye, same, incl, strict


def _gdn_kernel(q_ref, k_ref, v_ref, ba_ref, o_ref, s_st, *, flip, d):
    @pl.when(pl.program_id(2) == 0)
    def _():
        s_st[...] = jnp.zeros_like(s_st)

    eye, same, incl, strict = _block_masks(flip)
    eye_f = jnp.where(eye, 1.0, 0.0)
    rep = GDN_V_HEADS // GDN_QK_HEADS
    hd = GDN_HEAD_DIM
    n_chunks = ROW_TILE // CHUNK
    q = q_ref[...]
    k = k_ref[...]
    kf = k.astype(F32)
    qf = q.astype(F32)
    qk = _mm_nt(q, k)
    for jj in range(rep):
        v = v_ref[:, jj * hd:(jj + 1) * hd].astype(F32)
        bcol = d * 2 * rep + jj
        beta = ba_ref[:, bcol:bcol + 1]
        ld = ba_ref[:, bcol + rep:bcol + rep + 1]
        ld_row = jnp.sum(jnp.where(eye, ld, 0.0), axis=0, keepdims=True)
        gc_col = jnp.sum(jnp.where(incl, ld_row, 0.0), axis=1, keepdims=True)
        gc_row = jnp.sum(jnp.where(eye, gc_col, 0.0), axis=0, keepdims=True)
        gce_col = jnp.sum(jnp.where(same, ld_row, 0.0), axis=1, keepdims=True)
        decay = jnp.exp(jnp.where(incl, gc_col - gc_row, -jnp.inf))
        kb = kf * beta
        nil = jnp.where(strict, _mm_nt(kb, k) * decay, 0.0)
        inv = eye_f - nil
        pw = nil
        for _ in range(int(math.log2(CHUNK)) - 1):
            pw = _mm(pw, pw)
            inv = inv + _mm(inv, pw)
        eg = jnp.exp(gc_col)
        uw = _mm(inv, jnp.concatenate([v * beta, kb * eg], axis=1))
        kdec = kf * jnp.exp(gce_col - gc_col)
        aw = _mm(qk * decay, uw)
        qe = qf * eg - aw[:, hd:]
        ou = aw[:, :hd]
        for ci in range(n_chunks):
            cc = n_chunks - 1 - ci if flip else ci
            rows = slice(cc * CHUNK, (cc + 1) * CHUNK)
            bg = _mm(kdec[rows].T, uw[rows])
            s_prev = s_st[jj]
            o_ref[rows, jj * hd:(jj + 1) * hd] = _mm(qe[rows], s_prev) + ou[rows]
            g_end = jnp.exp(gce_col[cc * CHUNK:cc * CHUNK + 1])
            s_st[jj] = g_end * s_prev - _mm(bg[:, hd:], s_prev) + bg[:, :hd]


def _gdn_scan(cfg, qkv, ba):
    rep = GDN_V_HEADS // GDN_QK_HEADS
    hd = GDN_HEAD_DIM
    outs = []
    for d in range(2):
        flip = d == 1
        rb = _scan_block_index(cfg, flip)
        col = lambda w, off, rb=rb: pl.BlockSpec((ROW_TILE, w), lambda b, h, j: (rb(b, j), off + h))
        outs.append(pl.pallas_call(
            functools.partial(_gdn_kernel, flip=flip, d=d),
            grid=(cfg.b, GDN_QK_HEADS, cfg.cpb + cfg.tpb),
            in_specs=[col(hd, 0), col(hd, GDN_QK_HEADS), col(rep * hd, 2 * GDN_QK_HEADS // rep), col(LANES, 0)],
            out_specs=col(rep * hd, 0),
            out_shape=jax.ShapeDtypeStruct((cfg.m, GDN_V_HEADS * hd), F32),
            scratch_shapes=[pltpu.VMEM((rep, hd, hd), F32)],
            compiler_params=_params(("parallel", "parallel", "arbitrary")),
            name="gdn_bwd" if flip else "gdn_fwd",
        )(qkv, qkv, qkv, ba))
    return outs


def _gdn_conv_kernel(cur_ref, prev_ref, next_ref, w_ref, o_ref, pad_ref, *, lat_tiles, tpb, cpb):
    i = pl.program_id(0)
    cb = pl.program_id(1)
    lat = i < lat_tiles
    pos = jnp.where(lat, i % tpb, (i - lat_tiles) % cpb)
    per = jnp.where(lat, tpb, cpb)
    first = pos == 0
    last = pos == per - 1
    prev = prev_ref[...].astype(F32)
    nxt = next_ref[...].astype(F32)
    pad_ref[0:HALO, :] = jnp.where(first, 0.0, prev)
    pad_ref[HALO:HALO + ROW_TILE, :] = cur_ref[...].astype(F32)
    pad_ref[HALO + ROW_TILE:, :] = jnp.where(last, 0.0, nxt)
    half = GDN_CONV // 2
    acc = 0.0
    for tap in range(GDN_CONV):
        off = HALO - half + tap
        acc = acc + pad_ref[off:off + ROW_TILE, :] * w_ref[tap:tap + 1, :]
    act = acc * _sigmoid(acc)
    qscale = jnp.where(cb == 0, GDN_HEAD_DIM ** -0.5, 1.0)
    hd = GDN_HEAD_DIM
    for hh in range(act.shape[1] // hd):
        a = act[:, hh * hd:(hh + 1) * hd]
        nrm = a * lax.rsqrt(jnp.sum(a * a, axis=-1, keepdims=True) + NORM_EPS) * qscale
        o_ref[:, hh * hd:(hh + 1) * hd] = jnp.where(cb < 2, nrm, a).astype(o_ref.dtype)


def _gdn_conv(cfg, zqkv, conv_w):
    cw = GDN_QK_HEADS * GDN_HEAD_DIM
    n_cb = zqkv.shape[1] // cw
    hpt = ROW_TILE // HALO
    n_halo = cfg.m // HALO
    return pl.pallas_call(
        functools.partial(_gdn_conv_kernel, lat_tiles=cfg.lat_tiles, tpb=cfg.tpb, cpb=cfg.cpb),
        grid=(cfg.tiles, n_cb),
        in_specs=[pl.BlockSpec((ROW_TILE, cw), lambda i, c: (i, c)),
                  pl.BlockSpec((HALO, cw), lambda i, c: (jnp.maximum(i * hpt - 1, 0), c)),
                  pl.BlockSpec((HALO, cw), lambda i, c: (jnp.minimum((i + 1) * hpt, n_halo - 1), c)),
                  pl.BlockSpec((8, cw), lambda i, c: (0, c))],
        out_specs=pl.BlockSpec((ROW_TILE, cw), lambda i, c: (i, c)),
        out_shape=jax.ShapeDtypeStruct(zqkv.shape, BF16),
        scratch_shapes=[pltpu.VMEM((ROW_TILE + 2 * HALO, cw), F32)],
        compiler_params=_params(("parallel", "parallel")),
        name="gdn_conv",
    )(zqkv, zqkv, zqkv, conv_w)


def _rank_kernel(e_ref, rank_ref, cnt_ref, run_ref):
    @pl.when(pl.program_id(0) == 0)
    def _():
        run_ref[...] = jnp.zeros_like(run_ref)

    e = e_ref[...]
    onehot = (lax.broadcasted_iota(jnp.int32, (N_EXPERTS, RANK_TILE), 0) == e)
    oh = jnp.where(onehot, 1.0, 0.0)
    ji = lax.broadcasted_iota(jnp.int32, (RANK_TILE, RANK_TILE), 0)
    si = lax.broadcasted_iota(jnp.int32, (RANK_TILE, RANK_TILE), 1)
    upper = jnp.where(ji <= si, 1.0, 0.0).astype(BF16)
    cum = jnp.dot(oh.astype(BF16), upper, preferred_element_type=F32)
    run = run_ref[:, 0:1]
    rank = jnp.sum(oh * (cum - 1.0 + run), axis=0, keepdims=True)
    rank_ref[...] = rank.astype(jnp.int32)
    run_ref[...] = run_ref[...] + jnp.sum(oh, axis=1, keepdims=True)
    cnt_ref[...] = run_ref[...]


def _rank_pairs(expert_of_pair):
    n_pairs = expert_of_pair.shape[1]
    return pl.pallas_call(
        _rank_kernel,
        grid=(n_pairs // RANK_TILE,),
        in_specs=[pl.BlockSpec((1, RANK_TILE), lambda i: (0, i))],
        out_specs=[pl.BlockSpec((1, RANK_TILE), lambda i: (0, i)),
                   pl.BlockSpec((N_EXPERTS, LANES), lambda i: (0, 0))],
        out_shape=[jax.ShapeDtypeStruct((1, n_pairs), jnp.int32),
                   jax.ShapeDtypeStruct((N_EXPERTS, LANES), F32)],
        scratch_shapes=[pltpu.VMEM((N_EXPERTS, LANES), F32)],
        compiler_params=_params(("arbitrary",)),
        name="moe_rank",
    )(expert_of_pair)


def _moe_kernel(te_ref, nu_ref, x_ref, w1_ref, b1g_ref, b1l_ref, w2_ref, b2_ref, o_ref, w1g_s, w1l_s, w2_s):
    j = pl.program_id(0)
    used = j < nu_ref[0]
    new_expert = jnp.logical_or(j == 0, te_ref[j] != te_ref[jnp.maximum(j - 1, 0)])

    @pl.when(jnp.logical_and(used, new_expert))
    def _():
        slab = 2 * LANES
        src = lax.broadcasted_iota(jnp.int32, (slab, slab), 0)
        dst = lax.broadcasted_iota(jnp.int32, (slab, slab), 1)
        want = jnp.where(dst < LANES, 2 * dst, 2 * (dst - LANES) + 1)
        perm = jnp.where(src == want, 1.0, 0.0).astype(BF16)
        for sb in range(w1_ref.shape[1] // slab):
            sorted_cols = jnp.dot(w1_ref[:, sb * slab:(sb + 1) * slab].astype(BF16), perm,
                                  preferred_element_type=F32)
            w1g_s[:, sb * LANES:(sb + 1) * LANES] = sorted_cols[:, :LANES].astype(BF16)
            w1l_s[:, sb * LANES:(sb + 1) * LANES] = sorted_cols[:, LANES:].astype(BF16)
        w2_s[...] = w2_ref[...].astype(BF16)

    @pl.when(used)
    def _():
        x = x_ref[...]
        glu = jnp.minimum(jnp.dot(x, w1g_s[...], preferred_element_type=F32) + b1g_ref[...], SWIGLU_LIMIT)
        lin = jnp.clip(jnp.dot(x, w1l_s[...], preferred_element_type=F32) + b1l_ref[...],
                       -SWIGLU_LIMIT, SWIGLU_LIMIT)
        act = glu * _sigmoid(SWIGLU_ALPHA * glu) * (lin + 1.0)
        y = jnp.dot(act.astype(BF16), w2_s[...], preferred_element_type=F32) + b2_ref[...]
        o_ref[...] = y.astype(o_ref.dtype)

    @pl.when(jnp.logical_not(used))
    def _():
        o_ref[...] = jnp.zeros_like(o_ref)


def _moe_grouped(xs, tile_expert, n_used, layer, w1_all, b1g, b1l, w2_all, b2):
    n_slots, d = xs.shape
    de = w2_all.shape[2]
    wspec = lambda r, c: pl.BlockSpec((None, r, c), lambda j, te, nu: (te[j], 0, 0))
    lspec = lambda r, c: pl.BlockSpec((None, None, r, c), lambda j, te, nu: (layer, te[j], 0, 0))
    return pl.pallas_call(
        _moe_kernel,
        grid_spec=pltpu.PrefetchScalarGridSpec(
            num_scalar_prefetch=2,
            grid=(n_slots // GROUP_TILE,),
            in_specs=[pl.BlockSpec((GROUP_TILE, d), lambda j, te, nu: (j, 0)),
                      lspec(d, 2 * de), wspec(1, de), wspec(1, de), lspec(de, d), wspec(1, d)],
            out_specs=pl.BlockSpec((GROUP_TILE, d), lambda j, te, nu: (j, 0)),
            scratch_shapes=[pltpu.VMEM((d, de), BF16), pltpu.VMEM((d, de), BF16), pltpu.VMEM((de, d), BF16)]),
        out_shape=jax.ShapeDtypeStruct((n_slots, d), BF16),
        compiler_params=_params(("arbitrary",)),
        name="moe_grouped",
    )(tile_expert, n_used, xs, w1_all, b1g, b1l, w2_all, b2)


def _moe_ffn(h2, topi, topw, layer, w1_all, b1, w2_all, b2):
    n_tok = h2.shape[0]
    n_pairs = TOP_K * n_tok
    assert n_pairs % RANK_TILE == 0
    n_slots = -(-(n_pairs + N_EXPERTS * (GROUP_TILE - 1)) // GROUP_TILE) * GROUP_TILE
    n_tiles = n_slots // GROUP_TILE

    eid = topi.reshape(1, n_pairs)
    rank, counts = _rank_pairs(eid)
    counts = counts[:, 0].astype(jnp.int32)
    padded = (counts + GROUP_TILE - 1) // GROUP_TILE * GROUP_TILE
    ends = jnp.cumsum(padded)
    starts = ends - padded
    slot = starts[eid[0]] + rank[0]
    token = jnp.tile(jnp.arange(n_tok, dtype=jnp.int32), TOP_K)
    src = jnp.zeros((n_slots,), jnp.int32).at[slot].set(token)
    tile_start = jnp.arange(n_tiles, dtype=jnp.int32) * GROUP_TILE
    tile_expert = jnp.sum((ends[None, :] <= tile_start[:, None]).astype(jnp.int32), axis=1)
    tile_expert = jnp.minimum(tile_expert, N_EXPERTS - 1)
    n_used = (ends[-1] // GROUP_TILE).astype(jnp.int32).reshape(1)

    xs = jnp.take(h2, src, axis=0)
    b1g = b1[:, None, 0::2]
    b1l = b1[:, None, 1::2]
    ys = _moe_grouped(xs, tile_expert, n_used, layer, w1_all, b1g, b1l, w2_all, b2[:, None, :])

    pos = slot.reshape(TOP_K, n_tok)
    y = jnp.zeros((n_tok, h2.shape[1]), F32)
    for k in range(TOP_K):
        y = y + topw[k][:, None] * jnp.take(ys, pos[k], axis=0).astype(F32)
    return y


def _axial_angles(n_tokens, rot_dim):
    t = jnp.arange(n_tokens, dtype=jnp.int32)
    rows = (t // GRID_W).astype(F32)
    cols = (t % GRID_W).astype(F32)
    d_axis = rot_dim // 2
    inv_freq = ROPE_BASE ** (-jnp.arange(0, d_axis, 2, dtype=F32) / d_axis)
    ang = jnp.concatenate([rows[:, None] * inv_freq, cols[:, None] * inv_freq], axis=-1)
    return jnp.cos(ang), jnp.sin(ang)


def _seq_table(cfg, lat_rows, ctx_value):
    c = lat_rows.shape[-1]
    ctx = jnp.full((1, ROW_TILE, c), ctx_value, F32)
    return jnp.concatenate([ctx, lat_rows.reshape(cfg.tpb, ROW_TILE, c)], axis=0)


def _slab_rope_tables(cfg, rot_dim, lane0):
    cos, sin = _axial_angles(cfg.s, rot_dim)
    half = rot_dim // 2
    s = cfg.s
    ones = lambda n: jnp.ones((s, n), F32)
    zeros = lambda n: jnp.zeros((s, n), F32)
    tail = LANES - lane0 - rot_dim
    cos_t = jnp.concatenate([ones(lane0), cos, cos, ones(tail)], axis=1)
    sin_a = jnp.concatenate([zeros(lane0 + half), sin, zeros(tail)], axis=1)
    sin_b = jnp.concatenate([zeros(lane0), -sin, zeros(half + tail)], axis=1)
    return _seq_table(cfg, cos_t, 1.0), _seq_table(cfg, sin_a, 0.0), _seq_table(cfg, sin_b, 0.0)


def _mod_slices(table, n_rows):
    d = D_MODEL
    return [table[:, k * d:(k + 1) * d].reshape(n_rows, 1, d) for k in range(N_MOD)]


def kernel(x, c, ctx, c_ctx, ada_w, ada_b, norm1_g, norm2_g, router_w, router_b, moe_w1, moe_b1, moe_w2, moe_b2,
           mla_w_in, mla_q_norm_g, mla_kv_norm_g, mla_w_uq, mla_w_ukv, mla_w_o, gqa_w_in, gqa_q_norm_g,
           gqa_k_norm_g, gqa_w_o, mlstm_w_in, mlstm_gate_b, mlstm_norm_g, mlstm_w_o, gdn_w_in, gdn_conv_w,
           gdn_a_log, gdn_dt_bias, gdn_norm_g, gdn_w_o, final_norm_g):
    b, s, d = x.shape
    nc = ctx.shape[1]
    depth = ada_w.shape[0]
    cfg = _Cfg(b, s, nc)
    m = cfg.m
    row = lambda a: a.reshape(1, -1).astype(F32)

    cond = jnp.concatenate([c_ctx[None, :], c, jnp.zeros((MOD_ROWS - 1 - b, d), F32)], axis=0)
    mod_all = _ada_table(cond, ada_w, ada_b)

    stream = jnp.concatenate([x.reshape(b * s, d), ctx.reshape(b * nc, d)], axis=0)
    y_prev = None
    gate2_prev = None
    for i in range(depth):
        kind, j = i % 4, i // 4
        shift1, scale1, gate1, shift2, scale2, gate2 = _mod_slices(mod_all[i], MOD_ROWS)
        has_prev = y_prev is not None
        rows_in = [stream, y_prev] if has_prev else [stream]
        mods_in = ([gate2_prev] if has_prev else []) + [shift1, scale1]
        g1 = row(norm1_g[i])
        route_consts = [row(norm2_g[i]), router_w[i].T, router_b[i].reshape(N_EXPERTS, 1)]
        route_mods = [gate1, shift2, scale2]
        route_outs = [((m, d), F32), ((m, d), BF16), ((TOP_K, m), jnp.int32), ((TOP_K, m), F32)]

        if kind == 0:
            qk = MLA_NOPE + MLA_ROPE
            w_in = jnp.pad(mla_w_in[j], ((0, 0), (0, LANES - MLA_ROPE))).astype(BF16)
            wq = jnp.pad(mla_w_uq[j].reshape(MLA_Q_RANK, MLA_HEADS, qk), ((0, 0), (0, 0), (0, LANES - qk)))
            wq = wq.reshape(MLA_Q_RANK, MLA_HEADS * LANES).astype(BF16)
            wkv = mla_w_ukv[j].reshape(MLA_KV_RANK, MLA_HEADS, MLA_NOPE + MLA_V)
            wk = jnp.pad(wkv[:, :, :MLA_NOPE], ((0, 0), (0, 0), (0, LANES - MLA_NOPE)))
            wk = wk.reshape(MLA_KV_RANK, MLA_HEADS * LANES).astype(BF16)
            wv = jnp.pad(wkv[:, :, MLA_NOPE:], ((0, 0), (0, 0), (0, MLA_V)))
            wv = wv.reshape(MLA_KV_RANK, MLA_HEADS * 2 * MLA_V).astype(BF16)
            v_ones = jnp.tile(jnp.concatenate([jnp.zeros((MLA_V,), F32), jnp.ones((MLA_V,), F32)]), MLA_HEADS)
            v_ones = v_ones.reshape(1, MLA_HEADS * 2 * MLA_V)
            place = jnp.zeros((LANES, LANES), F32).at[jnp.arange(MLA_ROPE), MLA_NOPE + jnp.arange(MLA_ROPE)].set(1.0)
            place = jnp.tile(place, (1, MLA_HEADS)).astype(BF16)
            tabs = list(_slab_rope_tables(cfg, MLA_ROPE, MLA_NOPE)) + list(_slab_rope_tables(cfg, MLA_ROPE, 0))
            stream, q, k, v = _row_call(
                functools.partial(_mla_pre_body, has_prev), cfg, rows_in,
                [g1, w_in, row(mla_q_norm_g[j]), row(mla_kv_norm_g[j]), wq, wk, wv, place, v_ones], mods_in, tabs,
                [((m, d), F32), ((m, MLA_HEADS * LANES), BF16), ((m, MLA_HEADS * LANES), BF16),
                 ((m, MLA_HEADS * 2 * MLA_V), BF16)], name="mla_pre")
            a_lat = _attention(cfg, q, k, v, MLA_HEADS // 2, LANES, MLA_V, True)
            a_ctx = _attention(cfg, q, k, v, MLA_HEADS // 2, LANES, MLA_V, False)
            a = jnp.concatenate([a_lat, a_ctx], axis=0)
            stream, h2, topi, topw = _row_call(
                _attn_post_body, cfg, [stream, a], [mla_w_o[j].astype(BF16)] + route_consts, route_mods, [],
                route_outs, name="mla_post")
        elif kind == 1:
            cos, sin = _axial_angles(cfg.s, GQA_HEAD_DIM)
            tabs = [_seq_table(cfg, jnp.concatenate([cos, cos], axis=1), 1.0),
                    _seq_table(cfg, jnp.concatenate([-sin, sin], axis=1), 0.0)]
            stream, q, k, v = _row_call(
                functools.partial(_gqa_pre_body, has_prev), cfg, rows_in,
                [g1, gqa_w_in[j].astype(BF16), row(gqa_q_norm_g[j]), row(gqa_k_norm_g[j])], mods_in, tabs,
                [((m, d), F32), ((m, GQA_Q_HEADS * GQA_HEAD_DIM), BF16), ((m, GQA_KV_HEADS * GQA_HEAD_DIM), BF16),
                 ((m, GQA_KV_HEADS * 2 * GQA_HEAD_DIM), BF16)], name="gqa_pre")
            a_lat = _attention(cfg, q, k, v, GQA_KV_HEADS, 0, GQA_HEAD_DIM, True)
            a_ctx = _attention(cfg, q, k, v, GQA_KV_HEADS, 0, GQA_HEAD_DIM, False)
            a = jnp.concatenate([a_lat, a_ctx], axis=0)
            stream, h2, topi, topw = _row_call(
                _attn_post_body, cfg, [stream, a], [gqa_w_o[j].astype(BF16)] + route_consts, route_mods, [],
                route_outs, name="gqa_post")
        elif kind == 2:
            n_gate = 4 * MLSTM_HEADS
            w_in = jnp.pad(mlstm_w_in[j], ((0, 0), (0, LANES - n_gate))).astype(BF16)
            gate_b = jnp.pad(mlstm_gate_b[j].reshape(1, n_gate).astype(F32), ((0, 0), (0, LANES - n_gate)))
            col = jnp.arange(LANES)
            fmask = (((col // MLSTM_HEADS) % 2 == 1) & (col < n_gate)).astype(F32).reshape(1, LANES)
            nqk, nv = MLSTM_HEADS * MLSTM_QK, MLSTM_HEADS * MLSTM_V
            stream, q, k, v, og, gates = _row_call(
                functools.partial(_mlstm_pre_body, has_prev), cfg, rows_in, [g1, w_in, gate_b, fmask], mods_in, [],
                [((m, d), F32), ((m, nqk), BF16), ((m, nqk), BF16), ((m, nv), BF16), ((m, nv), BF16),
                 ((m, LANES), F32)], name="mlstm_pre")
            hf, hb = _mlstm_scan(cfg, q, k, v, gates)
            stream, h2, topi, topw = _row_call(
                _mlstm_post_body, cfg, [stream, hf, hb, og],
                [mlstm_w_o[j].astype(BF16), row(mlstm_norm_g[j])] + route_consts, route_mods, [], route_outs,
                name="mlstm_post")
        else:
            rep = GDN_V_HEADS // GDN_QK_HEADS
            qk_w = GDN_QK_HEADS * GDN_HEAD_DIM
            v_w = GDN_V_HEADS * GDN_HEAD_DIM
            ncv = 2 * qk_w + v_w
            w_main = gdn_w_in[j][:, :ncv + v_w].astype(BF16)
            w_ba = gdn_w_in[j][:, ncv + v_w:].reshape(d, 2, 2, GDN_QK_HEADS, rep).transpose(0, 3, 1, 2, 4)
            w_ba = jnp.pad(w_ba.reshape(d, GDN_QK_HEADS, 4 * rep), ((0, 0), (0, 0), (0, LANES - 4 * rep)))
            w_ba = w_ba.reshape(d, GDN_QK_HEADS * LANES).astype(BF16)

            def per_lane(p, fill):
                pv = p.astype(F32).reshape(2, GDN_QK_HEADS, rep).transpose(1, 0, 2)
                full = jnp.full((GDN_QK_HEADS, 2, 2, rep), fill, F32).at[:, :, 1, :].set(pv)
                full = jnp.pad(full.reshape(GDN_QK_HEADS, 4 * rep), ((0, 0), (0, LANES - 4 * rep)),
                               constant_values=fill)
                return full.reshape(1, GDN_QK_HEADS * LANES)

            a_scale = per_lane(jnp.exp(gdn_a_log[j].astype(F32)), 0.0)
            dt_b = per_lane(gdn_dt_bias[j], 0.0)
            dmask = per_lane(jnp.ones((2, GDN_V_HEADS), F32), 0.0)
            stream, zqkv, zg, ba = _row_call(
                functools.partial(_gdn_pre_body, has_prev), cfg, rows_in, [g1, w_main, w_ba, a_scale, dt_b, dmask],
                mods_in, [], [((m, d), F32), ((m, ncv), BF16), ((m, v_w), BF16), ((m, GDN_QK_HEADS * LANES), F32)],
                name="gdn_pre")
            conv_w = jnp.pad(gdn_conv_w[j].astype(F32), ((0, 8 - GDN_CONV), (0, 0)))
            qkv = _gdn_conv(cfg, zqkv, conv_w)
            of, ob = _gdn_scan(cfg, qkv, ba)
            stream, h2, topi, topw = _row_call(
                _gdn_post_body, cfg, [stream, of, ob, zg],
                [gdn_w_o[j].astype(BF16), row(gdn_norm_g[j])] + route_consts, route_mods, [], route_outs,
                name="gdn_post")

        y_prev = _moe_ffn(h2, topi, topw, i, moe_w1, moe_b1[i], moe_w2, moe_b2[i])
        gate2_prev = gate2

    out = _row_call(_final_body, cfg, [stream, y_prev], [row(final_norm_g)], [gate2_prev], [],
                    [((cfg.n_lat, d), F32)], n_tiles=cfg.lat_tiles, name="final_norm")[0]
    return out.reshape(b, s, d)
```

```python
import functools
import math

import jax
import jax.numpy as jnp
from jax import lax
from jax.experimental import pallas as pl
from jax.experimental.pallas import tpu as pltpu

F32 = jnp.float32
BF16 = jnp.bfloat16
HIGHEST = lax.Precision.HIGHEST

D_MODEL = 1024
N_MOD = 6
GRID_W = 64
ROPE_BASE = 10000.0
NORM_EPS = 1e-6
LOG2_E = math.log2(math.e)

MLA_HEADS = 16
MLA_Q_RANK = 512
MLA_KV_RANK = 256
MLA_NOPE = 64
MLA_ROPE = 32
MLA_V = 64

GQA_Q_HEADS = 8
GQA_KV_HEADS = 4
GQA_HEAD_DIM = 128

MLSTM_HEADS = 4
MLSTM_QK = 128
MLSTM_V = 256

GDN_QK_HEADS = 8
GDN_V_HEADS = 16
GDN_HEAD_DIM = 128
GDN_CONV = 5
GDN_QK_PER_STEP = 2

N_EXPERTS = 32
TOP_K = 4
D_EXPERT = 1024
SWIGLU_ALPHA = 1.702
SWIGLU_LIMIT = 7.0

LANES = 128
ROW_TILE = 256
CHUNK = 64
Q_TILE = 256
GROUP_TILE = 512
RANK_TILE = 512
MOD_ROWS = 16
HALO = 16
VMEM_LIMIT = 56 * 1024 * 1024


def _params(sem):
    return pltpu.CompilerParams(dimension_semantics=sem, vmem_limit_bytes=VMEM_LIMIT)


class _Cfg:
    def __init__(self, b, s, nc):
        assert s % ROW_TILE == 0 and nc % ROW_TILE == 0 and (b * s) % nc == 0 and b + 1 <= MOD_ROWS
        self.b, self.s, self.nc = b, s, nc
        self.n_lat = b * s
        self.m = b * s + b * nc
        self.lat_tiles = self.n_lat // ROW_TILE
        self.tiles = self.m // ROW_TILE
        self.tpb = s // ROW_TILE
        self.cpb = nc // ROW_TILE


def _rms(x, g):
    return x * lax.rsqrt(jnp.mean(x * x, axis=-1, keepdims=True) + NORM_EPS) * g


def _norm_mod(x, g, scale, shift):
    return _rms(x, g) * (1.0 + scale) + shift


def _sigmoid(x):
    return 1.0 / (1.0 + jnp.exp(-x))


def _softplus(x):
    return jnp.maximum(x, 0.0) + jnp.log(1.0 + jnp.exp(-jnp.abs(x)))


def _mm(a, b):
    return jnp.dot(a.astype(BF16), b.astype(BF16), preferred_element_type=F32)


def _mm_nt(a, b):
    return lax.dot_general(a.astype(BF16), b.astype(BF16), (((1,), (1,)), ((), ())), preferred_element_type=F32)


def _ada_kernel(c_ref, w_ref, b_ref, o_ref):
    cond = c_ref[...]
    a = cond * _sigmoid(cond)
    o_ref[...] = _mm(a, w_ref[...]) + b_ref[...]


def _ada_table(cond, ada_w, ada_b):
    depth, d, nd = ada_w.shape
    return pl.pallas_call(
        _ada_kernel,
        grid=(depth, nd // d),
        in_specs=[pl.BlockSpec((MOD_ROWS, d), lambda l, n: (0, 0)),
                  pl.BlockSpec((None, d, d), lambda l, n: (l, 0, n)),
                  pl.BlockSpec((None, 1, d), lambda l, n: (l, 0, n))],
        out_specs=pl.BlockSpec((None, MOD_ROWS, d), lambda l, n: (l, 0, n)),
        out_shape=jax.ShapeDtypeStruct((depth, MOD_ROWS, nd), F32),
        compiler_params=_params(("parallel", "parallel")),
        name="ada_table",
    )(cond, ada_w, ada_b.reshape(depth, 1, nd))


def _row_call(body, cfg, rows, consts, mods, tabs, outs, n_tiles=None, name=None):
    n_tiles = cfg.tiles if n_tiles is None else n_tiles
    lat_tiles, tpb = cfg.lat_tiles, cfg.tpb
    arrays, specs = [], []
    for r in rows:
        if isinstance(r, tuple):
            arr, g = r
            arrays.append(arr)
            specs.append(pl.BlockSpec((None, ROW_TILE, arr.shape[-1]), lambda i, g=g: (g, i, 0)))
        else:
            arrays.append(r)
            specs.append(pl.BlockSpec((ROW_TILE, r.shape[-1]), lambda i: (i, 0)))
    for a in consts:
        arrays.append(a)
        specs.append(pl.BlockSpec(a.shape, lambda i, nd=a.ndim: (0,) * nd))
    for a in mods:
        arrays.append(a)
        specs.append(pl.BlockSpec((None, 1, a.shape[-1]),
                                  lambda i: (jnp.where(i < lat_tiles, 1 + i // tpb, 0), 0, 0)))
    for a in tabs:
        arrays.append(a)
        specs.append(pl.BlockSpec((None, ROW_TILE, a.shape[-1]),
                                  lambda i: (jnp.where(i < lat_tiles, 1 + i % tpb, 0), 0, 0)))
    out_shapes, out_specs = [], []
    for shape, dtype in outs:
        out_shapes.append(jax.ShapeDtypeStruct(shape, dtype))
        if shape[0] == TOP_K:
            out_specs.append(pl.BlockSpec((shape[0], ROW_TILE), lambda i: (0, i)))
        elif shape[0] == N_EXPERTS:
            out_specs.append(pl.BlockSpec((shape[0], LANES), lambda i: (0, i)))
        else:
            out_specs.append(pl.BlockSpec((ROW_TILE, shape[-1]), lambda i: (i, 0)))
    n_r, n_c, n_m, n_t = len(rows), len(consts), len(mods), len(tabs)

    def kern(*refs):
        r = refs[:n_r]
        c = refs[n_r:n_r + n_c]
        m = refs[n_r + n_c:n_r + n_c + n_m]
        t = refs[n_r + n_c + n_m:n_r + n_c + n_m + n_t]
        o = refs[n_r + n_c + n_m + n_t:]
        body(r, c, m, t, o)

    return pl.pallas_call(
        kern, grid=(n_tiles,), in_specs=specs, out_specs=out_specs, out_shape=out_shapes,
        compiler_params=_params(("parallel",)), name=name,
    )(*arrays)


def _residual_in(r, m, has_prev):
    if has_prev:
        return r[0][...] + m[0][...] * r[1][...]
    return r[0][...]


def _rope_pair(x, cos, sin_a, sin_b, shift):
    return x * cos + pltpu.roll(x, shift, 1) * sin_a + pltpu.roll(x, LANES - shift, 1) * sin_b


def _mla_pre_body(has_prev, r, c, m, t, o):
    g1, w_in, qg, kvg, wq, wk, wv, place, v_ones = c
    shift1, scale1 = m[-2], m[-1]
    cos_q, sa_q, sb_q, cos_k, sa_k, sb_k = [a[...] for a in t]
    x = _residual_in(r, m, has_prev)
    h = _norm_mod(x, g1[...], scale1[...], shift1[...])
    z = _mm(h, w_in[...])
    cq = _rms(z[:, :MLA_Q_RANK], qg[...]).astype(BF16)
    ckv = _rms(z[:, MLA_Q_RANK:MLA_Q_RANK + MLA_KV_RANK], kvg[...]).astype(BF16)
    kr = z[:, MLA_Q_RANK + MLA_KV_RANK:]
    kr = _rope_pair(kr, cos_k, sa_k, sb_k, MLA_ROPE // 2)
    scale = (MLA_NOPE + MLA_ROPE) ** -0.5 * LOG2_E
    q = _mm(cq, wq[...])
    for hh in range(MLA_HEADS):
        qs = q[:, hh * LANES:(hh + 1) * LANES]
        qs = _rope_pair(qs, cos_q, sa_q, sb_q, MLA_ROPE // 2)
        o[1][:, hh * LANES:(hh + 1) * LANES] = (qs * scale).astype(BF16)
    o[0][...] = x
    o[2][...] = (_mm(ckv, wk[...]) + _mm(kr, place[...])).astype(BF16)
    o[3][...] = (_mm(ckv, wv[...]) + v_ones[...]).astype(BF16)


def _gqa_pre_body(has_prev, r, c, m, t, o):
    g1, w_in, qg, kg = c
    shift1, scale1 = m[-2], m[-1]
    cos, sin = t[0][...], t[1][...]
    x = _residual_in(r, m, has_prev)
    h = _norm_mod(x, g1[...], scale1[...], shift1[...])
    z = _mm(h, w_in[...])
    scale = GQA_HEAD_DIM ** -0.5 * LOG2_E
    hd = GQA_HEAD_DIM
    for hh in range(GQA_Q_HEADS + GQA_KV_HEADS):
        zs = z[:, hh * hd:(hh + 1) * hd]
        is_q = hh < GQA_Q_HEADS
        zs = _rms(zs, qg[...] if is_q else kg[...])
        zs = zs * cos + pltpu.roll(zs, hd // 2, 1) * sin
        if is_q:
            o[1][:, hh * hd:(hh + 1) * hd] = (zs * scale).astype(BF16)
        else:
            kk = hh - GQA_Q_HEADS
            o[2][:, kk * hd:(kk + 1) * hd] = zs.astype(BF16)
    o[0][...] = x
    o[3][...] = z[:, (GQA_Q_HEADS + GQA_KV_HEADS) * hd:].astype(BF16)


def _mlstm_pre_body(has_prev, r, c, m, t, o):
    g1, w_in, gate_b, fmask = c
    shift1, scale1 = m[-2], m[-1]
    x = _residual_in(r, m, has_prev)
    h = _norm_mod(x, g1[...], scale1[...], shift1[...])
    z = _mm(h, w_in[...])
    nqk = MLSTM_HEADS * MLSTM_QK
    nv = MLSTM_HEADS * MLSTM_V
    o[0][...] = x
    o[1][...] = z[:, :nqk].astype(BF16)
    o[2][...] = (z[:, nqk:2 * nqk] * MLSTM_QK ** -0.5).astype(BF16)
    o[3][...] = z[:, 2 * nqk:2 * nqk + nv].astype(BF16)
    o[4][...] = z[:, 2 * nqk + nv:2 * nqk + 2 * nv].astype(BF16)
    gt = z[:, 2 * nqk + 2 * nv:] + gate_b[...]
    log_sig = jnp.minimum(gt, 0.0) - jnp.log(1.0 + jnp.exp(-jnp.abs(gt)))
    o[5][...] = jnp.where(fmask[...] > 0.0, log_sig, gt)


def _gdn_pre_body(has_prev, r, c, m, t, o):
    g1, w_in, w_ba, a_scale, dt_b, dmask = c
    shift1, scale1 = m[-2], m[-1]
    x = _residual_in(r, m, has_prev)
    h = _norm_mod(x, g1[...], scale1[...], shift1[...]).astype(BF16)
    ncv = 2 * GDN_QK_HEADS * GDN_HEAD_DIM + GDN_V_HEADS * GDN_HEAD_DIM
    z = _mm(h, w_in[...])
    o[0][...] = x
    o[1][...] = z[:, :ncv].astype(BF16)
    o[2][...] = z[:, ncv:].astype(BF16)
    ba = _mm(h, w_ba[...])
    o[3][...] = jnp.where(dmask[...] > 0.0, -a_scale[...] * _softplus(ba + dt_b[...]), _sigmoid(ba))


def _route_tail(x, y, c, m, o):
    g2, rw_t, rb = c
    gate1, shift2, scale2 = m
    x1 = x + gate1[...] * y
    h2 = _norm_mod(x1, g2[...], scale2[...], shift2[...])
    o[0][...] = x1
    o[1][...] = h2.astype(BF16)
    logits = lax.dot_general(rw_t[...], h2, (((1,), (1,)), ((), ())), preferred_element_type=F32,
                             precision=HIGHEST) + rb[...]
    e_iota = lax.broadcasted_iota(jnp.int32, logits.shape, 0)
    vals = []
    picked = jnp.zeros(logits.shape, F32)
    for k in range(TOP_K):
        mx = jnp.max(logits, axis=0, keepdims=True)
        idx = jnp.min(jnp.where(logits == mx, e_iota, N_EXPERTS), axis=0, keepdims=True)
        o[2][k:k + 1, :] = idx
        vals.append(mx)
        hit = e_iota == idx
        picked = picked + jnp.where(hit, 1.0, 0.0)
        logits = jnp.where(hit, -jnp.inf, logits)
    o[4][...] = jnp.broadcast_to(jnp.sum(picked, axis=1, keepdims=True), o[4].shape)
    es = [jnp.exp(v - vals[0]) for v in vals]
    tot = es[0] + es[1] + es[2] + es[3]
    for k in range(TOP_K):
        o[3][k:k + 1, :] = es[k] / tot


def _attn_post_body(r, c, m, t, o):
    x, a = r
    y = _mm(a[...], c[0][...])
    _route_tail(x[...], y, c[1:], m, o)


def _mlstm_post_body(r, c, m, t, o):
    x, hf, hb, og = r
    w_o, ng = c[0], c[1]
    hs = hf[...] + hb[...]
    gate = _sigmoid(og[...].astype(F32))
    parts = []
    for hh in range(MLSTM_HEADS):
        sl = slice(hh * MLSTM_V, (hh + 1) * MLSTM_V)
        parts.append((_rms(hs[:, sl], ng[...]) * gate[:, sl]).astype(BF16))
    y = _mm(jnp.concatenate(parts, axis=1), w_o[...])
    _route_tail(x[...], y, c[2:], m, o)


def _gdn_post_body(r, c, m, t, o):
    x, of, ob, zg = r
    w_o, ng = c[0], c[1]
    os_ = of[...] + ob[...]
    g = zg[...].astype(F32)
    gate = g * _sigmoid(g)
    parts = []
    for hh in range(GDN_V_HEADS):
        sl = slice(hh * GDN_HEAD_DIM, (hh + 1) * GDN_HEAD_DIM)
        parts.append((_rms(os_[:, sl], ng[...]) * gate[:, sl]).astype(BF16))
    y = _mm(jnp.concatenate(parts, axis=1), w_o[...])
    _route_tail(x[...], y, c[2:], m, o)


def _final_body(r, c, m, t, o):
    x = r[0][...] + m[0][...] * r[1][...]
    o[0][...] = _rms(x, c[0][...])


def _attn_kernel(*refs, n_seg, k_stride, v_width):
    q_ref = refs[0]
    segs = [(refs[1 + 2 * i], refs[2 + 2 * i]) for i in range(n_seg)]
    o_ref = refs[1 + 2 * n_seg]
    for j in range(2):
        q = q_ref[:, j * LANES:(j + 1) * LANES]
        ss = [_mm_nt(q, k_ref[:, j * k_stride:j * k_stride + LANES]) for k_ref, _ in segs]
        mx = ss[0].max(axis=-1, keepdims=True)
        for s in ss[1:]:
            mx = jnp.maximum(mx, s.max(axis=-1, keepdims=True))
        acc, den = 0.0, 0.0
        for s, (_, v_ref) in zip(ss, segs):
            p = jnp.exp2(s - mx)
            if k_stride:
                acc = acc + _mm(p, v_ref[:, j * 2 * v_width:(j + 1) * 2 * v_width])
            else:
                den = den + p.sum(axis=-1, keepdims=True)
                acc = acc + _mm(p, v_ref[...])
        if k_stride:
            den = acc[:, v_width:v_width + 1]
            acc = acc[:, :v_width]
        o_ref[:, j * v_width:(j + 1) * v_width] = (acc / den).astype(o_ref.dtype)


def _attention(cfg, q, k, v, n_groups, k_stride, v_width, latent):
    kw = 2 * LANES if k_stride else LANES
    vw = 4 * v_width if k_stride else v_width
    ow = 2 * v_width
    ctx_blk0 = cfg.n_lat // cfg.nc
    k_ctx = pl.BlockSpec((cfg.nc, kw), lambda b, g, i: (ctx_blk0 + b, g))
    v_ctx = pl.BlockSpec((cfg.nc, vw), lambda b, g, i: (ctx_blk0 + b, g))
    if latent:
        qpb = cfg.s // Q_TILE
        grid = (cfg.b, n_groups, qpb)
        q_spec = pl.BlockSpec((Q_TILE, 2 * LANES), lambda b, g, i: (b * qpb + i, g))
        o_spec = pl.BlockSpec((Q_TILE, ow), lambda b, g, i: (b * qpb + i, g))
        in_specs = [q_spec, pl.BlockSpec((cfg.s, kw), lambda b, g, i: (b, g)),
                    pl.BlockSpec((cfg.s, vw), lambda b, g, i: (b, g)), k_ctx, v_ctx]
        args = (q, k, v, k, v)
        n_seg = 2
        rows = cfg.n_lat
    else:
        qpb = cfg.nc // Q_TILE
        q0 = cfg.n_lat // Q_TILE
        grid = (cfg.b, n_groups, qpb)
        q_spec = pl.BlockSpec((Q_TILE, 2 * LANES), lambda b, g, i: (q0 + b * qpb + i, g))
        o_spec = pl.BlockSpec((Q_TILE, ow), lambda b, g, i: (b * qpb + i, g))
        in_specs = [q_spec, k_ctx, v_ctx]
        args = (q, k, v)
        n_seg = 1
        rows = cfg.b * cfg.nc
    return pl.pallas_call(
        functools.partial(_attn_kernel, n_seg=n_seg, k_stride=k_stride, v_width=v_width),
        grid=grid, in_specs=in_specs, out_specs=o_spec,
        out_shape=jax.ShapeDtypeStruct((rows, n_groups * ow), BF16),
        compiler_params=_params(("parallel", "parallel", "arbitrary")),
        name="attention_lat" if latent else "attention_ctx",
    )(*args)


def _block_masks(flip):
    li = lax.broadcasted_iota(jnp.int32, (ROW_TILE, ROW_TILE), 0)
    si = lax.broadcasted_iota(jnp.int32, (ROW_TILE, ROW_TILE), 1)
    same = (li // CHUNK) == (si // CHUNK)
    eye = li == si
    incl = same & ((si >= li) if flip else (si <= li))
    strict = same & ((si > li) if flip else (si < li))
    return eye, same, incl, strict


def _scan_block_index(cfg, flip):
    cpb, tpb, lat_tiles = cfg.cpb, cfg.tpb, cfg.lat_tiles

    def row_block(b, j):
        if flip:
            ctx = lat_tiles + b * cpb + (cpb - 1 - j)
            lat = b * tpb + (tpb - 1 - (j - cpb))
        else:
            ctx = lat_tiles + b * cpb + j
            lat = b * tpb + (j - cpb)
        return jnp.where(j < cpb, ctx, lat)

    return row_block


def _mlstm_kernel(q_ref, k_ref, v_ref, g_ref, o_ref, c_st, n_st, m_st, *, flip, d):
    @pl.when(pl.program_id(1) == 0)
    def _():
        c_st[...] = jnp.zeros_like(c_st)
        n_st[...] = jnp.zeros_like(n_st)
        m_st[...] = jnp.zeros_like(m_st)

    eye, same, incl, _ = _block_masks(flip)
    n_chunks = ROW_TILE // CHUNK
    order = [n_chunks - 1 - ci if flip else ci for ci in range(n_chunks)]
    to_row = lambda col: jnp.sum(jnp.where(eye, col, 0.0), axis=0, keepdims=True)
    per_chunk_col = lambda vals: jnp.concatenate(
        [jnp.broadcast_to(vals[c], (CHUNK, 1)) for c in range(n_chunks)], axis=0)
    prep = []
    for hh in range(MLSTM_HEADS):
        q = q_ref[:, hh * MLSTM_QK:(hh + 1) * MLSTM_QK]
        k = k_ref[:, hh * MLSTM_QK:(hh + 1) * MLSTM_QK]
        v = v_ref[:, hh * MLSTM_V:(hh + 1) * MLSTM_V]
        icol = d * 2 * MLSTM_HEADS + hh
        i_col = g_ref[:, icol:icol + 1]
        f_col = g_ref[:, icol + MLSTM_HEADS:icol + MLSTM_HEADS + 1]
        f_row = to_row(f_col)
        i_row = to_row(i_col)
        bcum_col = jnp.sum(jnp.where(incl, f_row, 0.0), axis=1, keepdims=True)
        bcum_row = to_row(bcum_col)
        bend_col = jnp.sum(jnp.where(same, f_row, 0.0), axis=1, keepdims=True)
        w_log = bend_col - bcum_col + i_col
        wmax_col = jnp.max(jnp.where(same, to_row(w_log), -jnp.inf), axis=1, keepdims=True)
        m_prev, m_new = [None] * n_chunks, [None] * n_chunks
        m_run = m_st[hh]
        for cc in order:
            r0 = cc * CHUNK
            m_prev[cc] = m_run
            m_run = jnp.maximum(bend_col[r0:r0 + 1] + m_run, wmax_col[r0:r0 + 1])
            m_new[cc] = m_run
        carries = [jnp.exp(bend_col[cc * CHUNK:cc * CHUNK + 1] + m_prev[cc] - m_new[cc]) for cc in range(n_chunks)]
        kw = k.astype(F32) * jnp.exp(w_log - per_chunk_col(m_new))
        g_log = bcum_col + per_chunk_col(m_prev)
        d_log = jnp.where(incl, bcum_col - bcum_row + i_row, -jnp.inf)
        m_t = jnp.maximum(g_log, jnp.max(d_log, axis=1, keepdims=True))
        inter = jnp.exp(g_log - m_t)
        s = _mm_nt(q, k) * jnp.exp(d_log - m_t)
        sv = _mm(s, v)
        ssum = jnp.sum(s, axis=1, keepdims=True)
        floor = jnp.exp(-m_t)
        kvs = [_mm(kw[cc * CHUNK:(cc + 1) * CHUNK].T, v[cc * CHUNK:(cc + 1) * CHUNK]) for cc in range(n_chunks)]
        ksum = [jnp.sum(kw[cc * CHUNK:(cc + 1) * CHUNK], axis=0, keepdims=True) for cc in range(n_chunks)]
        prep.append((q, inter, sv, ssum, floor, kvs, ksum, carries, m_run))
    for cc in order:
        rows = slice(cc * CHUNK, (cc + 1) * CHUNK)
        for hh in range(MLSTM_HEADS):
            q, inter, sv, ssum, floor, kvs, ksum, carries, _ = prep[hh]
            c_prev, n_prev = c_st[hh], n_st[hh]
            qc = q[rows]
            num = inter[rows] * _mm(qc, c_prev) + sv[rows]
            den = inter[rows] * jnp.sum(qc.astype(F32) * n_prev, axis=1, keepdims=True) + ssum[rows]
            o_ref[rows, hh * MLSTM_V:(hh + 1) * MLSTM_V] = num / jnp.maximum(jnp.abs(den), floor[rows])
            c_st[hh] = carries[cc] * c_prev + kvs[cc]
            n_st[hh] = carries[cc] * n_prev + ksum[cc]
    for hh in range(MLSTM_HEADS):
        m_st[hh] = prep[hh][-1]


def _mlstm_scan(cfg, q, k, v, gates):
    nqk, nv = MLSTM_HEADS * MLSTM_QK, MLSTM_HEADS * MLSTM_V
    outs = []
    for d in range(2):
        flip = d == 1
        rb = _scan_block_index(cfg, flip)
        row = lambda w, rb=rb: pl.BlockSpec((ROW_TILE, w), lambda b, j: (rb(b, j), 0))
        outs.append(pl.pallas_call(
            functools.partial(_mlstm_kernel, flip=flip, d=d),
            grid=(cfg.b, cfg.cpb + cfg.tpb),
            in_specs=[row(nqk), row(nqk), row(nv), row(LANES)],
            out_specs=row(nv),
            out_shape=jax.ShapeDtypeStruct((cfg.m, nv), F32),
            scratch_shapes=[pltpu.VMEM((MLSTM_HEADS, MLSTM_QK, MLSTM_V), F32),
                            pltpu.VMEM((MLSTM_HEADS, 1, MLSTM_QK), F32),
                            pltpu.VMEM((MLSTM_HEADS, 1, 1), F32)],
            compiler_params=_params(("parallel", "arbitrary")),
            name="mlstm_bwd" if flip else "mlstm_fwd",
        )(q, k, v, gates))
    return outs


def _gdn_kernel(q_ref, k_ref, v_ref, ba_ref, o_ref, s_st, *, flip, d):
    @pl.when(pl.program_id(2) == 0)
    def _():
        s_st[...] = jnp.zeros_like(s_st)

    eye, same, incl, strict = _block_masks(flip)
    eye_f = jnp.where(eye, 1.0, 0.0)
    rep = GDN_V_HEADS // GDN_QK_HEADS
    hd = GDN_HEAD_DIM
    n_chunks = ROW_TILE // CHUNK
    heads = [(qi, jj) for qi in range(GDN_QK_PER_STEP) for jj in range(rep)]
    ks = [k_ref[:, qi * hd:(qi + 1) * hd] for qi in range(GDN_QK_PER_STEP)]
    kfs = [k.astype(F32) for k in ks]
    qfs = [q_ref[:, qi * hd:(qi + 1) * hd].astype(F32) for qi in range(GDN_QK_PER_STEP)]
    qks = [_mm_nt(q_ref[:, qi * hd:(qi + 1) * hd], ks[qi]) for qi in range(GDN_QK_PER_STEP)]
    betas, gcs, gces, decays, kbs = [], [], [], [], []
    for qi, jj in heads:
        bcol = qi * LANES + d * 2 * rep + jj
        beta = ba_ref[:, bcol:bcol + 1]
        ld = ba_ref[:, bcol + rep:bcol + rep + 1]
        ld_row = jnp.sum(jnp.where(eye, ld, 0.0), axis=0, keepdims=True)
        gc_col = jnp.sum(jnp.where(incl, ld_row, 0.0), axis=1, keepdims=True)
        gc_row = jnp.sum(jnp.where(eye, gc_col, 0.0), axis=0, keepdims=True)
        gces.append(jnp.sum(jnp.where(same, ld_row, 0.0), axis=1, keepdims=True))
        decays.append(jnp.exp(jnp.where(incl, gc_col - gc_row, -jnp.inf)))
        betas.append(beta)
        gcs.append(gc_col)
        kbs.append(kfs[qi] * beta)
    pws = [jnp.where(strict, _mm_nt(kbs[i], ks[qi]) * decays[i], 0.0) for i, (qi, _) in enumerate(heads)]
    invs = [eye_f - p for p in pws]
    for _ in range(int(math.log2(CHUNK)) - 1):
        pws = [_mm(p, p) for p in pws]
        invs = [iv + _mm(iv, p) for iv, p in zip(invs, pws)]
    egs = [jnp.exp(g) for g in gcs]
    uws = []
    for i, (qi, jj) in enumerate(heads):
        v = v_ref[:, (qi * rep + jj) * hd:(qi * rep + jj + 1) * hd].astype(F32)
        uws.append(_mm(invs[i], jnp.concatenate([v * betas[i], kbs[i] * egs[i]], axis=1)))
    aws = [_mm(qks[qi] * decays[i], uws[i]) for i, (qi, _) in enumerate(heads)]
    qes = [qfs[qi] * egs[i] - aws[i][:, hd:] for i, (qi, _) in enumerate(heads)]
    kdecs = [kfs[qi] * jnp.exp(gces[i] - gcs[i]) for i, (qi, _) in enumerate(heads)]
    for ci in range(n_chunks):
        cc = n_chunks - 1 - ci if flip else ci
        rows = slice(cc * CHUNK, (cc + 1) * CHUNK)
        bgs = [_mm(kdecs[i][rows].T, uws[i][rows]) for i in range(len(heads))]
        for i, (qi, jj) in enumerate(heads):
            s_prev = s_st[i]
            col = (qi * rep + jj) * hd
            o_ref[rows, col:col + hd] = _mm(qes[i][rows], s_prev) + aws[i][rows, :hd]
            g_end = jnp.exp(gces[i][cc * CHUNK:cc * CHUNK + 1])
            s_st[i] = g_end * s_prev - _mm(bgs[i][:, hd:], s_prev) + bgs[i][:, :hd]


def _gdn_scan(cfg, qkv, ba):
    rep = GDN_V_HEADS // GDN_QK_HEADS
    hd = GDN_HEAD_DIM
    g = GDN_QK_PER_STEP
    outs = []
    for d in range(2):
        flip = d == 1
        rb = _scan_block_index(cfg, flip)
        col = lambda w, off, rb=rb: pl.BlockSpec((ROW_TILE, w), lambda b, h, j: (rb(b, j), off + h))
        outs.append(pl.pallas_call(
            functools.partial(_gdn_kernel, flip=flip, d=d),
            grid=(cfg.b, GDN_QK_HEADS // g, cfg.cpb + cfg.tpb),
            in_specs=[col(g * hd, 0), col(g * hd, GDN_QK_HEADS // g),
                      col(g * rep * hd, 2 * GDN_QK_HEADS // (g * rep)), col(g * LANES, 0)],
            out_specs=col(g * rep * hd, 0),
            out_shape=jax.ShapeDtypeStruct((cfg.m, GDN_V_HEADS * hd), F32),
            scratch_shapes=[pltpu.VMEM((g * rep, hd, hd), F32)],
            compiler_params=_params(("parallel", "parallel", "arbitrary")),
            name="gdn_bwd" if flip else "gdn_fwd",
        )(qkv, qkv, qkv, ba))
    return outs


def _gdn_conv_kernel(cur_ref, prev_ref, next_ref, w_ref, o_ref, pad_ref, *, lat_tiles, tpb, cpb):
    i = pl.program_id(0)
    cb = pl.program_id(1)
    lat = i < lat_tiles
    pos = jnp.where(lat, i % tpb, (i - lat_tiles) % cpb)
    per = jnp.where(lat, tpb, cpb)
    first = pos == 0
    last = pos == per - 1
    prev = prev_ref[...].astype(F32)
    nxt = next_ref[...].astype(F32)
    pad_ref[0:HALO, :] = jnp.where(first, 0.0, prev)
    pad_ref[HALO:HALO + ROW_TILE, :] = cur_ref[...].astype(F32)
    pad_ref[HALO + ROW_TILE:, :] = jnp.where(last, 0.0, nxt)
    half = GDN_CONV // 2
    acc = 0.0
    for tap in range(GDN_CONV):
        off = HALO - half + tap
        acc = acc + pad_ref[off:off + ROW_TILE, :] * w_ref[tap:tap + 1, :]
    act = acc * _sigmoid(acc)
    qscale = jnp.where(cb == 0, GDN_HEAD_DIM ** -0.5, 1.0)
    hd = GDN_HEAD_DIM
    for hh in range(act.shape[1] // hd):
        a = act[:, hh * hd:(hh + 1) * hd]
        nrm = a * lax.rsqrt(jnp.sum(a * a, axis=-1, keepdims=True) + NORM_EPS) * qscale
        o_ref[:, hh * hd:(hh + 1) * hd] = jnp.where(cb < 2, nrm, a).astype(o_ref.dtype)


def _gdn_conv(cfg, zqkv, conv_w):
    cw = GDN_QK_HEADS * GDN_HEAD_DIM
    n_cb = zqkv.shape[1] // cw
    hpt = ROW_TILE // HALO
    n_halo = cfg.m // HALO
    return pl.pallas_call(
        functools.partial(_gdn_conv_kernel, lat_tiles=cfg.lat_tiles, tpb=cfg.tpb, cpb=cfg.cpb),
        grid=(cfg.tiles, n_cb),
        in_specs=[pl.BlockSpec((ROW_TILE, cw), lambda i, c: (i, c)),
                  pl.BlockSpec((HALO, cw), lambda i, c: (jnp.maximum(i * hpt - 1, 0), c)),
                  pl.BlockSpec((HALO, cw), lambda i, c: (jnp.minimum((i + 1) * hpt, n_halo - 1), c)),
                  pl.BlockSpec((8, cw), lambda i, c: (0, c))],
        out_specs=pl.BlockSpec((ROW_TILE, cw), lambda i, c: (i, c)),
        out_shape=jax.ShapeDtypeStruct(zqkv.shape, BF16),
        scratch_shapes=[pltpu.VMEM((ROW_TILE + 2 * HALO, cw), F32)],
        compiler_params=_params(("parallel", "parallel")),
        name="gdn_conv",
    )(zqkv, zqkv, zqkv, conv_w)


def _rank_kernel(e_ref, start_ref, slot_ref, run_ref):
    @pl.when(pl.program_id(0) == 0)
    def _():
        run_ref[...] = start_ref[...]

    e = e_ref[...]
    onehot = (lax.broadcasted_iota(jnp.int32, (N_EXPERTS, RANK_TILE), 0) == e)
    oh = jnp.where(onehot, 1.0, 0.0)
    ji = lax.broadcasted_iota(jnp.int32, (RANK_TILE, RANK_TILE), 0)
    si = lax.broadcasted_iota(jnp.int32, (RANK_TILE, RANK_TILE), 1)
    upper = jnp.where(ji <= si, 1.0, 0.0).astype(BF16)
    cum = jnp.dot(oh.astype(BF16), upper, preferred_element_type=F32)
    run = run_ref[:, 0:1]
    slot = jnp.sum(oh * (cum - 1.0 + run), axis=0, keepdims=True)
    slot_ref[...] = slot.astype(jnp.int32)
    run_ref[...] = run_ref[...] + jnp.sum(oh, axis=1, keepdims=True)


def _slot_pairs(expert_of_pair, group_start):
    n_pairs = expert_of_pair.shape[1]
    return pl.pallas_call(
        _rank_kernel,
        grid=(n_pairs // RANK_TILE,),
        in_specs=[pl.BlockSpec((1, RANK_TILE), lambda i: (0, i)),
                  pl.BlockSpec((N_EXPERTS, LANES), lambda i: (0, 0))],
        out_specs=pl.BlockSpec((1, RANK_TILE), lambda i: (0, i)),
        out_shape=jax.ShapeDtypeStruct((1, n_pairs), jnp.int32),
        scratch_shapes=[pltpu.VMEM((N_EXPERTS, LANES), F32)],
        compiler_params=_params(("arbitrary",)),
        name="moe_rank",
    )(expert_of_pair, group_start)


def _moe_kernel(te_ref, nu_ref, x_ref, w1_ref, b1g_ref, b1l_ref, w2_ref, b2_ref, o_ref, w1g_s, w1l_s, w2_s):
    j = pl.program_id(0)
    used = j < nu_ref[0]
    new_expert = jnp.logical_or(j == 0, te_ref[j] != te_ref[jnp.maximum(j - 1, 0)])

    @pl.when(jnp.logical_and(used, new_expert))
    def _():
        slab = 2 * LANES
        src = lax.broadcasted_iota(jnp.int32, (slab, slab), 0)
        dst = lax.broadcasted_iota(jnp.int32, (slab, slab), 1)
        want = jnp.where(dst < LANES, 2 * dst, 2 * (dst - LANES) + 1)
        perm = jnp.where(src == want, 1.0, 0.0).astype(BF16)
        for sb in range(w1_ref.shape[1] // slab):
            sorted_cols = jnp.dot(w1_ref[:, sb * slab:(sb + 1) * slab].astype(BF16), perm,
                                  preferred_element_type=F32)
            w1g_s[:, sb * LANES:(sb + 1) * LANES] = sorted_cols[:, :LANES].astype(BF16)
            w1l_s[:, sb * LANES:(sb + 1) * LANES] = sorted_cols[:, LANES:].astype(BF16)
        w2_s[...] = w2_ref[...].astype(BF16)

    @pl.when(used)
    def _():
        x = x_ref[...]
        glu = jnp.minimum(jnp.dot(x, w1g_s[...], preferred_element_type=F32) + b1g_ref[...], SWIGLU_LIMIT)
        lin = jnp.clip(jnp.dot(x, w1l_s[...], preferred_element_type=F32) + b1l_ref[...],
                       -SWIGLU_LIMIT, SWIGLU_LIMIT)
        act = glu * _sigmoid(SWIGLU_ALPHA * glu) * (lin + 1.0)
        y = jnp.dot(act.astype(BF16), w2_s[...], preferred_element_type=F32) + b2_ref[...]
        o_ref[...] = y.astype(o_ref.dtype)

    @pl.when(jnp.logical_not(used))
    def _():
        o_ref[...] = jnp.zeros_like(o_ref)


def _moe_grouped(xs, tile_expert, n_used, layer, w1_all, b1g, b1l, w2_all, b2):
    n_slots, d = xs.shape
    de = w2_all.shape[2]
    wspec = lambda r, c: pl.BlockSpec((None, r, c), lambda j, te, nu: (te[j], 0, 0))
    lspec = lambda r, c: pl.BlockSpec((None, None, r, c), lambda j, te, nu: (layer, te[j], 0, 0))
    return pl.pallas_call(
        _moe_kernel,
        grid_spec=pltpu.PrefetchScalarGridSpec(
            num_scalar_prefetch=2,
            grid=(n_slots // GROUP_TILE,),
            in_specs=[pl.BlockSpec((GROUP_TILE, d), lambda j, te, nu: (j, 0)),
                      lspec(d, 2 * de), wspec(1, de), wspec(1, de), lspec(de, d), wspec(1, d)],
            out_specs=pl.BlockSpec((GROUP_TILE, d), lambda j, te, nu: (j, 0)),
            scratch_shapes=[pltpu.VMEM((d, de), BF16), pltpu.VMEM((d, de), BF16), pltpu.VMEM((de, d), BF16)]),
        out_shape=jax.ShapeDtypeStruct((n_slots, d), BF16),
        compiler_params=_params(("arbitrary",)),
        name="moe_grouped",
    )(tile_expert, n_used, xs, w1_all, b1g, b1l, w2_all, b2)


def _moe_ffn(h2, topi, topw, tile_counts, layer, w1_all, b1, w2_all, b2):
    n_tok = h2.shape[0]
    n_pairs = TOP_K * n_tok
    assert n_pairs % RANK_TILE == 0
    n_slots = -(-(n_pairs + N_EXPERTS * (GROUP_TILE - 1)) // GROUP_TILE) * GROUP_TILE
    n_tiles = n_slots // GROUP_TILE

    counts = jnp.sum(tile_counts.reshape(N_EXPERTS, -1, LANES)[:, :, 0], axis=1).astype(jnp.int32)
    padded = (counts + GROUP_TILE - 1) // GROUP_TILE * GROUP_TILE
    ends = jnp.cumsum(padded)
    starts = ends - padded
    group_start = jnp.broadcast_to(starts.astype(F32)[:, None], (N_EXPERTS, LANES))
    slot = _slot_pairs(topi.reshape(1, n_pairs), group_start)[0]
    token = jnp.tile(jnp.arange(n_tok, dtype=jnp.int32), TOP_K)
    src = (jnp.arange(n_slots, dtype=jnp.int32) % n_tok).at[slot].set(token)
    tile_start = jnp.arange(n_tiles, dtype=jnp.int32) * GROUP_TILE
    tile_expert = jnp.sum((ends[None, :] <= tile_start[:, None]).astype(jnp.int32), axis=1)
    tile_expert = jnp.minimum(tile_expert, N_EXPERTS - 1)
    n_used = (ends[-1] // GROUP_TILE).astype(jnp.int32).reshape(1)

    xs = jnp.take(h2, src, axis=0)
    b1g = b1[:, None, 0::2]
    b1l = b1[:, None, 1::2]
    ys = _moe_grouped(xs, tile_expert, n_used, layer, w1_all, b1g, b1l, w2_all, b2[:, None, :])

    pos = slot.reshape(TOP_K, n_tok)
    y = jnp.zeros((n_tok, h2.shape[1]), F32)
    for k in range(TOP_K):
        y = y + topw[k][:, None] * jnp.take(ys, pos[k], axis=0).astype(F32)
    return y


def _axial_angles(n_tokens, rot_dim):
    t = jnp.arange(n_tokens, dtype=jnp.int32)
    rows = (t // GRID_W).astype(F32)
    cols = (t % GRID_W).astype(F32)
    d_axis = rot_dim // 2
    inv_freq = ROPE_BASE ** (-jnp.arange(0, d_axis, 2, dtype=F32) / d_axis)
    ang = jnp.concatenate([rows[:, None] * inv_freq, cols[:, None] * inv_freq], axis=-1)
    return jnp.cos(ang), jnp.sin(ang)


def _seq_table(cfg, lat_rows, ctx_value):
    c = lat_rows.shape[-1]
    ctx = jnp.full((1, ROW_TILE, c), ctx_value, F32)
    return jnp.concatenate([ctx, lat_rows.reshape(cfg.tpb, ROW_TILE, c)], axis=0)


def _slab_rope_tables(cfg, rot_dim, lane0):
    cos, sin = _axial_angles(cfg.s, rot_dim)
    half = rot_dim // 2
    s = cfg.s
    ones = lambda n: jnp.ones((s, n), F32)
    zeros = lambda n: jnp.zeros((s, n), F32)
    tail = LANES - lane0 - rot_dim
    cos_t = jnp.concatenate([ones(lane0), cos, cos, ones(tail)], axis=1)
    sin_a = jnp.concatenate([zeros(lane0 + half), sin, zeros(tail)], axis=1)
    sin_b = jnp.concatenate([zeros(lane0), -sin, zeros(half + tail)], axis=1)
    return _seq_table(cfg, cos_t, 1.0), _seq_table(cfg, sin_a, 0.0), _seq_table(cfg, sin_b, 0.0)


def _mod_slices(table, n_rows):
    d = D_MODEL
    return [table[:, k * d:(k + 1) * d].reshape(n_rows, 1, d) for k in range(N_MOD)]


def kernel(x, c, ctx, c_ctx, ada_w, ada_b, norm1_g, norm2_g, router_w, router_b, moe_w1, moe_b1, moe_w2, moe_b2,
           mla_w_in, mla_q_norm_g, mla_kv_norm_g, mla_w_uq, mla_w_ukv, mla_w_o, gqa_w_in, gqa_q_norm_g,
           gqa_k_norm_g, gqa_w_o, mlstm_w_in, mlstm_gate_b, mlstm_norm_g, mlstm_w_o, gdn_w_in, gdn_conv_w,
           gdn_a_log, gdn_dt_bias, gdn_norm_g, gdn_w_o, final_norm_g):
    b, s, d = x.shape
    nc = ctx.shape[1]
    depth = ada_w.shape[0]
    cfg = _Cfg(b, s, nc)
    m = cfg.m
    row = lambda a: a.reshape(1, -1).astype(F32)

    cond = jnp.concatenate([c_ctx[None, :], c, jnp.zeros((MOD_ROWS - 1 - b, d), F32)], axis=0)
    mod_all = _ada_table(cond, ada_w, ada_b)

    stream = jnp.concatenate([x.reshape(b * s, d), ctx.reshape(b * nc, d)], axis=0)
    y_prev = None
    gate2_prev = None
    for i in range(depth):
        kind, j = i % 4, i // 4
        shift1, scale1, gate1, shift2, scale2, gate2 = _mod_slices(mod_all[i], MOD_ROWS)
        has_prev = y_prev is not None
        rows_in = [stream, y_prev] if has_prev else [stream]
        mods_in = ([gate2_prev] if has_prev else []) + [shift1, scale1]
        g1 = row(norm1_g[i])
        route_consts = [row(norm2_g[i]), router_w[i].T, router_b[i].reshape(N_EXPERTS, 1)]
        route_mods = [gate1, shift2, scale2]
        route_outs = [((m, d), F32), ((m, d), BF16), ((TOP_K, m), jnp.int32), ((TOP_K, m), F32),
                      ((N_EXPERTS, cfg.tiles * LANES), F32)]

        if kind == 0:
            qk = MLA_NOPE + MLA_ROPE
            w_in = jnp.pad(mla_w_in[j], ((0, 0), (0, LANES - MLA_ROPE))).astype(BF16)
            wq = jnp.pad(mla_w_uq[j].reshape(MLA_Q_RANK, MLA_HEADS, qk), ((0, 0), (0, 0), (0, LANES - qk)))
            wq = wq.reshape(MLA_Q_RANK, MLA_HEADS * LANES).astype(BF16)
            wkv = mla_w_ukv[j].reshape(MLA_KV_RANK, MLA_HEADS, MLA_NOPE + MLA_V)
            wk = jnp.pad(wkv[:, :, :MLA_NOPE], ((0, 0), (0, 0), (0, LANES - MLA_NOPE)))
            wk = wk.reshape(MLA_KV_RANK, MLA_HEADS * LANES).astype(BF16)
            wv = jnp.pad(wkv[:, :, MLA_NOPE:], ((0, 0), (0, 0), (0, MLA_V)))
            wv = wv.reshape(MLA_KV_RANK, MLA_HEADS * 2 * MLA_V).astype(BF16)
            v_ones = jnp.tile(jnp.concatenate([jnp.zeros((MLA_V,), F32), jnp.ones((MLA_V,), F32)]), MLA_HEADS)
            v_ones = v_ones.reshape(1, MLA_HEADS * 2 * MLA_V)
            place = jnp.zeros((LANES, LANES), F32).at[jnp.arange(MLA_ROPE), MLA_NOPE + jnp.arange(MLA_ROPE)].set(1.0)
            place = jnp.tile(place, (1, MLA_HEADS)).astype(BF16)
            tabs = list(_slab_rope_tables(cfg, MLA_ROPE, MLA_NOPE)) + list(_slab_rope_tables(cfg, MLA_ROPE, 0))
            stream, q, k, v = _row_call(
                functools.partial(_mla_pre_body, has_prev), cfg, rows_in,
                [g1, w_in, row(mla_q_norm_g[j]), row(mla_kv_norm_g[j]), wq, wk, wv, place, v_ones], mods_in, tabs,
                [((m, d), F32), ((m, MLA_HEADS * LANES), BF16), ((m, MLA_HEADS * LANES), BF16),
                 ((m, MLA_HEADS * 2 * MLA_V), BF16)], name="mla_pre")
            a_lat = _attention(cfg, q, k, v, MLA_HEADS // 2, LANES, MLA_V, True)
            a_ctx = _attention(cfg, q, k, v, MLA_HEADS // 2, LANES, MLA_V, False)
            a = jnp.concatenate([a_lat, a_ctx], axis=0)
            stream, h2, topi, topw, tile_counts = _row_call(
                _attn_post_body, cfg, [stream, a], [mla_w_o[j].astype(BF16)] + route_consts, route_mods, [],
                route_outs, name="mla_post")
        elif kind == 1:
            cos, sin = _axial_angles(cfg.s, GQA_HEAD_DIM)
            tabs = [_seq_table(cfg, jnp.concatenate([cos, cos], axis=1), 1.0),
                    _seq_table(cfg, jnp.concatenate([-sin, sin], axis=1), 0.0)]
            stream, q, k, v = _row_call(
                functools.partial(_gqa_pre_body, has_prev), cfg, rows_in,
                [g1, gqa_w_in[j].astype(BF16), row(gqa_q_norm_g[j]), row(gqa_k_norm_g[j])], mods_in, tabs,
                [((m, d), F32), ((m, GQA_Q_HEADS * GQA_HEAD_DIM), BF16), ((m, GQA_KV_HEADS * GQA_HEAD_DIM), BF16),
                 ((m, GQA_KV_HEADS * GQA_HEAD_DIM), BF16)], name="gqa_pre")
            a_lat = _attention(cfg, q, k, v, GQA_KV_HEADS, 0, GQA_HEAD_DIM, True)
            a_ctx = _attention(cfg, q, k, v, GQA_KV_HEADS, 0, GQA_HEAD_DIM, False)
            a = jnp.concatenate([a_lat, a_ctx], axis=0)
            stream, h2, topi, topw, tile_counts = _row_call(
                _attn_post_body, cfg, [stream, a], [gqa_w_o[j].astype(BF16)] + route_consts, route_mods, [],
                route_outs, name="gqa_post")
        elif kind == 2:
            n_gate = 4 * MLSTM_HEADS
            w_in = jnp.pad(mlstm_w_in[j], ((0, 0), (0, LANES - n_gate))).astype(BF16)
            gate_b = jnp.pad(mlstm_gate_b[j].reshape(1, n_gate).astype(F32), ((0, 0), (0, LANES - n_gate)))
            col = jnp.arange(LANES)
            fmask = (((col // MLSTM_HEADS) % 2 == 1) & (col < n_gate)).astype(F32).reshape(1, LANES)
            nqk, nv = MLSTM_HEADS * MLSTM_QK, MLSTM_HEADS * MLSTM_V
            stream, q, k, v, og, gates = _row_call(
                functools.partial(_mlstm_pre_body, has_prev), cfg, rows_in, [g1, w_in, gate_b, fmask], mods_in, [],
                [((m, d), F32), ((m, nqk), BF16), ((m, nqk), BF16), ((m, nv), BF16), ((m, nv), BF16),
                 ((m, LANES), F32)], name="mlstm_pre")
            hf, hb = _mlstm_scan(cfg, q, k, v, gates)
            stream, h2, topi, topw, tile_counts = _row_call(
                _mlstm_post_body, cfg, [stream, hf, hb, og],
                [mlstm_w_o[j].astype(BF16), row(mlstm_norm_g[j])] + route_consts, route_mods, [], route_outs,
                name="mlstm_post")
        else:
            rep = GDN_V_HEADS // GDN_QK_HEADS
            qk_w = GDN_QK_HEADS * GDN_HEAD_DIM
            v_w = GDN_V_HEADS * GDN_HEAD_DIM
            ncv = 2 * qk_w + v_w
            w_main = gdn_w_in[j][:, :ncv + v_w].astype(BF16)
            w_ba = gdn_w_in[j][:, ncv + v_w:].reshape(d, 2, 2, GDN_QK_HEADS, rep).transpose(0, 3, 1, 2, 4)
            w_ba = jnp.pad(w_ba.reshape(d, GDN_QK_HEADS, 4 * rep), ((0, 0), (0, 0), (0, LANES - 4 * rep)))
            w_ba = w_ba.reshape(d, GDN_QK_HEADS * LANES).astype(BF16)

            def per_lane(p, fill):
                pv = p.astype(F32).reshape(2, GDN_QK_HEADS, rep).transpose(1, 0, 2)
                full = jnp.full((GDN_QK_HEADS, 2, 2, rep), fill, F32).at[:, :, 1, :].set(pv)
                full = jnp.pad(full.reshape(GDN_QK_HEADS, 4 * rep), ((0, 0), (0, LANES - 4 * rep)),
                               constant_values=fill)
                return full.reshape(1, GDN_QK_HEADS * LANES)

            a_scale = per_lane(jnp.exp(gdn_a_log[j].astype(F32)), 0.0)
            dt_b = per_lane(gdn_dt_bias[j], 0.0)
            dmask = per_lane(jnp.ones((2, GDN_V_HEADS), F32), 0.0)
            stream, zqkv, zg, ba = _row_call(
                functools.partial(_gdn_pre_body, has_prev), cfg, rows_in, [g1, w_main, w_ba, a_scale, dt_b, dmask],
                mods_in, [], [((m, d), F32), ((m, ncv), BF16), ((m, v_w), BF16), ((m, GDN_QK_HEADS * LANES), F32)],
                name="gdn_pre")
            conv_w = jnp.pad(gdn_conv_w[j].astype(F32), ((0, 8 - GDN_CONV), (0, 0)))
            qkv = _gdn_conv(cfg, zqkv, conv_w)
            of, ob = _gdn_scan(cfg, qkv, ba)
            stream, h2, topi, topw, tile_counts = _row_call(
                _gdn_post_body, cfg, [stream, of, ob, zg],
                [gdn_w_o[j].astype(BF16), row(gdn_norm_g[j])] + route_consts, route_mods, [], route_outs,
                name="gdn_post")

        y_prev = _moe_ffn(h2, topi, topw, tile_counts, i, moe_w1, moe_b1[i], moe_w2, moe_b2[i])
        gate2_prev = gate2

    out = _row_call(_final_body, cfg, [stream, y_prev], [row(final_norm_g)], [gate2_prev], [],
                    [((cfg.n_lat, d), F32)], n_tiles=cfg.lat_tiles, name="final_norm")[0]
    return out.reshape(b, s, d)
```

```python
import functools
import math

import jax
import jax.numpy as jnp
from jax import lax
from jax.experimental import pallas as pl
from jax.experimental.pallas import tpu as pltpu

F32 = jnp.float32
BF16 = jnp.bfloat16
HIGHEST = lax.Precision.HIGHEST

D_MODEL = 1024
N_MOD = 6
GRID_W = 64
ROPE_BASE = 10000.0
NORM_EPS = 1e-6
LOG2_E = math.log2(math.e)

MLA_HEADS = 16
MLA_Q_RANK = 512
MLA_KV_RANK = 256
MLA_NOPE = 64
MLA_ROPE = 32
MLA_V = 64

GQA_Q_HEADS = 8
GQA_KV_HEADS = 4
GQA_HEAD_DIM = 128

MLSTM_HEADS = 4
MLSTM_QK = 128
MLSTM_V = 256

GDN_QK_HEADS = 8
GDN_V_HEADS = 16
GDN_HEAD_DIM = 128
GDN_CONV = 5
GDN_QK_PER_STEP = 2

N_EXPERTS = 32
TOP_K = 4
D_EXPERT = 1024
SWIGLU_ALPHA = 1.702
SWIGLU_LIMIT = 7.0

LANES = 128
ROW_TILE = 256
CHUNK = 64
Q_TILE = 256
ATTN_GROUPS_PER_STEP = 2
GROUP_TILE = 512
RANK_TILE = 512
MOD_ROWS = 16
HALO = 16
VMEM_LIMIT = 56 * 1024 * 1024


def _params(sem):
    return pltpu.CompilerParams(dimension_semantics=sem, vmem_limit_bytes=VMEM_LIMIT)


class _Cfg:
    def __init__(self, b, s, nc):
        assert s % ROW_TILE == 0 and nc % ROW_TILE == 0 and (b * s) % nc == 0 and b + 1 <= MOD_ROWS
        self.b, self.s, self.nc = b, s, nc
        self.n_lat = b * s
        self.m = b * s + b * nc
        self.lat_tiles = self.n_lat // ROW_TILE
        self.tiles = self.m // ROW_TILE
        self.tpb = s // ROW_TILE
        self.cpb = nc // ROW_TILE


def _rms(x, g):
    return x * lax.rsqrt(jnp.mean(x * x, axis=-1, keepdims=True) + NORM_EPS) * g


def _norm_mod(x, g, scale, shift):
    return _rms(x, g) * (1.0 + scale) + shift


def _sigmoid(x):
    return 1.0 / (1.0 + jnp.exp(-x))


def _softplus(x):
    return jnp.maximum(x, 0.0) + jnp.log(1.0 + jnp.exp(-jnp.abs(x)))


def _mm(a, b):
    return jnp.dot(a.astype(BF16), b.astype(BF16), preferred_element_type=F32)


def _mm_nt(a, b):
    return lax.dot_general(a.astype(BF16), b.astype(BF16), (((1,), (1,)), ((), ())), preferred_element_type=F32)


def _ada_kernel(c_ref, w_ref, b_ref, o_ref):
    cond = c_ref[...]
    a = cond * _sigmoid(cond)
    o_ref[...] = _mm(a, w_ref[...]) + b_ref[...]


def _ada_table(cond, ada_w, ada_b):
    depth, d, nd = ada_w.shape
    return pl.pallas_call(
        _ada_kernel,
        grid=(depth, nd // d),
        in_specs=[pl.BlockSpec((MOD_ROWS, d), lambda l, n: (0, 0)),
                  pl.BlockSpec((None, d, d), lambda l, n: (l, 0, n)),
                  pl.BlockSpec((None, 1, d), lambda l, n: (l, 0, n))],
        out_specs=pl.BlockSpec((None, MOD_ROWS, d), lambda l, n: (l, 0, n)),
        out_shape=jax.ShapeDtypeStruct((depth, MOD_ROWS, nd), F32),
        compiler_params=_params(("parallel", "parallel")),
        name="ada_table",
    )(cond, ada_w, ada_b.reshape(depth, 1, nd))


def _row_call(body, cfg, rows, consts, mods, tabs, outs, n_tiles=None, name=None):
    n_tiles = cfg.tiles if n_tiles is None else n_tiles
    lat_tiles, tpb = cfg.lat_tiles, cfg.tpb
    arrays, specs = [], []
    for r in rows:
        if isinstance(r, tuple):
            arr, g = r
            arrays.append(arr)
            specs.append(pl.BlockSpec((None, ROW_TILE, arr.shape[-1]), lambda i, g=g: (g, i, 0)))
        else:
            arrays.append(r)
            specs.append(pl.BlockSpec((ROW_TILE, r.shape[-1]), lambda i: (i, 0)))
    for a in consts:
        arrays.append(a)
        specs.append(pl.BlockSpec(a.shape, lambda i, nd=a.ndim: (0,) * nd))
    for a in mods:
        arrays.append(a)
        specs.append(pl.BlockSpec((None, 1, a.shape[-1]),
                                  lambda i: (jnp.where(i < lat_tiles, 1 + i // tpb, 0), 0, 0)))
    for a in tabs:
        arrays.append(a)
        specs.append(pl.BlockSpec((None, ROW_TILE, a.shape[-1]),
                                  lambda i: (jnp.where(i < lat_tiles, 1 + i % tpb, 0), 0, 0)))
    out_shapes, out_specs = [], []
    for shape, dtype in outs:
        out_shapes.append(jax.ShapeDtypeStruct(shape, dtype))
        if shape[0] == TOP_K:
            out_specs.append(pl.BlockSpec((shape[0], ROW_TILE), lambda i: (0, i)))
        elif shape[0] == N_EXPERTS:
            out_specs.append(pl.BlockSpec((shape[0], LANES), lambda i: (0, i)))
        else:
            out_specs.append(pl.BlockSpec((ROW_TILE, shape[-1]), lambda i: (i, 0)))
    n_r, n_c, n_m, n_t = len(rows), len(consts), len(mods), len(tabs)

    def kern(*refs):
        r = refs[:n_r]
        c = refs[n_r:n_r + n_c]
        m = refs[n_r + n_c:n_r + n_c + n_m]
        t = refs[n_r + n_c + n_m:n_r + n_c + n_m + n_t]
        o = refs[n_r + n_c + n_m + n_t:]
        body(r, c, m, t, o)

    return pl.pallas_call(
        kern, grid=(n_tiles,), in_specs=specs, out_specs=out_specs, out_shape=out_shapes,
        compiler_params=_params(("parallel",)), name=name,
    )(*arrays)


def _residual_in(r, m, has_prev):
    if has_prev:
        return r[0][...] + m[0][...] * r[1][...]
    return r[0][...]


def _rope_pair(x, cos, sin_a, sin_b, shift):
    return x * cos + pltpu.roll(x, shift, 1) * sin_a + pltpu.roll(x, LANES - shift, 1) * sin_b


def _mla_pre_body(has_prev, r, c, m, t, o):
    g1, w_in, qg, kvg, wq, wk, wv, place, v_ones = c
    shift1, scale1 = m[-2], m[-1]
    cos_q, sa_q, sb_q, cos_k, sa_k, sb_k = [a[...] for a in t]
    x = _residual_in(r, m, has_prev)
    h = _norm_mod(x, g1[...], scale1[...], shift1[...])
    z = _mm(h, w_in[...])
    cq = _rms(z[:, :MLA_Q_RANK], qg[...]).astype(BF16)
    ckv = _rms(z[:, MLA_Q_RANK:MLA_Q_RANK + MLA_KV_RANK], kvg[...]).astype(BF16)
    kr = z[:, MLA_Q_RANK + MLA_KV_RANK:]
    kr = _rope_pair(kr, cos_k, sa_k, sb_k, MLA_ROPE // 2)
    scale = (MLA_NOPE + MLA_ROPE) ** -0.5 * LOG2_E
    q = _mm(cq, wq[...])
    for hh in range(MLA_HEADS):
        qs = q[:, hh * LANES:(hh + 1) * LANES]
        qs = _rope_pair(qs, cos_q, sa_q, sb_q, MLA_ROPE // 2)
        o[1][:, hh * LANES:(hh + 1) * LANES] = (qs * scale).astype(BF16)
    o[0][...] = x
    o[2][...] = (_mm(ckv, wk[...]) + _mm(kr, place[...])).astype(BF16)
    o[3][...] = (_mm(ckv, wv[...]) + v_ones[...]).astype(BF16)


def _gqa_pre_body(has_prev, r, c, m, t, o):
    g1, w_in, qg, kg = c
    shift1, scale1 = m[-2], m[-1]
    cos, sin = t[0][...], t[1][...]
    x = _residual_in(r, m, has_prev)
    h = _norm_mod(x, g1[...], scale1[...], shift1[...])
    z = _mm(h, w_in[...])
    scale = GQA_HEAD_DIM ** -0.5 * LOG2_E
    hd = GQA_HEAD_DIM
    for hh in range(GQA_Q_HEADS + GQA_KV_HEADS):
        zs = z[:, hh * hd:(hh + 1) * hd]
        is_q = hh < GQA_Q_HEADS
        zs = _rms(zs, qg[...] if is_q else kg[...])
        zs = zs * cos + pltpu.roll(zs, hd // 2, 1) * sin
        if is_q:
            o[1][:, hh * hd:(hh + 1) * hd] = (zs * scale).astype(BF16)
        else:
            kk = hh - GQA_Q_HEADS
            o[2][:, kk * hd:(kk + 1) * hd] = zs.astype(BF16)
    o[0][...] = x
    o[3][...] = z[:, (GQA_Q_HEADS + GQA_KV_HEADS) * hd:].astype(BF16)


def _mlstm_pre_body(has_prev, r, c, m, t, o):
    g1, w_in, gate_b, fmask = c
    shift1, scale1 = m[-2], m[-1]
    x = _residual_in(r, m, has_prev)
    h = _norm_mod(x, g1[...], scale1[...], shift1[...])
    z = _mm(h, w_in[...])
    nqk = MLSTM_HEADS * MLSTM_QK
    nv = MLSTM_HEADS * MLSTM_V
    o[0][...] = x
    o[1][...] = z[:, :nqk].astype(BF16)
    o[2][...] = (z[:, nqk:2 * nqk] * MLSTM_QK ** -0.5).astype(BF16)
    o[3][...] = z[:, 2 * nqk:2 * nqk + nv].astype(BF16)
    o[4][...] = z[:, 2 * nqk + nv:2 * nqk + 2 * nv].astype(BF16)
    gt = z[:, 2 * nqk + 2 * nv:] + gate_b[...]
    log_sig = jnp.minimum(gt, 0.0) - jnp.log(1.0 + jnp.exp(-jnp.abs(gt)))
    o[5][...] = jnp.where(fmask[...] > 0.0, log_sig, gt)


def _gdn_pre_body(has_prev, r, c, m, t, o):
    g1, w_in, w_ba, a_scale, dt_b, dmask = c
    shift1, scale1 = m[-2], m[-1]
    x = _residual_in(r, m, has_prev)
    h = _norm_mod(x, g1[...], scale1[...], shift1[...]).astype(BF16)
    ncv = 2 * GDN_QK_HEADS * GDN_HEAD_DIM + GDN_V_HEADS * GDN_HEAD_DIM
    z = _mm(h, w_in[...])
    o[0][...] = x
    o[1][...] = z[:, :ncv].astype(BF16)
    o[2][...] = z[:, ncv:].astype(BF16)
    ba = _mm(h, w_ba[...])
    o[3][...] = jnp.where(dmask[...] > 0.0, -a_scale[...] * _softplus(ba + dt_b[...]), _sigmoid(ba))


def _route_tail(x, y, c, m, o):
    g2, rw_t, rb = c
    gate1, shift2, scale2 = m
    x1 = x + gate1[...] * y
    h2 = _norm_mod(x1, g2[...], scale2[...], shift2[...])
    o[0][...] = x1
    o[1][...] = h2.astype(BF16)
    logits = lax.dot_general(rw_t[...], h2, (((1,), (1,)), ((), ())), preferred_element_type=F32,
                             precision=HIGHEST) + rb[...]
    e_iota = lax.broadcasted_iota(jnp.int32, logits.shape, 0)
    vals = []
    picked = jnp.zeros(logits.shape, F32)
    for k in range(TOP_K):
        mx = jnp.max(logits, axis=0, keepdims=True)
        idx = jnp.min(jnp.where(logits == mx, e_iota, N_EXPERTS), axis=0, keepdims=True)
        o[2][k:k + 1, :] = idx
        vals.append(mx)
        hit = e_iota == idx
        picked = picked + jnp.where(hit, 1.0, 0.0)
        logits = jnp.where(hit, -jnp.inf, logits)
    o[4][...] = jnp.broadcast_to(jnp.sum(picked, axis=1, keepdims=True), o[4].shape)
    es = [jnp.exp(v - vals[0]) for v in vals]
    tot = es[0] + es[1] + es[2] + es[3]
    for k in range(TOP_K):
        o[3][k:k + 1, :] = es[k] / tot


def _attn_post_body(r, c, m, t, o):
    x, a = r
    y = _mm(a[...], c[0][...])
    _route_tail(x[...], y, c[1:], m, o)


def _mlstm_post_body(r, c, m, t, o):
    x, hf, hb, og = r
    w_o, ng = c[0], c[1]
    hs = hf[...] + hb[...]
    gate = _sigmoid(og[...].astype(F32))
    parts = []
    for hh in range(MLSTM_HEADS):
        sl = slice(hh * MLSTM_V, (hh + 1) * MLSTM_V)
        parts.append((_rms(hs[:, sl], ng[...]) * gate[:, sl]).astype(BF16))
    y = _mm(jnp.concatenate(parts, axis=1), w_o[...])
    _route_tail(x[...], y, c[2:], m, o)


def _gdn_post_body(r, c, m, t, o):
    x, of, ob, zg = r
    w_o, ng = c[0], c[1]
    os_ = of[...] + ob[...]
    g = zg[...].astype(F32)
    gate = g * _sigmoid(g)
    parts = []
    for hh in range(GDN_V_HEADS):
        sl = slice(hh * GDN_HEAD_DIM, (hh + 1) * GDN_HEAD_DIM)
        parts.append((_rms(os_[:, sl], ng[...]) * gate[:, sl]).astype(BF16))
    y = _mm(jnp.concatenate(parts, axis=1), w_o[...])
    _route_tail(x[...], y, c[2:], m, o)


def _final_body(r, c, m, t, o):
    x = r[0][...] + m[0][...] * r[1][...]
    o[0][...] = _rms(x, c[0][...])


def _attn_kernel(*refs, n_seg, k_stride, v_width):
    q_ref = refs[0]
    segs = [(refs[1 + 2 * i], refs[2 + 2 * i]) for i in range(n_seg)]
    o_ref = refs[1 + 2 * n_seg]
    n_heads = 2 * ATTN_GROUPS_PER_STEP

    def scores(j):
        q = q_ref[:, j * LANES:(j + 1) * LANES]
        koff = j * LANES if k_stride else (j // 2) * LANES
        return [_mm_nt(q, k_ref[:, koff:koff + LANES]) for k_ref, _ in segs]

    ss_next = scores(0)
    for j in range(n_heads):
        ss = ss_next
        if j + 1 < n_heads:
            ss_next = scores(j + 1)
        mx = ss[0].max(axis=-1, keepdims=True)
        for s in ss[1:]:
            mx = jnp.maximum(mx, s.max(axis=-1, keepdims=True))
        acc, den = 0.0, 0.0
        for s, (_, v_ref) in zip(ss, segs):
            p = jnp.exp2(s - mx)
            if k_stride:
                acc = acc + _mm(p, v_ref[:, j * 2 * v_width:(j + 1) * 2 * v_width])
            else:
                den = den + p.sum(axis=-1, keepdims=True)
                acc = acc + _mm(p, v_ref[:, (j // 2) * v_width:(j // 2 + 1) * v_width])
        if k_stride:
            den = acc[:, v_width:v_width + 1]
            acc = acc[:, :v_width]
        o_ref[:, j * v_width:(j + 1) * v_width] = (acc / den).astype(o_ref.dtype)


def _attention(cfg, q, k, v, n_groups, k_stride, v_width, latent):
    gps = ATTN_GROUPS_PER_STEP
    kw = gps * (2 * LANES if k_stride else LANES)
    vw = gps * (4 * v_width if k_stride else v_width)
    qw = gps * 2 * LANES
    ow = gps * 2 * v_width
    ctx_blk0 = cfg.n_lat // cfg.nc
    k_ctx = pl.BlockSpec((cfg.nc, kw), lambda b, g, i: (ctx_blk0 + b, g))
    v_ctx = pl.BlockSpec((cfg.nc, vw), lambda b, g, i: (ctx_blk0 + b, g))
    if latent:
        qpb = cfg.s // Q_TILE
        q_spec = pl.BlockSpec((Q_TILE, qw), lambda b, g, i: (b * qpb + i, g))
        o_spec = pl.BlockSpec((Q_TILE, ow), lambda b, g, i: (b * qpb + i, g))
        in_specs = [q_spec, pl.BlockSpec((cfg.s, kw), lambda b, g, i: (b, g)),
                    pl.BlockSpec((cfg.s, vw), lambda b, g, i: (b, g)), k_ctx, v_ctx]
        args = (q, k, v, k, v)
        n_seg = 2
        rows = cfg.n_lat
    else:
        qpb = cfg.nc // Q_TILE
        q0 = cfg.n_lat // Q_TILE
        q_spec = pl.BlockSpec((Q_TILE, qw), lambda b, g, i: (q0 + b * qpb + i, g))
        o_spec = pl.BlockSpec((Q_TILE, ow), lambda b, g, i: (b * qpb + i, g))
        in_specs = [q_spec, k_ctx, v_ctx]
        args = (q, k, v)
        n_seg = 1
        rows = cfg.b * cfg.nc
    grid = (cfg.b, n_groups // gps, qpb)
    return pl.pallas_call(
        functools.partial(_attn_kernel, n_seg=n_seg, k_stride=k_stride, v_width=v_width),
        grid=grid, in_specs=in_specs, out_specs=o_spec,
        out_shape=jax.ShapeDtypeStruct((rows, n_groups * 2 * v_width), BF16),
        compiler_params=_params(("parallel", "parallel", "arbitrary")),
        name="attention_lat" if latent else "attention_ctx",
    )(*args)


def _block_masks(flip):
    li = lax.broadcasted_iota(jnp.int32, (ROW_TILE, ROW_TILE), 0)
    si = lax.broadcasted_iota(jnp.int32, (ROW_TILE, ROW_TILE), 1)
    same = (li // CHUNK) == (si // CHUNK)
    eye = li == si
    incl = same & ((si >= li) if flip else (si <= li))
    strict = same & ((si > li) if flip else (si < li))
    return eye, same, incl, strict


def _scan_block_index(cfg, flip):
    cpb, tpb, lat_tiles = cfg.cpb, cfg.tpb, cfg.lat_tiles

    def row_block(b, j):
        if flip:
            ctx = lat_tiles + b * cpb + (cpb - 1 - j)
            lat = b * tpb + (tpb - 1 - (j - cpb))
        else:
            ctx = lat_tiles + b * cpb + j
            lat = b * tpb + (j - cpb)
        return jnp.where(j < cpb, ctx, lat)

    return row_block


def _mlstm_kernel(q_ref, k_ref, v_ref, g_ref, o_ref, c_st, n_st, m_st, *, flip, d):
    @pl.when(pl.program_id(1) == 0)
    def _():
        c_st[...] = jnp.zeros_like(c_st)
        n_st[...] = jnp.zeros_like(n_st)
        m_st[...] = jnp.zeros_like(m_st)

    eye, same, incl, _ = _block_masks(flip)
    n_chunks = ROW_TILE // CHUNK
    order = [n_chunks - 1 - ci if flip else ci for ci in range(n_chunks)]
    to_row = lambda col: jnp.sum(jnp.where(eye, col, 0.0), axis=0, keepdims=True)
    per_chunk_col = lambda vals: jnp.concatenate(
        [jnp.broadcast_to(vals[c], (CHUNK, 1)) for c in range(n_chunks)], axis=0)
    prep = []
    for hh in range(MLSTM_HEADS):
        q = q_ref[:, hh * MLSTM_QK:(hh + 1) * MLSTM_QK]
        k = k_ref[:, hh * MLSTM_QK:(hh + 1) * MLSTM_QK]
        v = v_ref[:, hh * MLSTM_V:(hh + 1) * MLSTM_V]
        icol = d * 2 * MLSTM_HEADS + hh
        i_col = g_ref[:, icol:icol + 1]
        f_col = g_ref[:, icol + MLSTM_HEADS:icol + MLSTM_HEADS + 1]
        f_row = to_row(f_col)
        i_row = to_row(i_col)
        bcum_col = jnp.sum(jnp.where(incl, f_row, 0.0), axis=1, keepdims=True)
        bcum_row = to_row(bcum_col)
        bend_col = jnp.sum(jnp.where(same, f_row, 0.0), axis=1, keepdims=True)
        w_log = bend_col - bcum_col + i_col
        wmax_col = jnp.max(jnp.where(same, to_row(w_log), -jnp.inf), axis=1, keepdims=True)
        m_prev, m_new = [None] * n_chunks, [None] * n_chunks
        m_run = m_st[hh]
        for cc in order:
            r0 = cc * CHUNK
            m_prev[cc] = m_run
            m_run = jnp.maximum(bend_col[r0:r0 + 1] + m_run, wmax_col[r0:r0 + 1])
            m_new[cc] = m_run
        carries = [jnp.exp(bend_col[cc * CHUNK:cc * CHUNK + 1] + m_prev[cc] - m_new[cc]) for cc in range(n_chunks)]
        kw = k.astype(F32) * jnp.exp(w_log - per_chunk_col(m_new))
        g_log = bcum_col + per_chunk_col(m_prev)
        d_log = jnp.where(incl, bcum_col - bcum_row + i_row, -jnp.inf)
        m_t = jnp.maximum(g_log, jnp.max(d_log, axis=1, keepdims=True))
        inter = jnp.exp(g_log - m_t)
        s = _mm_nt(q, k) * jnp.exp(d_log - m_t)
        sv = _mm(s, v)
        ssum = jnp.sum(s, axis=1, keepdims=True)
        floor = jnp.exp(-m_t)
        kvs = [_mm(kw[cc * CHUNK:(cc + 1) * CHUNK].T, v[cc * CHUNK:(cc + 1) * CHUNK]) for cc in range(n_chunks)]
        ksum = [jnp.sum(kw[cc * CHUNK:(cc + 1) * CHUNK], axis=0, keepdims=True) for cc in range(n_chunks)]
        prep.append((q, inter, sv, ssum, floor, kvs, ksum, carries, m_run))
    for cc in order:
        rows = slice(cc * CHUNK, (cc + 1) * CHUNK)
        for hh in range(MLSTM_HEADS):
            q, inter, sv, ssum, floor, kvs, ksum, carries, _ = prep[hh]
            c_prev, n_prev = c_st[hh], n_st[hh]
            qc = q[rows]
            num = inter[rows] * _mm(qc, c_prev) + sv[rows]
            den = inter[rows] * jnp.sum(qc.astype(F32) * n_prev, axis=1, keepdims=True) + ssum[rows]
            o_ref[rows, hh * MLSTM_V:(hh + 1) * MLSTM_V] = num / jnp.maximum(jnp.abs(den), floor[rows])
            c_st[hh] = carries[cc] * c_prev + kvs[cc]
            n_st[hh] = carries[cc] * n_prev + ksum[cc]
    for hh in range(MLSTM_HEADS):
        m_st[hh] = prep[hh][-1]


def _mlstm_scan(cfg, q, k, v, gates):
    nqk, nv = MLSTM_HEADS * MLSTM_QK, MLSTM_HEADS * MLSTM_V
    outs = []
    for d in range(2):
        flip = d == 1
        rb = _scan_block_index(cfg, flip)
        row = lambda w, rb=rb: pl.BlockSpec((ROW_TILE, w), lambda b, j: (rb(b, j), 0))
        outs.append(pl.pallas_call(
            functools.partial(_mlstm_kernel, flip=flip, d=d),
            grid=(cfg.b, cfg.cpb + cfg.tpb),
            in_specs=[row(nqk), row(nqk), row(nv), row(LANES)],
            out_specs=row(nv),
            out_shape=jax.ShapeDtypeStruct((cfg.m, nv), F32),
            scratch_shapes=[pltpu.VMEM((MLSTM_HEADS, MLSTM_QK, MLSTM_V), F32),
                            pltpu.VMEM((MLSTM_HEADS, 1, MLSTM_QK), F32),
                            pltpu.VMEM((MLSTM_HEADS, 1, 1), F32)],
            compiler_params=_params(("parallel", "arbitrary")),
            name="mlstm_bwd" if flip else "mlstm_fwd",
        )(q, k, v, gates))
    return outs


def _gdn_kernel(q_ref, k_ref, v_ref, ba_ref, o_ref, s_st, *, flip, d):
    @pl.when(pl.program_id(2) == 0)
    def _():
        s_st[...] = jnp.zeros_like(s_st)

    eye, same, incl, strict = _block_masks(flip)
    eye_f = jnp.where(eye, 1.0, 0.0)
    rep = GDN_V_HEADS // GDN_QK_HEADS
    hd = GDN_HEAD_DIM
    n_chunks = ROW_TILE // CHUNK
    heads = [(qi, jj) for qi in range(GDN_QK_PER_STEP) for jj in range(rep)]
    ks = [k_ref[:, qi * hd:(qi + 1) * hd] for qi in range(GDN_QK_PER_STEP)]
    kfs = [k.astype(F32) for k in ks]
    qfs = [q_ref[:, qi * hd:(qi + 1) * hd].astype(F32) for qi in range(GDN_QK_PER_STEP)]
    qks = [_mm_nt(q_ref[:, qi * hd:(qi + 1) * hd], ks[qi]) for qi in range(GDN_QK_PER_STEP)]
    betas, gcs, gces, decays, kbs = [], [], [], [], []
    for qi, jj in heads:
        bcol = qi * LANES + d * 2 * rep + jj
        beta = ba_ref[:, bcol:bcol + 1]
        ld = ba_ref[:, bcol + rep:bcol + rep + 1]
        ld_row = jnp.sum(jnp.where(eye, ld, 0.0), axis=0, keepdims=True)
        gc_col = jnp.sum(jnp.where(incl, ld_row, 0.0), axis=1, keepdims=True)
        gc_row = jnp.sum(jnp.where(eye, gc_col, 0.0), axis=0, keepdims=True)
        gces.append(jnp.sum(jnp.where(same, ld_row, 0.0), axis=1, keepdims=True))
        decays.append(jnp.exp(jnp.where(incl, gc_col - gc_row, -jnp.inf)))
        betas.append(beta)
        gcs.append(gc_col)
        kbs.append(kfs[qi] * beta)
    pws = [jnp.where(strict, _mm_nt(kbs[i], ks[qi]) * decays[i], 0.0) for i, (qi, _) in enumerate(heads)]
    invs = [eye_f - p for p in pws]
    for _ in range(int(math.log2(CHUNK)) - 1):
        pws = [_mm(p, p) for p in pws]
        invs = [iv + _mm(iv, p) for iv, p in zip(invs, pws)]
    egs = [jnp.exp(g) for g in gcs]
    uws = []
    for i, (qi, jj) in enumerate(heads):
        v = v_ref[:, (qi * rep + jj) * hd:(qi * rep + jj + 1) * hd].astype(F32)
        uws.append(_mm(invs[i], jnp.concatenate([v * betas[i], kbs[i] * egs[i]], axis=1)))
    aws = [_mm(qks[qi] * decays[i], uws[i]) for i, (qi, _) in enumerate(heads)]
    qes = [qfs[qi] * egs[i] - aws[i][:, hd:] for i, (qi, _) in enumerate(heads)]
    kdecs = [kfs[qi] * jnp.exp(gces[i] - gcs[i]) for i, (qi, _) in enumerate(heads)]
    for ci in range(n_chunks):
        cc = n_chunks - 1 - ci if flip else ci
        rows = slice(cc * CHUNK, (cc + 1) * CHUNK)
        bgs = [_mm(kdecs[i][rows].T, uws[i][rows]) for i in range(len(heads))]
        for i, (qi, jj) in enumerate(heads):
            s_prev = s_st[i]
            col = (qi * rep + jj) * hd
            o_ref[rows, col:col + hd] = _mm(qes[i][rows], s_prev) + aws[i][rows, :hd]
            g_end = jnp.exp(gces[i][cc * CHUNK:cc * CHUNK + 1])
            s_st[i] = g_end * s_prev - _mm(bgs[i][:, hd:], s_prev) + bgs[i][:, :hd]


def _gdn_scan(cfg, qkv, ba):
    rep = GDN_V_HEADS // GDN_QK_HEADS
    hd = GDN_HEAD_DIM
    g = GDN_QK_PER_STEP
    outs = []
    for d in range(2):
        flip = d == 1
        rb = _scan_block_index(cfg, flip)
        col = lambda w, off, rb=rb: pl.BlockSpec((ROW_TILE, w), lambda b, h, j: (rb(b, j), off + h))
        outs.append(pl.pallas_call(
            functools.partial(_gdn_kernel, flip=flip, d=d),
            grid=(cfg.b, GDN_QK_HEADS // g, cfg.cpb + cfg.tpb),
            in_specs=[col(g * hd, 0), col(g * hd, GDN_QK_HEADS // g),
                      col(g * rep * hd, 2 * GDN_QK_HEADS // (g * rep)), col(g * LANES, 0)],
            out_specs=col(g * rep * hd, 0),
            out_shape=jax.ShapeDtypeStruct((cfg.m, GDN_V_HEADS * hd), F32),
            scratch_shapes=[pltpu.VMEM((g * rep, hd, hd), F32)],
            compiler_params=_params(("parallel", "parallel", "arbitrary")),
            name="gdn_bwd" if flip else "gdn_fwd",
        )(qkv, qkv, qkv, ba))
    return outs


def _gdn_conv_kernel(cur_ref, prev_ref, next_ref, w_ref, o_ref, pad_ref, *, lat_tiles, tpb, cpb):
    i = pl.program_id(0)
    cb = pl.program_id(1)
    lat = i < lat_tiles
    pos = jnp.where(lat, i % tpb, (i - lat_tiles) % cpb)
    per = jnp.where(lat, tpb, cpb)
    first = pos == 0
    last = pos == per - 1
    prev = prev_ref[...].astype(F32)
    nxt = next_ref[...].astype(F32)
    pad_ref[0:HALO, :] = jnp.where(first, 0.0, prev)
    pad_ref[HALO:HALO + ROW_TILE, :] = cur_ref[...].astype(F32)
    pad_ref[HALO + ROW_TILE:, :] = jnp.where(last, 0.0, nxt)
    half = GDN_CONV // 2
    acc = 0.0
    for tap in range(GDN_CONV):
        off = HALO - half + tap
        acc = acc + pad_ref[off:off + ROW_TILE, :] * w_ref[tap:tap + 1, :]
    act = acc * _sigmoid(acc)
    qscale = jnp.where(cb == 0, GDN_HEAD_DIM ** -0.5, 1.0)
    hd = GDN_HEAD_DIM
    for hh in range(act.shape[1] // hd):
        a = act[:, hh * hd:(hh + 1) * hd]
        nrm = a * lax.rsqrt(jnp.sum(a * a, axis=-1, keepdims=True) + NORM_EPS) * qscale
        o_ref[:, hh * hd:(hh + 1) * hd] = jnp.where(cb < 2, nrm, a).astype(o_ref.dtype)


def _gdn_conv(cfg, zqkv, conv_w):
    cw = GDN_QK_HEADS * GDN_HEAD_DIM
    n_cb = zqkv.shape[1] // cw
    hpt = ROW_TILE // HALO
    n_halo = cfg.m // HALO
    return pl.pallas_call(
        functools.partial(_gdn_conv_kernel, lat_tiles=cfg.lat_tiles, tpb=cfg.tpb, cpb=cfg.cpb),
        grid=(cfg.tiles, n_cb),
        in_specs=[pl.BlockSpec((ROW_TILE, cw), lambda i, c: (i, c)),
                  pl.BlockSpec((HALO, cw), lambda i, c: (jnp.maximum(i * hpt - 1, 0), c)),
                  pl.BlockSpec((HALO, cw), lambda i, c: (jnp.minimum((i + 1) * hpt, n_halo - 1), c)),
                  pl.BlockSpec((8, cw), lambda i, c: (0, c))],
        out_specs=pl.BlockSpec((ROW_TILE, cw), lambda i, c: (i, c)),
        out_shape=jax.ShapeDtypeStruct(zqkv.shape, BF16),
        scratch_shapes=[pltpu.VMEM((ROW_TILE + 2 * HALO, cw), F32)],
        compiler_params=_params(("parallel", "parallel")),
        name="gdn_conv",
    )(zqkv, zqkv, zqkv, conv_w)


def _rank_kernel(e_ref, start_ref, slot_ref, run_ref):
    @pl.when(pl.program_id(0) == 0)
    def _():
        run_ref[...] = start_ref[...]

    e = e_ref[...]
    onehot = (lax.broadcasted_iota(jnp.int32, (N_EXPERTS, RANK_TILE), 0) == e)
    oh = jnp.where(onehot, 1.0, 0.0)
    ji = lax.broadcasted_iota(jnp.int32, (RANK_TILE, RANK_TILE), 0)
    si = lax.broadcasted_iota(jnp.int32, (RANK_TILE, RANK_TILE), 1)
    upper = jnp.where(ji <= si, 1.0, 0.0).astype(BF16)
    cum = jnp.dot(oh.astype(BF16), upper, preferred_element_type=F32)
    run = run_ref[:, 0:1]
    slot = jnp.sum(oh * (cum - 1.0 + run), axis=0, keepdims=True)
    slot_ref[...] = slot.astype(jnp.int32)
    run_ref[...] = run_ref[...] + jnp.sum(oh, axis=1, keepdims=True)


def _slot_pairs(expert_of_pair, group_start):
    n_pairs = expert_of_pair.shape[1]
    return pl.pallas_call(
        _rank_kernel,
        grid=(n_pairs // RANK_TILE,),
        in_specs=[pl.BlockSpec((1, RANK_TILE), lambda i: (0, i)),
                  pl.BlockSpec((N_EXPERTS, LANES), lambda i: (0, 0))],
        out_specs=pl.BlockSpec((1, RANK_TILE), lambda i: (0, i)),
        out_shape=jax.ShapeDtypeStruct((1, n_pairs), jnp.int32),
        scratch_shapes=[pltpu.VMEM((N_EXPERTS, LANES), F32)],
        compiler_params=_params(("arbitrary",)),
        name="moe_rank",
    )(expert_of_pair, group_start)


def _moe_kernel(te_ref, nu_ref, x_ref, w1_ref, b1g_ref, b1l_ref, w2_ref, b2_ref, o_ref, w1g_s, w1l_s, w2_s):
    j = pl.program_id(0)
    used = j < nu_ref[0]
    new_expert = jnp.logical_or(j == 0, te_ref[j] != te_ref[jnp.maximum(j - 1, 0)])

    @pl.when(jnp.logical_and(used, new_expert))
    def _():
        slab = 2 * LANES
        src = lax.broadcasted_iota(jnp.int32, (slab, slab), 0)
        dst = lax.broadcasted_iota(jnp.int32, (slab, slab), 1)
        want = jnp.where(dst < LANES, 2 * dst, 2 * (dst - LANES) + 1)
        perm = jnp.where(src == want, 1.0, 0.0).astype(BF16)
        for sb in range(w1_ref.shape[1] // slab):
            sorted_cols = jnp.dot(w1_ref[:, sb * slab:(sb + 1) * slab].astype(BF16), perm,
                                  preferred_element_type=F32)
            w1g_s[:, sb * LANES:(sb + 1) * LANES] = sorted_cols[:, :LANES].astype(BF16)
            w1l_s[:, sb * LANES:(sb + 1) * LANES] = sorted_cols[:, LANES:].astype(BF16)
        w2_s[...] = w2_ref[...].astype(BF16)

    @pl.when(used)
    def _():
        x = x_ref[...]
        glu = jnp.minimum(jnp.dot(x, w1g_s[...], preferred_element_type=F32) + b1g_ref[...], SWIGLU_LIMIT)
        lin = jnp.clip(jnp.dot(x, w1l_s[...], preferred_element_type=F32) + b1l_ref[...],
                       -SWIGLU_LIMIT, SWIGLU_LIMIT)
        act = glu * _sigmoid(SWIGLU_ALPHA * glu) * (lin + 1.0)
        y = jnp.dot(act.astype(BF16), w2_s[...], preferred_element_type=F32) + b2_ref[...]
        o_ref[...] = y.astype(o_ref.dtype)

    @pl.when(jnp.logical_not(used))
    def _():
        o_ref[...] = jnp.zeros_like(o_ref)


def _moe_grouped(xs, tile_expert, n_used, layer, w1_all, b1g, b1l, w2_all, b2):
    n_slots, d = xs.shape
    de = w2_all.shape[2]
    wspec = lambda r, c: pl.BlockSpec((None, r, c), lambda j, te, nu: (te[j], 0, 0))
    lspec = lambda r, c: pl.BlockSpec((None, None, r, c), lambda j, te, nu: (layer, te[j], 0, 0))
    return pl.pallas_call(
        _moe_kernel,
        grid_spec=pltpu.PrefetchScalarGridSpec(
            num_scalar_prefetch=2,
            grid=(n_slots // GROUP_TILE,),
            in_specs=[pl.BlockSpec((GROUP_TILE, d), lambda j, te, nu: (j, 0)),
                      lspec(d, 2 * de), wspec(1, de), wspec(1, de), lspec(de, d), wspec(1, d)],
            out_specs=pl.BlockSpec((GROUP_TILE, d), lambda j, te, nu: (j, 0)),
            scratch_shapes=[pltpu.VMEM((d, de), BF16), pltpu.VMEM((d, de), BF16), pltpu.VMEM((de, d), BF16)]),
        out_shape=jax.ShapeDtypeStruct((n_slots, d), BF16),
        compiler_params=_params(("arbitrary",)),
        name="moe_grouped",
    )(tile_expert, n_used, xs, w1_all, b1g, b1l, w2_all, b2)


def _moe_ffn(h2, topi, topw, tile_counts, layer, w1_all, b1, w2_all, b2):
    n_tok = h2.shape[0]
    n_pairs = TOP_K * n_tok
    assert n_pairs % RANK_TILE == 0
    n_slots = -(-(n_pairs + N_EXPERTS * (GROUP_TILE - 1)) // GROUP_TILE) * GROUP_TILE
    n_tiles = n_slots // GROUP_TILE

    counts = jnp.sum(tile_counts.reshape(N_EXPERTS, -1, LANES)[:, :, 0], axis=1).astype(jnp.int32)
    padded = (counts + GROUP_TILE - 1) // GROUP_TILE * GROUP_TILE
    ends = jnp.cumsum(padded)
    starts = ends - padded
    group_start = jnp.broadcast_to(starts.astype(F32)[:, None], (N_EXPERTS, LANES))
    slot = _slot_pairs(topi.reshape(1, n_pairs), group_start)[0]
    token = jnp.tile(jnp.arange(n_tok, dtype=jnp.int32), TOP_K)
    src = (jnp.arange(n_slots, dtype=jnp.int32) % n_tok).at[slot].set(
        token, unique_indices=True, mode="promise_in_bounds")
    tile_start = jnp.arange(n_tiles, dtype=jnp.int32) * GROUP_TILE
    tile_expert = jnp.sum((ends[None, :] <= tile_start[:, None]).astype(jnp.int32), axis=1)
    tile_expert = jnp.minimum(tile_expert, N_EXPERTS - 1)
    n_used = (ends[-1] // GROUP_TILE).astype(jnp.int32).reshape(1)

    xs = jnp.take(h2, src, axis=0, mode="clip")
    b1g = b1[:, None, 0::2]
    b1l = b1[:, None, 1::2]
    ys = _moe_grouped(xs, tile_expert, n_used, layer, w1_all, b1g, b1l, w2_all, b2[:, None, :])

    pos = slot.reshape(TOP_K, n_tok)
    y = jnp.zeros((n_tok, h2.shape[1]), F32)
    for k in range(TOP_K):
        y = y + topw[k][:, None] * jnp.take(ys, pos[k], axis=0, mode="clip").astype(F32)
    return y


def _axial_angles(n_tokens, rot_dim):
    t = jnp.arange(n_tokens, dtype=jnp.int32)
    rows = (t // GRID_W).astype(F32)
    cols = (t % GRID_W).astype(F32)
    d_axis = rot_dim // 2
    inv_freq = ROPE_BASE ** (-jnp.arange(0, d_axis, 2, dtype=F32) / d_axis)
    ang = jnp.concatenate([rows[:, None] * inv_freq, cols[:, None] * inv_freq], axis=-1)
    return jnp.cos(ang), jnp.sin(ang)


def _seq_table(cfg, lat_rows, ctx_value):
    c = lat_rows.shape[-1]
    ctx = jnp.full((1, ROW_TILE, c), ctx_value, F32)
    return jnp.concatenate([ctx, lat_rows.reshape(cfg.tpb, ROW_TILE, c)], axis=0)


def _slab_rope_tables(cfg, rot_dim, lane0):
    cos, sin = _axial_angles(cfg.s, rot_dim)
    half = rot_dim // 2
    s = cfg.s
    ones = lambda n: jnp.ones((s, n), F32)
    zeros = lambda n: jnp.zeros((s, n), F32)
    tail = LANES - lane0 - rot_dim
    cos_t = jnp.concatenate([ones(lane0), cos, cos, ones(tail)], axis=1)
    sin_a = jnp.concatenate([zeros(lane0 + half), sin, zeros(tail)], axis=1)
    sin_b = jnp.concatenate([zeros(lane0), -sin, zeros(half + tail)], axis=1)
    return _seq_table(cfg, cos_t, 1.0), _seq_table(cfg, sin_a, 0.0), _seq_table(cfg, sin_b, 0.0)


def _mod_slices(table, n_rows):
    d = D_MODEL
    return [table[:, k * d:(k + 1) * d].reshape(n_rows, 1, d) for k in range(N_MOD)]


def kernel(x, c, ctx, c_ctx, ada_w, ada_b, norm1_g, norm2_g, router_w, router_b, moe_w1, moe_b1, moe_w2, moe_b2,
           mla_w_in, mla_q_norm_g, mla_kv_norm_g, mla_w_uq, mla_w_ukv, mla_w_o, gqa_w_in, gqa_q_norm_g,
           gqa_k_norm_g, gqa_w_o, mlstm_w_in, mlstm_gate_b, mlstm_norm_g, mlstm_w_o, gdn_w_in, gdn_conv_w,
           gdn_a_log, gdn_dt_bias, gdn_norm_g, gdn_w_o, final_norm_g):
    b, s, d = x.shape
    nc = ctx.shape[1]
    depth = ada_w.shape[0]
    cfg = _Cfg(b, s, nc)
    m = cfg.m
    row = lambda a: a.reshape(1, -1).astype(F32)

    cond = jnp.concatenate([c_ctx[None, :], c, jnp.zeros((MOD_ROWS - 1 - b, d), F32)], axis=0)
    mod_all = _ada_table(cond, ada_w, ada_b)

    stream = jnp.concatenate([x.reshape(b * s, d), ctx.reshape(b * nc, d)], axis=0)
    y_prev = None
    gate2_prev = None
    for i in range(depth):
        kind, j = i % 4, i // 4
        shift1, scale1, gate1, shift2, scale2, gate2 = _mod_slices(mod_all[i], MOD_ROWS)
        has_prev = y_prev is not None
        rows_in = [stream, y_prev] if has_prev else [stream]
        mods_in = ([gate2_prev] if has_prev else []) + [shift1, scale1]
        g1 = row(norm1_g[i])
        route_consts = [row(norm2_g[i]), router_w[i].T, router_b[i].reshape(N_EXPERTS, 1)]
        route_mods = [gate1, shift2, scale2]
        route_outs = [((m, d), F32), ((m, d), BF16), ((TOP_K, m), jnp.int32), ((TOP_K, m), F32),
                      ((N_EXPERTS, cfg.tiles * LANES), F32)]

        if kind == 0:
            qk = MLA_NOPE + MLA_ROPE
            w_in = jnp.pad(mla_w_in[j], ((0, 0), (0, LANES - MLA_ROPE))).astype(BF16)
            wq = jnp.pad(mla_w_uq[j].reshape(MLA_Q_RANK, MLA_HEADS, qk), ((0, 0), (0, 0), (0, LANES - qk)))
            wq = wq.reshape(MLA_Q_RANK, MLA_HEADS * LANES).astype(BF16)
            wkv = mla_w_ukv[j].reshape(MLA_KV_RANK, MLA_HEADS, MLA_NOPE + MLA_V)
            wk = jnp.pad(wkv[:, :, :MLA_NOPE], ((0, 0), (0, 0), (0, LANES - MLA_NOPE)))
            wk = wk.reshape(MLA_KV_RANK, MLA_HEADS * LANES).astype(BF16)
            wv = jnp.pad(wkv[:, :, MLA_NOPE:], ((0, 0), (0, 0), (0, MLA_V)))
            wv = wv.reshape(MLA_KV_RANK, MLA_HEADS * 2 * MLA_V).astype(BF16)
            v_ones = jnp.tile(jnp.concatenate([jnp.zeros((MLA_V,), F32), jnp.ones((MLA_V,), F32)]), MLA_HEADS)
            v_ones = v_ones.reshape(1, MLA_HEADS * 2 * MLA_V)
            place = jnp.zeros((LANES, LANES), F32).at[jnp.arange(MLA_ROPE), MLA_NOPE + jnp.arange(MLA_ROPE)].set(1.0)
            place = jnp.tile(place, (1, MLA_HEADS)).astype(BF16)
            tabs = list(_slab_rope_tables(cfg, MLA_ROPE, MLA_NOPE)) + list(_slab_rope_tables(cfg, MLA_ROPE, 0))
            stream, q, k, v = _row_call(
                functools.partial(_mla_pre_body, has_prev), cfg, rows_in,
                [g1, w_in, row(mla_q_norm_g[j]), row(mla_kv_norm_g[j]), wq, wk, wv, place, v_ones], mods_in, tabs,
                [((m, d), F32), ((m, MLA_HEADS * LANES), BF16), ((m, MLA_HEADS * LANES), BF16),
                 ((m, MLA_HEADS * 2 * MLA_V), BF16)], name="mla_pre")
            a_lat = _attention(cfg, q, k, v, MLA_HEADS // 2, LANES, MLA_V, True)
            a_ctx = _attention(cfg, q, k, v, MLA_HEADS // 2, LANES, MLA_V, False)
            a = jnp.concatenate([a_lat, a_ctx], axis=0)
            stream, h2, topi, topw, tile_counts = _row_call(
                _attn_post_body, cfg, [stream, a], [mla_w_o[j].astype(BF16)] + route_consts, route_mods, [],
                route_outs, name="mla_post")
        elif kind == 1:
            cos, sin = _axial_angles(cfg.s, GQA_HEAD_DIM)
            tabs = [_seq_table(cfg, jnp.concatenate([cos, cos], axis=1), 1.0),
                    _seq_table(cfg, jnp.concatenate([-sin, sin], axis=1), 0.0)]
            stream, q, k, v = _row_call(
                functools.partial(_gqa_pre_body, has_prev), cfg, rows_in,
                [g1, gqa_w_in[j].astype(BF16), row(gqa_q_norm_g[j]), row(gqa_k_norm_g[j])], mods_in, tabs,
                [((m, d), F32), ((m, GQA_Q_HEADS * GQA_HEAD_DIM), BF16), ((m, GQA_KV_HEADS * GQA_HEAD_DIM), BF16),
                 ((m, GQA_KV_HEADS * GQA_HEAD_DIM), BF16)], name="gqa_pre")
            a_lat = _attention(cfg, q, k, v, GQA_KV_HEADS, 0, GQA_HEAD_DIM, True)
            a_ctx = _attention(cfg, q, k, v, GQA_KV_HEADS, 0, GQA_HEAD_DIM, False)
            a = jnp.concatenate([a_lat, a_ctx], axis=0)
            stream, h2, topi, topw, tile_counts = _row_call(
                _attn_post_body, cfg, [stream, a], [gqa_w_o[j].astype(BF16)] + route_consts, route_mods, [],
                route_outs, name="gqa_post")
        elif kind == 2:
            n_gate = 4 * MLSTM_HEADS
            w_in = jnp.pad(mlstm_w_in[j], ((0, 0), (0, LANES - n_gate))).astype(BF16)
            gate_b = jnp.pad(mlstm_gate_b[j].reshape(1, n_gate).astype(F32), ((0, 0), (0, LANES - n_gate)))
            col = jnp.arange(LANES)
            fmask = (((col // MLSTM_HEADS) % 2 == 1) & (col < n_gate)).astype(F32).reshape(1, LANES)
            nqk, nv = MLSTM_HEADS * MLSTM_QK, MLSTM_HEADS * MLSTM_V
            stream, q, k, v, og, gates = _row_call(
                functools.partial(_mlstm_pre_body, has_prev), cfg, rows_in, [g1, w_in, gate_b, fmask], mods_in, [],
                [((m, d), F32), ((m, nqk), BF16), ((m, nqk), BF16), ((m, nv), BF16), ((m, nv), BF16),
                 ((m, LANES), F32)], name="mlstm_pre")
            hf, hb = _mlstm_scan(cfg, q, k, v, gates)
            stream, h2, topi, topw, tile_counts = _row_call(
                _mlstm_post_body, cfg, [stream, hf, hb, og],
                [mlstm_w_o[j].astype(BF16), row(mlstm_norm_g[j])] + route_consts, route_mods, [], route_outs,
                name="mlstm_post")
        else:
            rep = GDN_V_HEADS // GDN_QK_HEADS
            qk_w = GDN_QK_HEADS * GDN_HEAD_DIM
            v_w = GDN_V_HEADS * GDN_HEAD_DIM
            ncv = 2 * qk_w + v_w
            w_main = gdn_w_in[j][:, :ncv + v_w].astype(BF16)
            w_ba = gdn_w_in[j][:, ncv + v_w:].reshape(d, 2, 2, GDN_QK_HEADS, rep).transpose(0, 3, 1, 2, 4)
            w_ba = jnp.pad(w_ba.reshape(d, GDN_QK_HEADS, 4 * rep), ((0, 0), (0, 0), (0, LANES - 4 * rep)))
            w_ba = w_ba.reshape(d, GDN_QK_HEADS * LANES).astype(BF16)

            def per_lane(p, fill):
                pv = p.astype(F32).reshape(2, GDN_QK_HEADS, rep).transpose(1, 0, 2)
                full = jnp.full((GDN_QK_HEADS, 2, 2, rep), fill, F32).at[:, :, 1, :].set(pv)
                full = jnp.pad(full.reshape(GDN_QK_HEADS, 4 * rep), ((0, 0), (0, LANES - 4 * rep)),
                               constant_values=fill)
                return full.reshape(1, GDN_QK_HEADS * LANES)

            a_scale = per_lane(jnp.exp(gdn_a_log[j].astype(F32)), 0.0)
            dt_b = per_lane(gdn_dt_bias[j], 0.0)
            dmask = per_lane(jnp.ones((2, GDN_V_HEADS), F32), 0.0)
            stream, zqkv, zg, ba = _row_call(
                functools.partial(_gdn_pre_body, has_prev), cfg, rows_in, [g1, w_main, w_ba, a_scale, dt_b, dmask],
                mods_in, [], [((m, d), F32), ((m, ncv), BF16), ((m, v_w), BF16), ((m, GDN_QK_HEADS * LANES), F32)],
                name="gdn_pre")
            conv_w = jnp.pad(gdn_conv_w[j].astype(F32), ((0, 8 - GDN_CONV), (0, 0)))
            qkv = _gdn_conv(cfg, zqkv, conv_w)
            of, ob = _gdn_scan(cfg, qkv, ba)
            stream, h2, topi, topw, tile_counts = _row_call(
                _gdn_post_body, cfg, [stream, of, ob, zg],
                [gdn_w_o[j].astype(BF16), row(gdn_norm_g[j])] + route_consts, route_mods, [], route_outs,
                name="gdn_post")

        y_prev = _moe_ffn(h2, topi, topw, tile_counts, i, moe_w1, moe_b1[i], moe_w2, moe_b2[i])
        gate2_prev = gate2

    out = _row_call(_final_body, cfg, [stream, y_prev], [row(final_norm_g)], [gate2_prev], [],
                    [((cfg.n_lat, d), F32)], n_tiles=cfg.lat_tiles, name="final_norm")[0]
    return out.reshape(b, s, d)
```

```python
import functools
import math

import jax
import jax.numpy as jnp
from jax import lax
from jax.experimental import pallas as pl
from jax.experimental.pallas import tpu as pltpu

F32 = jnp.float32
BF16 = jnp.bfloat16
HIGHEST = lax.Precision.HIGHEST

D_MODEL = 1024
N_MOD = 6
GRID_W = 64
ROPE_BASE = 10000.0
NORM_EPS = 1e-6
LOG2_E = math.log2(math.e)

MLA_HEADS = 16
MLA_Q_RANK = 512
MLA_KV_RANK = 256
MLA_NOPE = 64
MLA_ROPE = 32
MLA_V = 64

GQA_Q_HEADS = 8
GQA_KV_HEADS = 4
GQA_HEAD_DIM = 128

MLSTM_HEADS = 4
MLSTM_QK = 128
MLSTM_V = 256

GDN_QK_HEADS = 8
GDN_V_HEADS = 16
GDN_HEAD_DIM = 128
GDN_CONV = 5
GDN_QK_PER_STEP = 2

N_EXPERTS = 32
TOP_K = 4
D_EXPERT = 1024
SWIGLU_ALPHA = 1.702
SWIGLU_LIMIT = 7.0

LANES = 128
ROW_TILE = 256
CHUNK = 64
Q_TILE = 512
ATTN_GROUPS_PER_STEP = 2
GROUP_TILE = 512
RANK_TILE = 512
MOD_ROWS = 16
HALO = 16
VMEM_LIMIT = 56 * 1024 * 1024


def _params(sem):
    return pltpu.CompilerParams(dimension_semantics=sem, vmem_limit_bytes=VMEM_LIMIT)


class _Cfg:
    def __init__(self, b, s, nc):
        assert s % ROW_TILE == 0 and nc % ROW_TILE == 0 and (b * s) % nc == 0 and b + 1 <= MOD_ROWS
        self.b, self.s, self.nc = b, s, nc
        self.n_lat = b * s
        self.m = b * s + b * nc
        self.lat_tiles = self.n_lat // ROW_TILE
        self.tiles = self.m // ROW_TILE
        self.tpb = s // ROW_TILE
        self.cpb = nc // ROW_TILE


def _rms(x, g):
    return x * lax.rsqrt(jnp.mean(x * x, axis=-1, keepdims=True) + NORM_EPS) * g


def _norm_mod(x, g, scale, shift):
    return _rms(x, g) * (1.0 + scale) + shift


def _sigmoid(x):
    return 1.0 / (1.0 + jnp.exp(-x))


def _softplus(x):
    return jnp.maximum(x, 0.0) + jnp.log(1.0 + jnp.exp(-jnp.abs(x)))


def _mm(a, b):
    return jnp.dot(a.astype(BF16), b.astype(BF16), preferred_element_type=F32)


def _mm_nt(a, b):
    return lax.dot_general(a.astype(BF16), b.astype(BF16), (((1,), (1,)), ((), ())), preferred_element_type=F32)


def _ada_kernel(c_ref, w_ref, b_ref, o_ref):
    cond = c_ref[...]
    a = cond * _sigmoid(cond)
    o_ref[...] = _mm(a, w_ref[...]) + b_ref[...]


def _ada_table(cond, ada_w, ada_b):
    depth, d, nd = ada_w.shape
    return pl.pallas_call(
        _ada_kernel,
        grid=(depth, nd // d),
        in_specs=[pl.BlockSpec((MOD_ROWS, d), lambda l, n: (0, 0)),
                  pl.BlockSpec((None, d, d), lambda l, n: (l, 0, n)),
                  pl.BlockSpec((None, 1, d), lambda l, n: (l, 0, n))],
        out_specs=pl.BlockSpec((None, MOD_ROWS, d), lambda l, n: (l, 0, n)),
        out_shape=jax.ShapeDtypeStruct((depth, MOD_ROWS, nd), F32),
        compiler_params=_params(("parallel", "parallel")),
        name="ada_table",
    )(cond, ada_w, ada_b.reshape(depth, 1, nd))


def _row_call(body, cfg, rows, consts, mods, tabs, outs, n_tiles=None, name=None):
    n_tiles = cfg.tiles if n_tiles is None else n_tiles
    lat_tiles, tpb = cfg.lat_tiles, cfg.tpb
    arrays, specs = [], []
    for r in rows:
        if isinstance(r, tuple):
            arr, g = r
            arrays.append(arr)
            specs.append(pl.BlockSpec((None, ROW_TILE, arr.shape[-1]), lambda i, g=g: (g, i, 0)))
        else:
            arrays.append(r)
            specs.append(pl.BlockSpec((ROW_TILE, r.shape[-1]), lambda i: (i, 0)))
    for a in consts:
        arrays.append(a)
        specs.append(pl.BlockSpec(a.shape, lambda i, nd=a.ndim: (0,) * nd))
    for a in mods:
        arrays.append(a)
        specs.append(pl.BlockSpec((None, 1, a.shape[-1]),
                                  lambda i: (jnp.where(i < lat_tiles, 1 + i // tpb, 0), 0, 0)))
    for a in tabs:
        arrays.append(a)
        specs.append(pl.BlockSpec((None, ROW_TILE, a.shape[-1]),
                                  lambda i: (jnp.where(i < lat_tiles, 1 + i % tpb, 0), 0, 0)))
    out_shapes, out_specs = [], []
    for shape, dtype in outs:
        out_shapes.append(jax.ShapeDtypeStruct(shape, dtype))
        if shape[0] == TOP_K:
            out_specs.append(pl.BlockSpec((shape[0], ROW_TILE), lambda i: (0, i)))
        elif shape[0] == N_EXPERTS:
            out_specs.append(pl.BlockSpec((shape[0], LANES), lambda i: (0, i)))
        else:
            out_specs.append(pl.BlockSpec((ROW_TILE, shape[-1]), lambda i: (i, 0)))
    n_r, n_c, n_m, n_t = len(rows), len(consts), len(mods), len(tabs)

    def kern(*refs):
        r = refs[:n_r]
        c = refs[n_r:n_r + n_c]
        m = refs[n_r + n_c:n_r + n_c + n_m]
        t = refs[n_r + n_c + n_m:n_r + n_c + n_m + n_t]
        o = refs[n_r + n_c + n_m + n_t:]
        body(r, c, m, t, o)

    return pl.pallas_call(
        kern, grid=(n_tiles,), in_specs=specs, out_specs=out_specs, out_shape=out_shapes,
        compiler_params=_params(("parallel",)), name=name,
    )(*arrays)


def _residual_in(r, m, has_prev):
    if has_prev:
        return r[0][...] + m[0][...] * r[1][...]
    return r[0][...]


def _rope_pair(x, cos, sin_a, sin_b, shift):
    return x * cos + pltpu.roll(x, shift, 1) * sin_a + pltpu.roll(x, LANES - shift, 1) * sin_b


def _mla_pre_body(has_prev, r, c, m, t, o):
    g1, w_in, qg, kvg, wq, wk, wv, place, v_ones = c
    shift1, scale1 = m[-2], m[-1]
    cos_q, sa_q, sb_q, cos_k, sa_k, sb_k = [a[...] for a in t]
    x = _residual_in(r, m, has_prev)
    h = _norm_mod(x, g1[...], scale1[...], shift1[...])
    z = _mm(h, w_in[...])
    cq = _rms(z[:, :MLA_Q_RANK], qg[...]).astype(BF16)
    ckv = _rms(z[:, MLA_Q_RANK:MLA_Q_RANK + MLA_KV_RANK], kvg[...]).astype(BF16)
    kr = z[:, MLA_Q_RANK + MLA_KV_RANK:]
    kr = _rope_pair(kr, cos_k, sa_k, sb_k, MLA_ROPE // 2)
    scale = (MLA_NOPE + MLA_ROPE) ** -0.5 * LOG2_E
    q = _mm(cq, wq[...])
    for hh in range(MLA_HEADS):
        qs = q[:, hh * LANES:(hh + 1) * LANES]
        qs = _rope_pair(qs, cos_q, sa_q, sb_q, MLA_ROPE // 2)
        o[1][:, hh * LANES:(hh + 1) * LANES] = (qs * scale).astype(BF16)
    o[0][...] = x
    o[2][...] = (_mm(ckv, wk[...]) + _mm(kr, place[...])).astype(BF16)
    o[3][...] = (_mm(ckv, wv[...]) + v_ones[...]).astype(BF16)


def _gqa_pre_body(has_prev, r, c, m, t, o):
    g1, w_in, qg, kg = c
    shift1, scale1 = m[-2], m[-1]
    cos, sin = t[0][...], t[1][...]
    x = _residual_in(r, m, has_prev)
    h = _norm_mod(x, g1[...], scale1[...], shift1[...])
    z = _mm(h, w_in[...])
    scale = GQA_HEAD_DIM ** -0.5 * LOG2_E
    hd = GQA_HEAD_DIM
    for hh in range(GQA_Q_HEADS + GQA_KV_HEADS):
        zs = z[:, hh * hd:(hh + 1) * hd]
        is_q = hh < GQA_Q_HEADS
        zs = _rms(zs, qg[...] if is_q else kg[...])
        zs = zs * cos + pltpu.roll(zs, hd // 2, 1) * sin
        if is_q:
            o[1][:, hh * hd:(hh + 1) * hd] = (zs * scale).astype(BF16)
        else:
            kk = hh - GQA_Q_HEADS
            o[2][:, kk * hd:(kk + 1) * hd] = zs.astype(BF16)
    o[0][...] = x
    o[3][...] = z[:, (GQA_Q_HEADS + GQA_KV_HEADS) * hd:].astype(BF16)


def _mlstm_pre_body(has_prev, r, c, m, t, o):
    g1, w_in, gate_b, fmask = c
    shift1, scale1 = m[-2], m[-1]
    x = _residual_in(r, m, has_prev)
    h = _norm_mod(x, g1[...], scale1[...], shift1[...])
    z = _mm(h, w_in[...])
    nqk = MLSTM_HEADS * MLSTM_QK
    nv = MLSTM_HEADS * MLSTM_V
    o[0][...] = x
    o[1][...] = z[:, :nqk].astype(BF16)
    o[2][...] = (z[:, nqk:2 * nqk] * MLSTM_QK ** -0.5).astype(BF16)
    o[3][...] = z[:, 2 * nqk:2 * nqk + nv].astype(BF16)
    o[4][...] = z[:, 2 * nqk + nv:2 * nqk + 2 * nv].astype(BF16)
    gt = z[:, 2 * nqk + 2 * nv:] + gate_b[...]
    log_sig = jnp.minimum(gt, 0.0) - jnp.log(1.0 + jnp.exp(-jnp.abs(gt)))
    o[5][...] = jnp.where(fmask[...] > 0.0, log_sig, gt)


def _gdn_pre_body(has_prev, r, c, m, t, o):
    g1, w_in, w_ba, a_scale, dt_b, dmask = c
    shift1, scale1 = m[-2], m[-1]
    x = _residual_in(r, m, has_prev)
    h = _norm_mod(x, g1[...], scale1[...], shift1[...]).astype(BF16)
    ncv = 2 * GDN_QK_HEADS * GDN_HEAD_DIM + GDN_V_HEADS * GDN_HEAD_DIM
    z = _mm(h, w_in[...])
    o[0][...] = x
    o[1][...] = z[:, :ncv].astype(BF16)
    o[2][...] = z[:, ncv:].astype(BF16)
    ba = _mm(h, w_ba[...])
    o[3][...] = jnp.where(dmask[...] > 0.0, -a_scale[...] * _softplus(ba + dt_b[...]), _sigmoid(ba))


def _route_tail(x, y, c, m, o):
    g2, rw_t, rb = c
    gate1, shift2, scale2 = m
    x1 = x + gate1[...] * y
    h2 = _norm_mod(x1, g2[...], scale2[...], shift2[...])
    o[0][...] = x1
    o[1][...] = h2.astype(BF16)
    logits = lax.dot_general(rw_t[...], h2, (((1,), (1,)), ((), ())), preferred_element_type=F32,
                             precision=HIGHEST) + rb[...]
    e_iota = lax.broadcasted_iota(jnp.int32, logits.shape, 0)
    vals = []
    picked = jnp.zeros(logits.shape, F32)
    for k in range(TOP_K):
        mx = jnp.max(logits, axis=0, keepdims=True)
        idx = jnp.min(jnp.where(logits == mx, e_iota, N_EXPERTS), axis=0, keepdims=True)
        o[2][k:k + 1, :] = idx
        vals.append(mx)
        hit = e_iota == idx
        picked = picked + jnp.where(hit, 1.0, 0.0)
        logits = jnp.where(hit, -jnp.inf, logits)
    o[4][...] = jnp.broadcast_to(jnp.sum(picked, axis=1, keepdims=True), o[4].shape)
    es = [jnp.exp(v - vals[0]) for v in vals]
    tot = es[0] + es[1] + es[2] + es[3]
    for k in range(TOP_K):
        o[3][k:k + 1, :] = es[k] / tot


def _attn_post_body(r, c, m, t, o):
    x, a = r
    y = _mm(a[...], c[0][...])
    _route_tail(x[...], y, c[1:], m, o)


def _mlstm_post_body(r, c, m, t, o):
    x, hf, hb, og = r
    w_o, ng = c[0], c[1]
    hs = hf[...] + hb[...]
    gate = _sigmoid(og[...].astype(F32))
    parts = []
    for hh in range(MLSTM_HEADS):
        sl = slice(hh * MLSTM_V, (hh + 1) * MLSTM_V)
        parts.append((_rms(hs[:, sl], ng[...]) * gate[:, sl]).astype(BF16))
    y = _mm(jnp.concatenate(parts, axis=1), w_o[...])
    _route_tail(x[...], y, c[2:], m, o)


def _gdn_post_body(r, c, m, t, o):
    x, of, ob, zg = r
    w_o, ng = c[0], c[1]
    os_ = of[...] + ob[...]
    g = zg[...].astype(F32)
    gate = g * _sigmoid(g)
    parts = []
    for hh in range(GDN_V_HEADS):
        sl = slice(hh * GDN_HEAD_DIM, (hh + 1) * GDN_HEAD_DIM)
        parts.append((_rms(os_[:, sl], ng[...]) * gate[:, sl]).astype(BF16))
    y = _mm(jnp.concatenate(parts, axis=1), w_o[...])
    _route_tail(x[...], y, c[2:], m, o)


def _final_body(r, c, m, t, o):
    x = r[0][...] + m[0][...] * r[1][...]
    o[0][...] = _rms(x, c[0][...])


def _attn_kernel(*refs, n_seg, k_stride, v_width):
    q_ref = refs[0]
    segs = [(refs[1 + 2 * i], refs[2 + 2 * i]) for i in range(n_seg)]
    o_ref = refs[1 + 2 * n_seg]
    n_heads = 2 * ATTN_GROUPS_PER_STEP

    def scores(j):
        q = q_ref[:, j * LANES:(j + 1) * LANES]
        koff = j * LANES if k_stride else (j // 2) * LANES
        return [_mm_nt(q, k_ref[:, koff:koff + LANES]) for k_ref, _ in segs]

    ss_next = scores(0)
    for j in range(n_heads):
        ss = ss_next
        if j + 1 < n_heads:
            ss_next = scores(j + 1)
        mx = ss[0].max(axis=-1, keepdims=True)
        for s in ss[1:]:
            mx = jnp.maximum(mx, s.max(axis=-1, keepdims=True))
        acc, den = 0.0, 0.0
        for s, (_, v_ref) in zip(ss, segs):
            p = jnp.exp2(s - mx)
            if k_stride:
                acc = acc + _mm(p, v_ref[:, j * 2 * v_width:(j + 1) * 2 * v_width])
            else:
                den = den + p.sum(axis=-1, keepdims=True)
                acc = acc + _mm(p, v_ref[:, (j // 2) * v_width:(j // 2 + 1) * v_width])
        if k_stride:
            den = acc[:, v_width:v_width + 1]
            acc = acc[:, :v_width]
        o_ref[:, j * v_width:(j + 1) * v_width] = (acc / den).astype(o_ref.dtype)


def _attention(cfg, q, k, v, n_groups, k_stride, v_width, latent):
    gps = ATTN_GROUPS_PER_STEP
    kw = gps * (2 * LANES if k_stride else LANES)
    vw = gps * (4 * v_width if k_stride else v_width)
    qw = gps * 2 * LANES
    ow = gps * 2 * v_width
    ctx_blk0 = cfg.n_lat // cfg.nc
    k_ctx = pl.BlockSpec((cfg.nc, kw), lambda b, g, i: (ctx_blk0 + b, g))
    v_ctx = pl.BlockSpec((cfg.nc, vw), lambda b, g, i: (ctx_blk0 + b, g))
    if latent:
        qpb = cfg.s // Q_TILE
        q_spec = pl.BlockSpec((Q_TILE, qw), lambda b, g, i: (b * qpb + i, g))
        o_spec = pl.BlockSpec((Q_TILE, ow), lambda b, g, i: (b * qpb + i, g))
        in_specs = [q_spec, pl.BlockSpec((cfg.s, kw), lambda b, g, i: (b, g)),
                    pl.BlockSpec((cfg.s, vw), lambda b, g, i: (b, g)), k_ctx, v_ctx]
        args = (q, k, v, k, v)
        n_seg = 2
        rows = cfg.n_lat
    else:
        tq = min(Q_TILE, cfg.nc)
        qpb = cfg.nc // tq
        q0 = cfg.n_lat // tq
        q_spec = pl.BlockSpec((tq, qw), lambda b, g, i: (q0 + b * qpb + i, g))
        o_spec = pl.BlockSpec((tq, ow), lambda b, g, i: (b * qpb + i, g))
        in_specs = [q_spec, k_ctx, v_ctx]
        args = (q, k, v)
        n_seg = 1
        rows = cfg.b * cfg.nc
    grid = (cfg.b, n_groups // gps, qpb)
    return pl.pallas_call(
        functools.partial(_attn_kernel, n_seg=n_seg, k_stride=k_stride, v_width=v_width),
        grid=grid, in_specs=in_specs, out_specs=o_spec,
        out_shape=jax.ShapeDtypeStruct((rows, n_groups * 2 * v_width), BF16),
        compiler_params=_params(("parallel", "parallel", "arbitrary")),
        name="attention_lat" if latent else "attention_ctx",
    )(*args)


def _block_masks(flip):
    li = lax.broadcasted_iota(jnp.int32, (ROW_TILE, ROW_TILE), 0)
    si = lax.broadcasted_iota(jnp.int32, (ROW_TILE, ROW_TILE), 1)
    same = (li // CHUNK) == (si // CHUNK)
    eye = li == si
    incl = same & ((si >= li) if flip else (si <= li))
    strict = same & ((si > li) if flip else (si < li))
    return eye, same, incl, strict


def _scan_block_index(cfg, flip):
    cpb, tpb, lat_tiles = cfg.cpb, cfg.tpb, cfg.lat_tiles

    def row_block(b, j):
        if flip:
            ctx = lat_tiles + b * cpb + (cpb - 1 - j)
            lat = b * tpb + (tpb - 1 - (j - cpb))
        else:
            ctx = lat_tiles + b * cpb + j
            lat = b * tpb + (j - cpb)
        return jnp.where(j < cpb, ctx, lat)

    return row_block


def _mlstm_kernel(q_ref, k_ref, v_ref, g_ref, o_ref, c_st, n_st, m_st, *, flip, d):
    @pl.when(pl.program_id(1) == 0)
    def _():
        c_st[...] = jnp.zeros_like(c_st)
        n_st[...] = jnp.zeros_like(n_st)
        m_st[...] = jnp.zeros_like(m_st)

    eye, same, incl, _ = _block_masks(flip)
    n_chunks = ROW_TILE // CHUNK
    order = [n_chunks - 1 - ci if flip else ci for ci in range(n_chunks)]
    to_row = lambda col: jnp.sum(jnp.where(eye, col, 0.0), axis=0, keepdims=True)
    per_chunk_col = lambda vals: jnp.concatenate(
        [jnp.broadcast_to(vals[c], (CHUNK, 1)) for c in range(n_chunks)], axis=0)
    hs = range(MLSTM_HEADS)
    qs = [q_ref[:, hh * MLSTM_QK:(hh + 1) * MLSTM_QK] for hh in hs]
    ks = [k_ref[:, hh * MLSTM_QK:(hh + 1) * MLSTM_QK] for hh in hs]
    vs = [v_ref[:, hh * MLSTM_V:(hh + 1) * MLSTM_V] for hh in hs]
    qks = [_mm_nt(qs[hh], ks[hh]) for hh in hs]
    i_cols = [g_ref[:, d * 2 * MLSTM_HEADS + hh:d * 2 * MLSTM_HEADS + hh + 1] for hh in hs]
    f_cols = [g_ref[:, (d * 2 + 1) * MLSTM_HEADS + hh:(d * 2 + 1) * MLSTM_HEADS + hh + 1] for hh in hs]
    f_rows = [to_row(f) for f in f_cols]
    i_rows = [to_row(i) for i in i_cols]
    bcum_cols = [jnp.sum(jnp.where(incl, f, 0.0), axis=1, keepdims=True) for f in f_rows]
    bcum_rows = [to_row(b) for b in bcum_cols]
    bend_cols = [jnp.sum(jnp.where(same, f, 0.0), axis=1, keepdims=True) for f in f_rows]
    w_logs = [bend_cols[hh] - bcum_cols[hh] + i_cols[hh] for hh in hs]
    wmax_cols = [jnp.max(jnp.where(same, to_row(w), -jnp.inf), axis=1, keepdims=True) for w in w_logs]
    m_prevs = [[None] * n_chunks for _ in hs]
    m_news = [[None] * n_chunks for _ in hs]
    m_runs = [m_st[hh] for hh in hs]
    for cc in order:
        r0 = cc * CHUNK
        for hh in hs:
            m_prevs[hh][cc] = m_runs[hh]
            m_runs[hh] = jnp.maximum(bend_cols[hh][r0:r0 + 1] + m_runs[hh], wmax_cols[hh][r0:r0 + 1])
            m_news[hh][cc] = m_runs[hh]
    d_logs = [jnp.where(incl, bcum_cols[hh] - bcum_rows[hh] + i_rows[hh], -jnp.inf) for hh in hs]
    g_logs = [bcum_cols[hh] + per_chunk_col(m_prevs[hh]) for hh in hs]
    m_ts = [jnp.maximum(g_logs[hh], jnp.max(d_logs[hh], axis=1, keepdims=True)) for hh in hs]
    ss = [qks[hh] * jnp.exp(d_logs[hh] - m_ts[hh]) for hh in hs]
    svs = [_mm(ss[hh], vs[hh]) for hh in hs]
    kws = [ks[hh].astype(F32) * jnp.exp(w_logs[hh] - per_chunk_col(m_news[hh])) for hh in hs]
    prep = []
    for hh in hs:
        carries = [jnp.exp(bend_cols[hh][cc * CHUNK:cc * CHUNK + 1] + m_prevs[hh][cc] - m_news[hh][cc])
                   for cc in range(n_chunks)]
        kvs = [_mm(kws[hh][cc * CHUNK:(cc + 1) * CHUNK].T, vs[hh][cc * CHUNK:(cc + 1) * CHUNK])
               for cc in range(n_chunks)]
        ksum = [jnp.sum(kws[hh][cc * CHUNK:(cc + 1) * CHUNK], axis=0, keepdims=True) for cc in range(n_chunks)]
        prep.append((qs[hh], jnp.exp(g_logs[hh] - m_ts[hh]), svs[hh], jnp.sum(ss[hh], axis=1, keepdims=True),
                     jnp.exp(-m_ts[hh]), kvs, ksum, carries, m_runs[hh]))
    for cc in order:
        rows = slice(cc * CHUNK, (cc + 1) * CHUNK)
        for hh in range(MLSTM_HEADS):
            q, inter, sv, ssum, floor, kvs, ksum, carries, _ = prep[hh]
            c_prev, n_prev = c_st[hh], n_st[hh]
            qc = q[rows]
            num = inter[rows] * _mm(qc, c_prev) + sv[rows]
            den = inter[rows] * jnp.sum(qc.astype(F32) * n_prev, axis=1, keepdims=True) + ssum[rows]
            o_ref[rows, hh * MLSTM_V:(hh + 1) * MLSTM_V] = num / jnp.maximum(jnp.abs(den), floor[rows])
            c_st[hh] = carries[cc] * c_prev + kvs[cc]
            n_st[hh] = carries[cc] * n_prev + ksum[cc]
    for hh in range(MLSTM_HEADS):
        m_st[hh] = prep[hh][-1]


def _mlstm_scan(cfg, q, k, v, gates):
    nqk, nv = MLSTM_HEADS * MLSTM_QK, MLSTM_HEADS * MLSTM_V
    outs = []
    for d in range(2):
        flip = d == 1
        rb = _scan_block_index(cfg, flip)
        row = lambda w, rb=rb: pl.BlockSpec((ROW_TILE, w), lambda b, j: (rb(b, j), 0))
        outs.append(pl.pallas_call(
            functools.partial(_mlstm_kernel, flip=flip, d=d),
            grid=(cfg.b, cfg.cpb + cfg.tpb),
            in_specs=[row(nqk), row(nqk), row(nv), row(LANES)],
            out_specs=row(nv),
            out_shape=jax.ShapeDtypeStruct((cfg.m, nv), F32),
            scratch_shapes=[pltpu.VMEM((MLSTM_HEADS, MLSTM_QK, MLSTM_V), F32),
                            pltpu.VMEM((MLSTM_HEADS, 1, MLSTM_QK), F32),
                            pltpu.VMEM((MLSTM_HEADS, 1, 1), F32)],
            compiler_params=_params(("parallel", "arbitrary")),
            name="mlstm_bwd" if flip else "mlstm_fwd",
        )(q, k, v, gates))
    return outs


def _gdn_kernel(q_ref, k_ref, v_ref, ba_ref, o_ref, s_st, *, flip, d):
    @pl.when(pl.program_id(2) == 0)
    def _():
        s_st[...] = jnp.zeros_like(s_st)

    eye, same, incl, strict = _block_masks(flip)
    eye_f = jnp.where(eye, 1.0, 0.0)
    rep = GDN_V_HEADS // GDN_QK_HEADS
    hd = GDN_HEAD_DIM
    n_chunks = ROW_TILE // CHUNK
    heads = [(qi, jj) for qi in range(GDN_QK_PER_STEP) for jj in range(rep)]
    ks = [k_ref[:, qi * hd:(qi + 1) * hd] for qi in range(GDN_QK_PER_STEP)]
    kfs = [k.astype(F32) for k in ks]
    qfs = [q_ref[:, qi * hd:(qi + 1) * hd].astype(F32) for qi in range(GDN_QK_PER_STEP)]
    qks = [_mm_nt(q_ref[:, qi * hd:(qi + 1) * hd], ks[qi]) for qi in range(GDN_QK_PER_STEP)]
    betas, gcs, gces, decays, kbs = [], [], [], [], []
    for qi, jj in heads:
        bcol = qi * LANES + d * 2 * rep + jj
        beta = ba_ref[:, bcol:bcol + 1]
        ld = ba_ref[:, bcol + rep:bcol + rep + 1]
        ld_row = jnp.sum(jnp.where(eye, ld, 0.0), axis=0, keepdims=True)
        gc_col = jnp.sum(jnp.where(incl, ld_row, 0.0), axis=1, keepdims=True)
        gc_row = jnp.sum(jnp.where(eye, gc_col, 0.0), axis=0, keepdims=True)
        gces.append(jnp.sum(jnp.where(same, ld_row, 0.0), axis=1, keepdims=True))
        decays.append(jnp.exp(jnp.where(incl, gc_col - gc_row, -jnp.inf)))
        betas.append(beta)
        gcs.append(gc_col)
        kbs.append(kfs[qi] * beta)
    def pack(bd):
        out = bd[0:CHUNK]
        for c in range(1, n_chunks):
            out = out + bd[c * CHUNK:(c + 1) * CHUNK]
        return out

    def block_diag(packed):
        return jnp.where(same, jnp.concatenate([packed] * n_chunks, axis=0), 0.0)

    pws = [pack(jnp.where(strict, _mm_nt(kbs[i], ks[qi]) * decays[i], 0.0)) for i, (qi, _) in enumerate(heads)]
    eye_p = pack(eye_f)
    invs = [eye_p - p for p in pws]
    bds = [block_diag(p) for p in pws]
    for _ in range(int(math.log2(CHUNK)) - 1):
        pws = [_mm(p, bd) for p, bd in zip(pws, bds)]
        bds = [block_diag(p) for p in pws]
        invs = [iv + _mm(iv, bd) for iv, bd in zip(invs, bds)]
    invs = [block_diag(iv) for iv in invs]
    egs = [jnp.exp(g) for g in gcs]
    uws = []
    for i, (qi, jj) in enumerate(heads):
        v = v_ref[:, (qi * rep + jj) * hd:(qi * rep + jj + 1) * hd].astype(F32)
        uws.append(_mm(invs[i], jnp.concatenate([v * betas[i], kbs[i] * egs[i]], axis=1)))
    aws = [_mm(qks[qi] * decays[i], uws[i]) for i, (qi, _) in enumerate(heads)]
    qes = [qfs[qi] * egs[i] - aws[i][:, hd:] for i, (qi, _) in enumerate(heads)]
    kdecs = [kfs[qi] * jnp.exp(gces[i] - gcs[i]) for i, (qi, _) in enumerate(heads)]
    for ci in range(n_chunks):
        cc = n_chunks - 1 - ci if flip else ci
        rows = slice(cc * CHUNK, (cc + 1) * CHUNK)
        bgs = [_mm(kdecs[i][rows].T, uws[i][rows]) for i in range(len(heads))]
        for i, (qi, jj) in enumerate(heads):
            s_prev = s_st[i]
            col = (qi * rep + jj) * hd
            o_ref[rows, col:col + hd] = _mm(qes[i][rows], s_prev) + aws[i][rows, :hd]
            g_end = jnp.exp(gces[i][cc * CHUNK:cc * CHUNK + 1])
            s_st[i] = g_end * s_prev - _mm(bgs[i][:, hd:], s_prev) + bgs[i][:, :hd]


def _gdn_scan(cfg, qkv, ba):
    rep = GDN_V_HEADS // GDN_QK_HEADS
    hd = GDN_HEAD_DIM
    g = GDN_QK_PER_STEP
    outs = []
    for d in range(2):
        flip = d == 1
        rb = _scan_block_index(cfg, flip)
        col = lambda w, off, rb=rb: pl.BlockSpec((ROW_TILE, w), lambda b, h, j: (rb(b, j), off + h))
        outs.append(pl.pallas_call(
            functools.partial(_gdn_kernel, flip=flip, d=d),
            grid=(cfg.b, GDN_QK_HEADS // g, cfg.cpb + cfg.tpb),
            in_specs=[col(g * hd, 0), col(g * hd, GDN_QK_HEADS // g),
                      col(g * rep * hd, 2 * GDN_QK_HEADS // (g * rep)), col(g * LANES, 0)],
            out_specs=col(g * rep * hd, 0),
            out_shape=jax.ShapeDtypeStruct((cfg.m, GDN_V_HEADS * hd), F32),
            scratch_shapes=[pltpu.VMEM((g * rep, hd, hd), F32)],
            compiler_params=_params(("parallel", "parallel", "arbitrary")),
            name="gdn_bwd" if flip else "gdn_fwd",
        )(qkv, qkv, qkv, ba))
    return outs


def _gdn_conv_kernel(cur_ref, prev_ref, next_ref, w_ref, o_ref, pad_ref, *, lat_tiles, tpb, cpb):
    i = pl.program_id(0)
    cb = pl.program_id(1)
    lat = i < lat_tiles
    pos = jnp.where(lat, i % tpb, (i - lat_tiles) % cpb)
    per = jnp.where(lat, tpb, cpb)
    first = pos == 0
    last = pos == per - 1
    prev = prev_ref[...].astype(F32)
    nxt = next_ref[...].astype(F32)
    pad_ref[0:HALO, :] = jnp.where(first, 0.0, prev)
    pad_ref[HALO:HALO + ROW_TILE, :] = cur_ref[...].astype(F32)
    pad_ref[HALO + ROW_TILE:, :] = jnp.where(last, 0.0, nxt)
    half = GDN_CONV // 2
    acc = 0.0
    for tap in range(GDN_CONV):
        off = HALO - half + tap
        acc = acc + pad_ref[off:off + ROW_TILE, :] * w_ref[tap:tap + 1, :]
    act = acc * _sigmoid(acc)
    qscale = jnp.where(cb == 0, GDN_HEAD_DIM ** -0.5, 1.0)
    hd = GDN_HEAD_DIM
    for hh in range(act.shape[1] // hd):
        a = act[:, hh * hd:(hh + 1) * hd]
        nrm = a * lax.rsqrt(jnp.sum(a * a, axis=-1, keepdims=True) + NORM_EPS) * qscale
        o_ref[:, hh * hd:(hh + 1) * hd] = jnp.where(cb < 2, nrm, a).astype(o_ref.dtype)


def _gdn_conv(cfg, zqkv, conv_w):
    cw = GDN_QK_HEADS * GDN_HEAD_DIM
    n_cb = zqkv.shape[1] // cw
    hpt = ROW_TILE // HALO
    n_halo = cfg.m // HALO
    return pl.pallas_call(
        functools.partial(_gdn_conv_kernel, lat_tiles=cfg.lat_tiles, tpb=cfg.tpb, cpb=cfg.cpb),
        grid=(cfg.tiles, n_cb),
        in_specs=[pl.BlockSpec((ROW_TILE, cw), lambda i, c: (i, c)),
                  pl.BlockSpec((HALO, cw), lambda i, c: (jnp.maximum(i * hpt - 1, 0), c)),
                  pl.BlockSpec((HALO, cw), lambda i, c: (jnp.minimum((i + 1) * hpt, n_halo - 1), c)),
                  pl.BlockSpec((8, cw), lambda i, c: (0, c))],
        out_specs=pl.BlockSpec((ROW_TILE, cw), lambda i, c: (i, c)),
        out_shape=jax.ShapeDtypeStruct(zqkv.shape, BF16),
        scratch_shapes=[pltpu.VMEM((ROW_TILE + 2 * HALO, cw), F32)],
        compiler_params=_params(("parallel", "parallel")),
        name="gdn_conv",
    )(zqkv, zqkv, zqkv, conv_w)


def _rank_kernel(e_ref, start_ref, slot_ref, run_ref):
    @pl.when(pl.program_id(0) == 0)
    def _():
        run_ref[...] = start_ref[...]

    e = e_ref[...]
    onehot = (lax.broadcasted_iota(jnp.int32, (N_EXPERTS, RANK_TILE), 0) == e)
    oh = jnp.where(onehot, 1.0, 0.0)
    ji = lax.broadcasted_iota(jnp.int32, (RANK_TILE, RANK_TILE), 0)
    si = lax.broadcasted_iota(jnp.int32, (RANK_TILE, RANK_TILE), 1)
    upper = jnp.where(ji <= si, 1.0, 0.0).astype(BF16)
    cum = jnp.dot(oh.astype(BF16), upper, preferred_element_type=F32)
    run = run_ref[:, 0:1]
    slot = jnp.sum(oh * (cum - 1.0 + run), axis=0, keepdims=True)
    slot_ref[...] = slot.astype(jnp.int32)
    run_ref[...] = run_ref[...] + jnp.sum(oh, axis=1, keepdims=True)


def _slot_pairs(expert_of_pair, group_start):
    n_pairs = expert_of_pair.shape[1]
    return pl.pallas_call(
        _rank_kernel,
        grid=(n_pairs // RANK_TILE,),
        in_specs=[pl.BlockSpec((1, RANK_TILE), lambda i: (0, i)),
                  pl.BlockSpec((N_EXPERTS, LANES), lambda i: (0, 0))],
        out_specs=pl.BlockSpec((1, RANK_TILE), lambda i: (0, i)),
        out_shape=jax.ShapeDtypeStruct((1, n_pairs), jnp.int32),
        scratch_shapes=[pltpu.VMEM((N_EXPERTS, LANES), F32)],
        compiler_params=_params(("arbitrary",)),
        name="moe_rank",
    )(expert_of_pair, group_start)


def _moe_kernel(te_ref, nu_ref, x_ref, w1_ref, b1g_ref, b1l_ref, w2_ref, b2_ref, o_ref, w1g_s, w1l_s, w2_s):
    j = pl.program_id(0)
    used = j < nu_ref[0]
    new_expert = jnp.logical_or(j == 0, te_ref[j] != te_ref[jnp.maximum(j - 1, 0)])

    @pl.when(jnp.logical_and(used, new_expert))
    def _():
        slab = 2 * LANES
        src = lax.broadcasted_iota(jnp.int32, (slab, slab), 0)
        dst = lax.broadcasted_iota(jnp.int32, (slab, slab), 1)
        want = jnp.where(dst < LANES, 2 * dst, 2 * (dst - LANES) + 1)
        perm = jnp.where(src == want, 1.0, 0.0).astype(BF16)
        for sb in range(w1_ref.shape[1] // slab):
            sorted_cols = jnp.dot(w1_ref[:, sb * slab:(sb + 1) * slab].astype(BF16), perm,
                                  preferred_element_type=F32)
            w1g_s[:, sb * LANES:(sb + 1) * LANES] = sorted_cols[:, :LANES].astype(BF16)
            w1l_s[:, sb * LANES:(sb + 1) * LANES] = sorted_cols[:, LANES:].astype(BF16)
        w2_s[...] = w2_ref[...].astype(BF16)

    @pl.when(used)
    def _():
        x = x_ref[...]
        glu = jnp.minimum(jnp.dot(x, w1g_s[...], preferred_element_type=F32) + b1g_ref[...], SWIGLU_LIMIT)
        lin = jnp.clip(jnp.dot(x, w1l_s[...], preferred_element_type=F32) + b1l_ref[...],
                       -SWIGLU_LIMIT, SWIGLU_LIMIT)
        act = glu * _sigmoid(SWIGLU_ALPHA * glu) * (lin + 1.0)
        y = jnp.dot(act.astype(BF16), w2_s[...], preferred_element_type=F32) + b2_ref[...]
        o_ref[...] = y.astype(o_ref.dtype)

    @pl.when(jnp.logical_not(used))
    def _():
        o_ref[...] = jnp.zeros_like(o_ref)


def _moe_grouped(xs, tile_expert, n_used, layer, w1_all, b1g, b1l, w2_all, b2):
    n_slots, d = xs.shape
    de = w2_all.shape[2]
    wspec = lambda r, c: pl.BlockSpec((None, r, c), lambda j, te, nu: (te[j], 0, 0))
    lspec = lambda r, c: pl.BlockSpec((None, None, r, c), lambda j, te, nu: (layer, te[j], 0, 0))
    return pl.pallas_call(
        _moe_kernel,
        grid_spec=pltpu.PrefetchScalarGridSpec(
            num_scalar_prefetch=2,
            grid=(n_slots // GROUP_TILE,),
            in_specs=[pl.BlockSpec((GROUP_TILE, d), lambda j, te, nu: (j, 0)),
                      lspec(d, 2 * de), wspec(1, de), wspec(1, de), lspec(de, d), wspec(1, d)],
            out_specs=pl.BlockSpec((GROUP_TILE, d), lambda j, te, nu: (j, 0)),
            scratch_shapes=[pltpu.VMEM((d, de), BF16), pltpu.VMEM((d, de), BF16), pltpu.VMEM((de, d), BF16)]),
        out_shape=jax.ShapeDtypeStruct((n_slots, d), BF16),
        compiler_params=_params(("arbitrary",)),
        name="moe_grouped",
    )(tile_expert, n_used, xs, w1_all, b1g, b1l, w2_all, b2)


def _moe_ffn(h2, topi, topw, tile_counts, layer, w1_all, b1, w2_all, b2):
    n_tok = h2.shape[0]
    n_pairs = TOP_K * n_tok
    assert n_pairs % RANK_TILE == 0
    n_slots = -(-(n_pairs + N_EXPERTS * (GROUP_TILE - 1)) // GROUP_TILE) * GROUP_TILE
    n_tiles = n_slots // GROUP_TILE

    counts = jnp.sum(tile_counts.reshape(N_EXPERTS, -1, LANES)[:, :, 0], axis=1).astype(jnp.int32)
    padded = (counts + GROUP_TILE - 1) // GROUP_TILE * GROUP_TILE
    ends = jnp.cumsum(padded)
    starts = ends - padded
    group_start = jnp.broadcast_to(starts.astype(F32)[:, None], (N_EXPERTS, LANES))
    slot = _slot_pairs(topi.reshape(1, n_pairs), group_start)[0]
    token = jnp.tile(jnp.arange(n_tok, dtype=jnp.int32), TOP_K)
    src = (jnp.arange(n_slots, dtype=jnp.int32) % n_tok).at[slot].set(
        token, unique_indices=True, mode="promise_in_bounds")
    tile_start = jnp.arange(n_tiles, dtype=jnp.int32) * GROUP_TILE
    tile_expert = jnp.sum((ends[None, :] <= tile_start[:, None]).astype(jnp.int32), axis=1)
    tile_expert = jnp.minimum(tile_expert, N_EXPERTS - 1)
    n_used = (ends[-1] // GROUP_TILE).astype(jnp.int32).reshape(1)

    xs = jnp.take(h2, src, axis=0, mode="clip")
    b1g = b1[:, None, 0::2]
    b1l = b1[:, None, 1::2]
    ys = _moe_grouped(xs, tile_expert, n_used, layer, w1_all, b1g, b1l, w2_all, b2[:, None, :])

    pos = slot.reshape(TOP_K, n_tok)
    y = jnp.zeros((n_tok, h2.shape[1]), F32)
    for k in range(TOP_K):
        y = y + topw[k][:, None] * jnp.take(ys, pos[k], axis=0, mode="clip").astype(F32)
    return y


def _axial_angles(n_tokens, rot_dim):
    t = jnp.arange(n_tokens, dtype=jnp.int32)
    rows = (t // GRID_W).astype(F32)
    cols = (t % GRID_W).astype(F32)
    d_axis = rot_dim // 2
    inv_freq = ROPE_BASE ** (-jnp.arange(0, d_axis, 2, dtype=F32) / d_axis)
    ang = jnp.concatenate([rows[:, None] * inv_freq, cols[:, None] * inv_freq], axis=-1)
    return jnp.cos(ang), jnp.sin(ang)


def _seq_table(cfg, lat_rows, ctx_value):
    c = lat_rows.shape[-1]
    ctx = jnp.full((1, ROW_TILE, c), ctx_value, F32)
    return jnp.concatenate([ctx, lat_rows.reshape(cfg.tpb, ROW_TILE, c)], axis=0)


def _slab_rope_tables(cfg, rot_dim, lane0):
    cos, sin = _axial_angles(cfg.s, rot_dim)
    half = rot_dim // 2
    s = cfg.s
    ones = lambda n: jnp.ones((s, n), F32)
    zeros = lambda n: jnp.zeros((s, n), F32)
    tail = LANES - lane0 - rot_dim
    cos_t = jnp.concatenate([ones(lane0), cos, cos, ones(tail)], axis=1)
    sin_a = jnp.concatenate([zeros(lane0 + half), sin, zeros(tail)], axis=1)
    sin_b = jnp.concatenate([zeros(lane0), -sin, zeros(half + tail)], axis=1)
    return _seq_table(cfg, cos_t, 1.0), _seq_table(cfg, sin_a, 0.0), _seq_table(cfg, sin_b, 0.0)


def _mod_slices(table, n_rows):
    d = D_MODEL
    return [table[:, k * d:(k + 1) * d].reshape(n_rows, 1, d) for k in range(N_MOD)]


def kernel(x, c, ctx, c_ctx, ada_w, ada_b, norm1_g, norm2_g, router_w, router_b, moe_w1, moe_b1, moe_w2, moe_b2,
           mla_w_in, mla_q_norm_g, mla_kv_norm_g, mla_w_uq, mla_w_ukv, mla_w_o, gqa_w_in, gqa_q_norm_g,
           gqa_k_norm_g, gqa_w_o, mlstm_w_in, mlstm_gate_b, mlstm_norm_g, mlstm_w_o, gdn_w_in, gdn_conv_w,
           gdn_a_log, gdn_dt_bias, gdn_norm_g, gdn_w_o, final_norm_g):
    b, s, d = x.shape
    nc = ctx.shape[1]
    depth = ada_w.shape[0]
    cfg = _Cfg(b, s, nc)
    m = cfg.m
    row = lambda a: a.reshape(1, -1).astype(F32)

    cond = jnp.concatenate([c_ctx[None, :], c, jnp.zeros((MOD_ROWS - 1 - b, d), F32)], axis=0)
    mod_all = _ada_table(cond, ada_w, ada_b)

    stream = jnp.concatenate([x.reshape(b * s, d), ctx.reshape(b * nc, d)], axis=0)
    y_prev = None
    gate2_prev = None
    for i in range(depth):
        kind, j = i % 4, i // 4
        shift1, scale1, gate1, shift2, scale2, gate2 = _mod_slices(mod_all[i], MOD_ROWS)
        has_prev = y_prev is not None
        rows_in = [stream, y_prev] if has_prev else [stream]
        mods_in = ([gate2_prev] if has_prev else []) + [shift1, scale1]
        g1 = row(norm1_g[i])
        route_consts = [row(norm2_g[i]), router_w[i].T, router_b[i].reshape(N_EXPERTS, 1)]
        route_mods = [gate1, shift2, scale2]
        route_outs = [((m, d), F32), ((m, d), BF16), ((TOP_K, m), jnp.int32), ((TOP_K, m), F32),
                      ((N_EXPERTS, cfg.tiles * LANES), F32)]

        if kind == 0:
            qk = MLA_NOPE + MLA_ROPE
            w_in = jnp.pad(mla_w_in[j], ((0, 0), (0, LANES - MLA_ROPE))).astype(BF16)
            wq = jnp.pad(mla_w_uq[j].reshape(MLA_Q_RANK, MLA_HEADS, qk), ((0, 0), (0, 0), (0, LANES - qk)))
            wq = wq.reshape(MLA_Q_RANK, MLA_HEADS * LANES).astype(BF16)
            wkv = mla_w_ukv[j].reshape(MLA_KV_RANK, MLA_HEADS, MLA_NOPE + MLA_V)
            wk = jnp.pad(wkv[:, :, :MLA_NOPE], ((0, 0), (0, 0), (0, LANES - MLA_NOPE)))
            wk = wk.reshape(MLA_KV_RANK, MLA_HEADS * LANES).astype(BF16)
            wv = jnp.pad(wkv[:, :, MLA_NOPE:], ((0, 0), (0, 0), (0, MLA_V)))
            wv = wv.reshape(MLA_KV_RANK, MLA_HEADS * 2 * MLA_V).astype(BF16)
            v_ones = jnp.tile(jnp.concatenate([jnp.zeros((MLA_V,), F32), jnp.ones((MLA_V,), F32)]), MLA_HEADS)
            v_ones = v_ones.reshape(1, MLA_HEADS * 2 * MLA_V)
            place = jnp.zeros((LANES, LANES), F32).at[jnp.arange(MLA_ROPE), MLA_NOPE + jnp.arange(MLA_ROPE)].set(1.0)
            place = jnp.tile(place, (1, MLA_HEADS)).astype(BF16)
            tabs = list(_slab_rope_tables(cfg, MLA_ROPE, MLA_NOPE)) + list(_slab_rope_tables(cfg, MLA_ROPE, 0))
            stream, q, k, v = _row_call(
                functools.partial(_mla_pre_body, has_prev), cfg, rows_in,
                [g1, w_in, row(mla_q_norm_g[j]), row(mla_kv_norm_g[j]), wq, wk, wv, place, v_ones], mods_in, tabs,
                [((m, d), F32), ((m, MLA_HEADS * LANES), BF16), ((m, MLA_HEADS * LANES), BF16),
                 ((m, MLA_HEADS * 2 * MLA_V), BF16)], name="mla_pre")
            a_lat = _attention(cfg, q, k, v, MLA_HEADS // 2, LANES, MLA_V, True)
            a_ctx = _attention(cfg, q, k, v, MLA_HEADS // 2, LANES, MLA_V, False)
            a = jnp.concatenate([a_lat, a_ctx], axis=0)
            stream, h2, topi, topw, tile_counts = _row_call(
                _attn_post_body, cfg, [stream, a], [mla_w_o[j].astype(BF16)] + route_consts, route_mods, [],
                route_outs, name="mla_post")
        elif kind == 1:
            cos, sin = _axial_angles(cfg.s, GQA_HEAD_DIM)
            tabs = [_seq_table(cfg, jnp.concatenate([cos, cos], axis=1), 1.0),
                    _seq_table(cfg, jnp.concatenate([-sin, sin], axis=1), 0.0)]
            stream, q, k, v = _row_call(
                functools.partial(_gqa_pre_body, has_prev), cfg, rows_in,
                [g1, gqa_w_in[j].astype(BF16), row(gqa_q_norm_g[j]), row(gqa_k_norm_g[j])], mods_in, tabs,
                [((m, d), F32), ((m, GQA_Q_HEADS * GQA_HEAD_DIM), BF16), ((m, GQA_KV_HEADS * GQA_HEAD_DIM), BF16),
                 ((m, GQA_KV_HEADS * GQA_HEAD_DIM), BF16)], name="gqa_pre")
            a_lat = _attention(cfg, q, k, v, GQA_KV_HEADS, 0, GQA_HEAD_DIM, True)
            a_ctx = _attention(cfg, q, k, v, GQA_KV_HEADS, 0, GQA_HEAD_DIM, False)
            a = jnp.concatenate([a_lat, a_ctx], axis=0)
            stream, h2, topi, topw, tile_counts = _row_call(
                _attn_post_body, cfg, [stream, a], [gqa_w_o[j].astype(BF16)] + route_consts, route_mods, [],
                route_outs, name="gqa_post")
        elif kind == 2:
            n_gate = 4 * MLSTM_HEADS
            w_in = jnp.pad(mlstm_w_in[j], ((0, 0), (0, LANES - n_gate))).astype(BF16)
            gate_b = jnp.pad(mlstm_gate_b[j].reshape(1, n_gate).astype(F32), ((0, 0), (0, LANES - n_gate)))
            col = jnp.arange(LANES)
            fmask = (((col // MLSTM_HEADS) % 2 == 1) & (col < n_gate)).astype(F32).reshape(1, LANES)
            nqk, nv = MLSTM_HEADS * MLSTM_QK, MLSTM_HEADS * MLSTM_V
            stream, q, k, v, og, gates = _row_call(
                functools.partial(_mlstm_pre_body, has_prev), cfg, rows_in, [g1, w_in, gate_b, fmask], mods_in, [],
                [((m, d), F32), ((m, nqk), BF16), ((m, nqk), BF16), ((m, nv), BF16), ((m, nv), BF16),
                 ((m, LANES), F32)], name="mlstm_pre")
            hf, hb = _mlstm_scan(cfg, q, k, v, gates)
            stream, h2, topi, topw, tile_counts = _row_call(
                _mlstm_post_body, cfg, [stream, hf, hb, og],
                [mlstm_w_o[j].astype(BF16), row(mlstm_norm_g[j])] + route_consts, route_mods, [], route_outs,
                name="mlstm_post")
        else:
            rep = GDN_V_HEADS // GDN_QK_HEADS
            qk_w = GDN_QK_HEADS * GDN_HEAD_DIM
            v_w = GDN_V_HEADS * GDN_HEAD_DIM
            ncv = 2 * qk_w + v_w
            w_main = gdn_w_in[j][:, :ncv + v_w].astype(BF16)
            w_ba = gdn_w_in[j][:, ncv + v_w:].reshape(d, 2, 2, GDN_QK_HEADS, rep).transpose(0, 3, 1, 2, 4)
            w_ba = jnp.pad(w_ba.reshape(d, GDN_QK_HEADS, 4 * rep), ((0, 0), (0, 0), (0, LANES - 4 * rep)))
            w_ba = w_ba.reshape(d, GDN_QK_HEADS * LANES).astype(BF16)

            def per_lane(p, fill):
                pv = p.astype(F32).reshape(2, GDN_QK_HEADS, rep).transpose(1, 0, 2)
                full = jnp.full((GDN_QK_HEADS, 2, 2, rep), fill, F32).at[:, :, 1, :].set(pv)
                full = jnp.pad(full.reshape(GDN_QK_HEADS, 4 * rep), ((0, 0), (0, LANES - 4 * rep)),
                               constant_values=fill)
                return full.reshape(1, GDN_QK_HEADS * LANES)

            a_scale = per_lane(jnp.exp(gdn_a_log[j].astype(F32)), 0.0)
            dt_b = per_lane(gdn_dt_bias[j], 0.0)
            dmask = per_lane(jnp.ones((2, GDN_V_HEADS), F32), 0.0)
            stream, zqkv, zg, ba = _row_call(
                functools.partial(_gdn_pre_body, has_prev), cfg, rows_in, [g1, w_main, w_ba, a_scale, dt_b, dmask],
                mods_in, [], [((m, d), F32), ((m, ncv), BF16), ((m, v_w), BF16), ((m, GDN_QK_HEADS * LANES), F32)],
                name="gdn_pre")
            conv_w = jnp.pad(gdn_conv_w[j].astype(F32), ((0, 8 - GDN_CONV), (0, 0)))
            qkv = _gdn_conv(cfg, zqkv, conv_w)
            of, ob = _gdn_scan(cfg, qkv, ba)
            stream, h2, topi, topw, tile_counts = _row_call(
                _gdn_post_body, cfg, [stream, of, ob, zg],
                [gdn_w_o[j].astype(BF16), row(gdn_norm_g[j])] + route_consts, route_mods, [], route_outs,
                name="gdn_post")

        y_prev = _moe_ffn(h2, topi, topw, tile_counts, i, moe_w1, moe_b1[i], moe_w2, moe_b2[i])
        gate2_prev = gate2

    out = _row_call(_final_body, cfg, [stream, y_prev], [row(final_norm_g)], [gate2_prev], [],
                    [((cfg.n_lat, d), F32)], n_tiles=cfg.lat_tiles, name="final_norm")[0]
    return out.reshape(b, s, d)
```

```python
import functools
import math

import jax
import jax.numpy as jnp
from jax import lax
from jax.experimental import pallas as pl
from jax.experimental.pallas import tpu as pltpu

F32 = jnp.float32
BF16 = jnp.bfloat16
HIGHEST = lax.Precision.HIGHEST

D_MODEL = 1024
N_MOD = 6
GRID_W = 64
ROPE_BASE = 10000.0
NORM_EPS = 1e-6
LOG2_E = math.log2(math.e)

MLA_HEADS = 16
MLA_Q_RANK = 512
MLA_KV_RANK = 256
MLA_NOPE = 64
MLA_ROPE = 32
MLA_V = 64

GQA_Q_HEADS = 8
GQA_KV_HEADS = 4
GQA_HEAD_DIM = 128

MLSTM_HEADS = 4
MLSTM_QK = 128
MLSTM_V = 256

GDN_QK_HEADS = 8
GDN_V_HEADS = 16
GDN_HEAD_DIM = 128
GDN_CONV = 5
GDN_QK_PER_STEP = 2

N_EXPERTS = 32
TOP_K = 4
D_EXPERT = 1024
SWIGLU_ALPHA = 1.702
SWIGLU_LIMIT = 7.0

LANES = 128
ROW_TILE = 256
CHUNK = 64
Q_TILE = 512
ATTN_GROUPS_PER_STEP = 2
GROUP_TILE = 512
RANK_TILE = 512
RANK_TILES_PER_STEP = 4
MOD_ROWS = 16
HALO = 16
VMEM_LIMIT = 56 * 1024 * 1024


def _params(sem):
    return pltpu.CompilerParams(dimension_semantics=sem, vmem_limit_bytes=VMEM_LIMIT)


class _Cfg:
    def __init__(self, b, s, nc):
        assert s % ROW_TILE == 0 and nc % ROW_TILE == 0 and (b * s) % nc == 0 and b + 1 <= MOD_ROWS
        self.b, self.s, self.nc = b, s, nc
        self.n_lat = b * s
        self.m = b * s + b * nc
        self.lat_tiles = self.n_lat // ROW_TILE
        self.tiles = self.m // ROW_TILE
        self.tpb = s // ROW_TILE
        self.cpb = nc // ROW_TILE


def _rms(x, g):
    return x * lax.rsqrt(jnp.mean(x * x, axis=-1, keepdims=True) + NORM_EPS) * g


def _norm_mod(x, g, scale, shift):
    return _rms(x, g) * (1.0 + scale) + shift


def _sigmoid(x):
    return 1.0 / (1.0 + jnp.exp(-x))


def _softplus(x):
    return jnp.maximum(x, 0.0) + jnp.log(1.0 + jnp.exp(-jnp.abs(x)))


def _mm(a, b):
    return jnp.dot(a.astype(BF16), b.astype(BF16), preferred_element_type=F32)


def _mm_nt(a, b):
    return lax.dot_general(a.astype(BF16), b.astype(BF16), (((1,), (1,)), ((), ())), preferred_element_type=F32)


def _ada_kernel(c_ref, w_ref, b_ref, o_ref):
    cond = c_ref[...]
    a = cond * _sigmoid(cond)
    o_ref[...] = _mm(a, w_ref[...]) + b_ref[...]


def _ada_table(cond, ada_w, ada_b):
    depth, d, nd = ada_w.shape
    return pl.pallas_call(
        _ada_kernel,
        grid=(depth, nd // d),
        in_specs=[pl.BlockSpec((MOD_ROWS, d), lambda l, n: (0, 0)),
                  pl.BlockSpec((None, d, d), lambda l, n: (l, 0, n)),
                  pl.BlockSpec((None, 1, d), lambda l, n: (l, 0, n))],
        out_specs=pl.BlockSpec((None, MOD_ROWS, d), lambda l, n: (l, 0, n)),
        out_shape=jax.ShapeDtypeStruct((depth, MOD_ROWS, nd), F32),
        compiler_params=_params(("parallel", "parallel")),
        name="ada_table",
    )(cond, ada_w, ada_b.reshape(depth, 1, nd))


def _row_call(body, cfg, rows, consts, mods, tabs, outs, n_tiles=None, name=None):
    n_tiles = cfg.tiles if n_tiles is None else n_tiles
    lat_tiles, tpb = cfg.lat_tiles, cfg.tpb
    arrays, specs = [], []
    for r in rows:
        if isinstance(r, tuple):
            arr, g = r
            arrays.append(arr)
            specs.append(pl.BlockSpec((None, ROW_TILE, arr.shape[-1]), lambda i, g=g: (g, i, 0)))
        else:
            arrays.append(r)
            specs.append(pl.BlockSpec((ROW_TILE, r.shape[-1]), lambda i: (i, 0)))
    for a in consts:
        arrays.append(a)
        specs.append(pl.BlockSpec(a.shape, lambda i, nd=a.ndim: (0,) * nd))
    for a in mods:
        arrays.append(a)
        specs.append(pl.BlockSpec((None, 1, a.shape[-1]),
                                  lambda i: (jnp.where(i < lat_tiles, 1 + i // tpb, 0), 0, 0)))
    for a in tabs:
        arrays.append(a)
        specs.append(pl.BlockSpec((None, ROW_TILE, a.shape[-1]),
                                  lambda i: (jnp.where(i < lat_tiles, 1 + i % tpb, 0), 0, 0)))
    out_shapes, out_specs = [], []
    for shape, dtype in outs:
        out_shapes.append(jax.ShapeDtypeStruct(shape, dtype))
        if shape[0] == TOP_K:
            out_specs.append(pl.BlockSpec((shape[0], ROW_TILE), lambda i: (0, i)))
        elif shape[0] == N_EXPERTS:
            out_specs.append(pl.BlockSpec((shape[0], LANES), lambda i: (0, i)))
        else:
            out_specs.append(pl.BlockSpec((ROW_TILE, shape[-1]), lambda i: (i, 0)))
    n_r, n_c, n_m, n_t = len(rows), len(consts), len(mods), len(tabs)

    def kern(*refs):
        r = refs[:n_r]
        c = refs[n_r:n_r + n_c]
        m = refs[n_r + n_c:n_r + n_c + n_m]
        t = refs[n_r + n_c + n_m:n_r + n_c + n_m + n_t]
        o = refs[n_r + n_c + n_m + n_t:]
        body(r, c, m, t, o)

    return pl.pallas_call(
        kern, grid=(n_tiles,), in_specs=specs, out_specs=out_specs, out_shape=out_shapes,
        compiler_params=_params(("parallel",)), name=name,
    )(*arrays)


def _residual_in(r, m, has_prev):
    if not has_prev:
        return r[0][...]
    wt = r[1 + TOP_K][...]
    y = wt[:, 0:1] * r[1][...].astype(F32)
    for k in range(1, TOP_K):
        y = y + wt[:, k:k + 1] * r[1 + k][...].astype(F32)
    return r[0][...] + m[0][...] * y


def _rope_pair(x, cos, sin_a, sin_b, shift):
    return x * cos + pltpu.roll(x, shift, 1) * sin_a + pltpu.roll(x, LANES - shift, 1) * sin_b


def _mla_pre_body(has_prev, r, c, m, t, o):
    g1, w_in, qg, kvg, wq, wk, wv, place, v_ones = c
    shift1, scale1 = m[-2], m[-1]
    cos_q, sa_q, sb_q, cos_k, sa_k, sb_k = [a[...] for a in t]
    x = _residual_in(r, m, has_prev)
    h = _norm_mod(x, g1[...], scale1[...], shift1[...])
    z = _mm(h, w_in[...])
    cq = _rms(z[:, :MLA_Q_RANK], qg[...]).astype(BF16)
    ckv = _rms(z[:, MLA_Q_RANK:MLA_Q_RANK + MLA_KV_RANK], kvg[...]).astype(BF16)
    kr = z[:, MLA_Q_RANK + MLA_KV_RANK:]
    kr = _rope_pair(kr, cos_k, sa_k, sb_k, MLA_ROPE // 2)
    scale = (MLA_NOPE + MLA_ROPE) ** -0.5 * LOG2_E
    q = _mm(cq, wq[...])
    for hh in range(MLA_HEADS):
        qs = q[:, hh * LANES:(hh + 1) * LANES]
        qs = _rope_pair(qs, cos_q, sa_q, sb_q, MLA_ROPE // 2)
        o[1][:, hh * LANES:(hh + 1) * LANES] = (qs * scale).astype(BF16)
    o[0][...] = x
    o[2][...] = (_mm(ckv, wk[...]) + _mm(kr, place[...])).astype(BF16)
    o[3][...] = (_mm(ckv, wv[...]) + v_ones[...]).astype(BF16)


def _gqa_pre_body(has_prev, r, c, m, t, o):
    g1, w_in, qg, kg = c
    shift1, scale1 = m[-2], m[-1]
    cos, sin = t[0][...], t[1][...]
    x = _residual_in(r, m, has_prev)
    h = _norm_mod(x, g1[...], scale1[...], shift1[...])
    z = _mm(h, w_in[...])
    scale = GQA_HEAD_DIM ** -0.5 * LOG2_E
    hd = GQA_HEAD_DIM
    for hh in range(GQA_Q_HEADS + GQA_KV_HEADS):
        zs = z[:, hh * hd:(hh + 1) * hd]
        is_q = hh < GQA_Q_HEADS
        zs = _rms(zs, qg[...] if is_q else kg[...])
        zs = zs * cos + pltpu.roll(zs, hd // 2, 1) * sin
        if is_q:
            o[1][:, hh * hd:(hh + 1) * hd] = (zs * scale).astype(BF16)
        else:
            kk = hh - GQA_Q_HEADS
            o[2][:, kk * hd:(kk + 1) * hd] = zs.astype(BF16)
    o[0][...] = x
    o[3][...] = z[:, (GQA_Q_HEADS + GQA_KV_HEADS) * hd:].astype(BF16)


def _mlstm_pre_body(has_prev, r, c, m, t, o):
    g1, w_in, gate_b, fmask = c
    shift1, scale1 = m[-2], m[-1]
    x = _residual_in(r, m, has_prev)
    h = _norm_mod(x, g1[...], scale1[...], shift1[...])
    z = _mm(h, w_in[...])
    nqk = MLSTM_HEADS * MLSTM_QK
    nv = MLSTM_HEADS * MLSTM_V
    o[0][...] = x
    o[1][...] = z[:, :nqk].astype(BF16)
    o[2][...] = (z[:, nqk:2 * nqk] * MLSTM_QK ** -0.5).astype(BF16)
    o[3][...] = z[:, 2 * nqk:2 * nqk + nv].astype(BF16)
    o[4][...] = z[:, 2 * nqk + nv:2 * nqk + 2 * nv].astype(BF16)
    gt = z[:, 2 * nqk + 2 * nv:] + gate_b[...]
    log_sig = jnp.minimum(gt, 0.0) - jnp.log(1.0 + jnp.exp(-jnp.abs(gt)))
    o[5][...] = jnp.where(fmask[...] > 0.0, log_sig, gt)


def _gdn_pre_body(has_prev, r, c, m, t, o):
    g1, w_in, w_ba, a_scale, dt_b, dmask = c
    shift1, scale1 = m[-2], m[-1]
    x = _residual_in(r, m, has_prev)
    h = _norm_mod(x, g1[...], scale1[...], shift1[...]).astype(BF16)
    ncv = 2 * GDN_QK_HEADS * GDN_HEAD_DIM + GDN_V_HEADS * GDN_HEAD_DIM
    z = _mm(h, w_in[...])
    o[0][...] = x
    o[1][...] = z[:, :ncv].astype(BF16)
    o[2][...] = z[:, ncv:].astype(BF16)
    ba = _mm(h, w_ba[...])
    o[3][...] = jnp.where(dmask[...] > 0.0, -a_scale[...] * _softplus(ba + dt_b[...]), _sigmoid(ba))


def _route_tail(x, y, c, m, o):
    g2, rw_t, rb = c
    gate1, shift2, scale2 = m
    x1 = x + gate1[...] * y
    h2 = _norm_mod(x1, g2[...], scale2[...], shift2[...])
    o[0][...] = x1
    o[1][...] = h2.astype(BF16)
    logits = lax.dot_general(rw_t[...], h2, (((1,), (1,)), ((), ())), preferred_element_type=F32,
                             precision=HIGHEST) + rb[...]
    e_iota = lax.broadcasted_iota(jnp.int32, logits.shape, 0)
    vals = []
    picked = jnp.zeros(logits.shape, F32)
    for k in range(TOP_K):
        mx = jnp.max(logits, axis=0, keepdims=True)
        idx = jnp.min(jnp.where(logits == mx, e_iota, N_EXPERTS), axis=0, keepdims=True)
        o[2][k:k + 1, :] = idx
        vals.append(mx)
        hit = e_iota == idx
        picked = picked + jnp.where(hit, 1.0, 0.0)
        logits = jnp.where(hit, -jnp.inf, logits)
    o[4][...] = jnp.broadcast_to(jnp.sum(picked, axis=1, keepdims=True), o[4].shape)
    es = [jnp.exp(v - vals[0]) for v in vals]
    tot = es[0] + es[1] + es[2] + es[3]
    for k in range(TOP_K):
        o[3][k:k + 1, :] = es[k] / tot


def _attn_post_body(r, c, m, t, o):
    x, a = r
    y = _mm(a[...], c[0][...])
    _route_tail(x[...], y, c[1:], m, o)


def _mlstm_post_body(r, c, m, t, o):
    x, hf, hb, og = r
    w_o, ng = c[0], c[1]
    hs = hf[...] + hb[...]
    gate = _sigmoid(og[...].astype(F32))
    parts = []
    for hh in range(MLSTM_HEADS):
        sl = slice(hh * MLSTM_V, (hh + 1) * MLSTM_V)
        parts.append((_rms(hs[:, sl], ng[...]) * gate[:, sl]).astype(BF16))
    y = _mm(jnp.concatenate(parts, axis=1), w_o[...])
    _route_tail(x[...], y, c[2:], m, o)


def _gdn_post_body(r, c, m, t, o):
    x, of, ob, zg = r
    w_o, ng = c[0], c[1]
    os_ = of[...] + ob[...]
    g = zg[...].astype(F32)
    gate = g * _sigmoid(g)
    parts = []
    for hh in range(GDN_V_HEADS):
        sl = slice(hh * GDN_HEAD_DIM, (hh + 1) * GDN_HEAD_DIM)
        parts.append((_rms(os_[:, sl], ng[...]) * gate[:, sl]).astype(BF16))
    y = _mm(jnp.concatenate(parts, axis=1), w_o[...])
    _route_tail(x[...], y, c[2:], m, o)


def _final_body(r, c, m, t, o):
    o[0][...] = _rms(_residual_in(r, m, True), c[0][...])


def _attn_kernel(*refs, n_seg, k_stride, v_width):
    q_ref = refs[0]
    segs = [(refs[1 + 2 * i], refs[2 + 2 * i]) for i in range(n_seg)]
    o_ref = refs[1 + 2 * n_seg]
    n_heads = 2 * ATTN_GROUPS_PER_STEP

    def scores(j):
        q = q_ref[:, j * LANES:(j + 1) * LANES]
        koff = j * LANES if k_stride else (j // 2) * LANES
        return [_mm_nt(q, k_ref[:, koff:koff + LANES]) for k_ref, _ in segs]

    ss_next = scores(0)
    for j in range(n_heads):
        ss = ss_next
        if j + 1 < n_heads:
            ss_next = scores(j + 1)
        mx = ss[0].max(axis=-1, keepdims=True)
        for s in ss[1:]:
            mx = jnp.maximum(mx, s.max(axis=-1, keepdims=True))
        acc, den = 0.0, 0.0
        for s, (_, v_ref) in zip(ss, segs):
            p = jnp.exp2(s - mx)
            if k_stride:
                acc = acc + _mm(p, v_ref[:, j * 2 * v_width:(j + 1) * 2 * v_width])
            else:
                den = den + p.sum(axis=-1, keepdims=True)
                acc = acc + _mm(p, v_ref[:, (j // 2) * v_width:(j // 2 + 1) * v_width])
        if k_stride:
            den = acc[:, v_width:v_width + 1]
            acc = acc[:, :v_width]
        o_ref[:, j * v_width:(j + 1) * v_width] = (acc / den).astype(o_ref.dtype)


def _attention(cfg, q, k, v, n_groups, k_stride, v_width, latent):
    gps = ATTN_GROUPS_PER_STEP
    kw = gps * (2 * LANES if k_stride else LANES)
    vw = gps * (4 * v_width if k_stride else v_width)
    qw = gps * 2 * LANES
    ow = gps * 2 * v_width
    ctx_blk0 = cfg.n_lat // cfg.nc
    k_ctx = pl.BlockSpec((cfg.nc, kw), lambda b, g, i: (ctx_blk0 + b, g))
    v_ctx = pl.BlockSpec((cfg.nc, vw), lambda b, g, i: (ctx_blk0 + b, g))
    if latent:
        qpb = cfg.s // Q_TILE
        q_spec = pl.BlockSpec((Q_TILE, qw), lambda b, g, i: (b * qpb + i, g))
        o_spec = pl.BlockSpec((Q_TILE, ow), lambda b, g, i: (b * qpb + i, g))
        in_specs = [q_spec, pl.BlockSpec((cfg.s, kw), lambda b, g, i: (b, g)),
                    pl.BlockSpec((cfg.s, vw), lambda b, g, i: (b, g)), k_ctx, v_ctx]
        args = (q, k, v, k, v)
        n_seg = 2
        rows = cfg.n_lat
    else:
        tq = min(Q_TILE, cfg.nc)
        qpb = cfg.nc // tq
        q0 = cfg.n_lat // tq
        q_spec = pl.BlockSpec((tq, qw), lambda b, g, i: (q0 + b * qpb + i, g))
        o_spec = pl.BlockSpec((tq, ow), lambda b, g, i: (b * qpb + i, g))
        in_specs = [q_spec, k_ctx, v_ctx]
        args = (q, k, v)
        n_seg = 1
        rows = cfg.b * cfg.nc
    grid = (cfg.b, n_groups // gps, qpb)
    return pl.pallas_call(
        functools.partial(_attn_kernel, n_seg=n_seg, k_stride=k_stride, v_width=v_width),
        grid=grid, in_specs=in_specs, out_specs=o_spec,
        out_shape=jax.ShapeDtypeStruct((rows, n_groups * 2 * v_width), BF16),
        compiler_params=_params(("parallel", "parallel", "arbitrary")),
        name="attention_lat" if latent else "attention_ctx",
    )(*args)


def _block_masks(flip):
    li = lax.broadcasted_iota(jnp.int32, (ROW_TILE, ROW_TILE), 0)
    si = lax.broadcasted_iota(jnp.int32, (ROW_TILE, ROW_TILE), 1)
    same = (li // CHUNK) == (si // CHUNK)
    eye = li == si
    incl = same & ((si >= li) if flip else (si <= li))
    strict = same & ((si > li) if flip else (si < li))
    return eye, same, incl, strict


def _scan_block_index(cfg, flip):
    cpb, tpb, lat_tiles = cfg.cpb, cfg.tpb, cfg.lat_tiles

    def row_block(b, j):
        if flip:
            ctx = lat_tiles + b * cpb + (cpb - 1 - j)
            lat = b * tpb + (tpb - 1 - (j - cpb))
        else:
            ctx = lat_tiles + b * cpb + j
            lat = b * tpb + (j - cpb)
        return jnp.where(j < cpb, ctx, lat)

    return row_block


def _mlstm_kernel(q_ref, k_ref, v_ref, g_ref, o_ref, c_st, n_st, m_st, *, flip, d):
    @pl.when(pl.program_id(1) == 0)
    def _():
        c_st[...] = jnp.zeros_like(c_st)
        n_st[...] = jnp.zeros_like(n_st)
        m_st[...] = jnp.zeros_like(m_st)

    eye, same, incl, _ = _block_masks(flip)
    n_chunks = ROW_TILE // CHUNK
    order = [n_chunks - 1 - ci if flip else ci for ci in range(n_chunks)]
    to_row = lambda col: jnp.sum(jnp.where(eye, col, 0.0), axis=0, keepdims=True)
    per_chunk_col = lambda vals: jnp.concatenate(
        [jnp.broadcast_to(vals[c], (CHUNK, 1)) for c in range(n_chunks)], axis=0)
    hs = range(MLSTM_HEADS)
    qs = [q_ref[:, hh * MLSTM_QK:(hh + 1) * MLSTM_QK] for hh in hs]
    ks = [k_ref[:, hh * MLSTM_QK:(hh + 1) * MLSTM_QK] for hh in hs]
    vs = [v_ref[:, hh * MLSTM_V:(hh + 1) * MLSTM_V] for hh in hs]
    qks = [_mm_nt(qs[hh], ks[hh]) for hh in hs]
    i_cols = [g_ref[:, d * 2 * MLSTM_HEADS + hh:d * 2 * MLSTM_HEADS + hh + 1] for hh in hs]
    f_cols = [g_ref[:, (d * 2 + 1) * MLSTM_HEADS + hh:(d * 2 + 1) * MLSTM_HEADS + hh + 1] for hh in hs]
    f_rows = [to_row(f) for f in f_cols]
    i_rows = [to_row(i) for i in i_cols]
    bcum_cols = [jnp.sum(jnp.where(incl, f, 0.0), axis=1, keepdims=True) for f in f_rows]
    bcum_rows = [to_row(b) for b in bcum_cols]
    bend_cols = [jnp.sum(jnp.where(same, f, 0.0), axis=1, keepdims=True) for f in f_rows]
    w_logs = [bend_cols[hh] - bcum_cols[hh] + i_cols[hh] for hh in hs]
    wmax_cols = [jnp.max(jnp.where(same, to_row(w), -jnp.inf), axis=1, keepdims=True) for w in w_logs]
    m_prevs = [[None] * n_chunks for _ in hs]
    m_news = [[None] * n_chunks for _ in hs]
    m_runs = [m_st[hh] for hh in hs]
    for cc in order:
        r0 = cc * CHUNK
        for hh in hs:
            m_prevs[hh][cc] = m_runs[hh]
            m_runs[hh] = jnp.maximum(bend_cols[hh][r0:r0 + 1] + m_runs[hh], wmax_cols[hh][r0:r0 + 1])
            m_news[hh][cc] = m_runs[hh]
    d_logs = [jnp.where(incl, bcum_cols[hh] - bcum_rows[hh] + i_rows[hh], -jnp.inf) for hh in hs]
    g_logs = [bcum_cols[hh] + per_chunk_col(m_prevs[hh]) for hh in hs]
    m_ts = [jnp.maximum(g_logs[hh], jnp.max(d_logs[hh], axis=1, keepdims=True)) for hh in hs]
    ss = [qks[hh] * jnp.exp(d_logs[hh] - m_ts[hh]) for hh in hs]
    svs = [_mm(ss[hh], vs[hh]) for hh in hs]
    kws = [ks[hh].astype(F32) * jnp.exp(w_logs[hh] - per_chunk_col(m_news[hh])) for hh in hs]
    prep = []
    for hh in hs:
        carries = [jnp.exp(bend_cols[hh][cc * CHUNK:cc * CHUNK + 1] + m_prevs[hh][cc] - m_news[hh][cc])
                   for cc in range(n_chunks)]
        kvs = [_mm(kws[hh][cc * CHUNK:(cc + 1) * CHUNK].T, vs[hh][cc * CHUNK:(cc + 1) * CHUNK])
               for cc in range(n_chunks)]
        ksum = [jnp.sum(kws[hh][cc * CHUNK:(cc + 1) * CHUNK], axis=0, keepdims=True) for cc in range(n_chunks)]
        prep.append((qs[hh], jnp.exp(g_logs[hh] - m_ts[hh]), svs[hh], jnp.sum(ss[hh], axis=1, keepdims=True),
                     jnp.exp(-m_ts[hh]), kvs, ksum, carries, m_runs[hh]))
    for cc in order:
        rows = slice(cc * CHUNK, (cc + 1) * CHUNK)
        for hh in range(MLSTM_HEADS):
            q, inter, sv, ssum, floor, kvs, ksum, carries, _ = prep[hh]
            c_prev, n_prev = c_st[hh], n_st[hh]
            qc = q[rows]
            num = inter[rows] * _mm(qc, c_prev) + sv[rows]
            den = inter[rows] * jnp.sum(qc.astype(F32) * n_prev, axis=1, keepdims=True) + ssum[rows]
            o_ref[rows, hh * MLSTM_V:(hh + 1) * MLSTM_V] = num / jnp.maximum(jnp.abs(den), floor[rows])
            c_st[hh] = carries[cc] * c_prev + kvs[cc]
            n_st[hh] = carries[cc] * n_prev + ksum[cc]
    for hh in range(MLSTM_HEADS):
        m_st[hh] = prep[hh][-1]


def _mlstm_scan(cfg, q, k, v, gates):
    nqk, nv = MLSTM_HEADS * MLSTM_QK, MLSTM_HEADS * MLSTM_V
    outs = []
    for d in range(2):
        flip = d == 1
        rb = _scan_block_index(cfg, flip)
        row = lambda w, rb=rb: pl.BlockSpec((ROW_TILE, w), lambda b, j: (rb(b, j), 0))
        outs.append(pl.pallas_call(
            functools.partial(_mlstm_kernel, flip=flip, d=d),
            grid=(cfg.b, cfg.cpb + cfg.tpb),
            in_specs=[row(nqk), row(nqk), row(nv), row(LANES)],
            out_specs=row(nv),
            out_shape=jax.ShapeDtypeStruct((cfg.m, nv), F32),
            scratch_shapes=[pltpu.VMEM((MLSTM_HEADS, MLSTM_QK, MLSTM_V), F32),
                            pltpu.VMEM((MLSTM_HEADS, 1, MLSTM_QK), F32),
                            pltpu.VMEM((MLSTM_HEADS, 1, 1), F32)],
            compiler_params=_params(("parallel", "arbitrary")),
            name="mlstm_bwd" if flip else "mlstm_fwd",
        )(q, k, v, gates))
    return outs


def _gdn_kernel(q_ref, k_ref, v_ref, ba_ref, o_ref, s_st, *, flip, d):
    @pl.when(pl.program_id(2) == 0)
    def _():
        s_st[...] = jnp.zeros_like(s_st)

    eye, same, incl, strict = _block_masks(flip)
    eye_f = jnp.where(eye, 1.0, 0.0)
    rep = GDN_V_HEADS // GDN_QK_HEADS
    hd = GDN_HEAD_DIM
    n_chunks = ROW_TILE // CHUNK
    heads = [(qi, jj) for qi in range(GDN_QK_PER_STEP) for jj in range(rep)]
    ks = [k_ref[:, qi * hd:(qi + 1) * hd] for qi in range(GDN_QK_PER_STEP)]
    kfs = [k.astype(F32) for k in ks]
    qfs = [q_ref[:, qi * hd:(qi + 1) * hd].astype(F32) for qi in range(GDN_QK_PER_STEP)]
    qks = [_mm_nt(q_ref[:, qi * hd:(qi + 1) * hd], ks[qi]) for qi in range(GDN_QK_PER_STEP)]
    betas, gcs, gces, decays, kbs = [], [], [], [], []
    for qi, jj in heads:
        bcol = qi * LANES + d * 2 * rep + jj
        beta = ba_ref[:, bcol:bcol + 1]
        ld = ba_ref[:, bcol + rep:bcol + rep + 1]
        ld_row = jnp.sum(jnp.where(eye, ld, 0.0), axis=0, keepdims=True)
        gc_col = jnp.sum(jnp.where(incl, ld_row, 0.0), axis=1, keepdims=True)
        gc_row = jnp.sum(jnp.where(eye, gc_col, 0.0), axis=0, keepdims=True)
        gces.append(jnp.sum(jnp.where(same, ld_row, 0.0), axis=1, keepdims=True))
        decays.append(jnp.exp(jnp.where(incl, gc_col - gc_row, -jnp.inf)))
        betas.append(beta)
        gcs.append(gc_col)
        kbs.append(kfs[qi] * beta)
    def pack(bd):
        out = bd[0:CHUNK]
        for c in range(1, n_chunks):
            out = out + bd[c * CHUNK:(c + 1) * CHUNK]
        return out

    def block_diag(packed):
        return jnp.where(same, jnp.concatenate([packed] * n_chunks, axis=0), 0.0)

    pws = [pack(jnp.where(strict, _mm_nt(kbs[i], ks[qi]) * decays[i], 0.0)) for i, (qi, _) in enumerate(heads)]
    eye_p = pack(eye_f)
    invs = [eye_p - p for p in pws]
    bds = [block_diag(p) for p in pws]
    for _ in range(int(math.log2(CHUNK)) - 1):
        pws = [_mm(p, bd) for p, bd in zip(pws, bds)]
        bds = [block_diag(p) for p in pws]
        invs = [iv + _mm(iv, bd) for iv, bd in zip(invs, bds)]
    invs = [block_diag(iv) for iv in invs]
    egs = [jnp.exp(g) for g in gcs]
    uws = []
    for i, (qi, jj) in enumerate(heads):
        v = v_ref[:, (qi * rep + jj) * hd:(qi * rep + jj + 1) * hd].astype(F32)
        uws.append(_mm(invs[i], jnp.concatenate([v * betas[i], kbs[i] * egs[i]], axis=1)))
    aws = [_mm(qks[qi] * decays[i], uws[i]) for i, (qi, _) in enumerate(heads)]
    qes = [qfs[qi] * egs[i] - aws[i][:, hd:] for i, (qi, _) in enumerate(heads)]
    kdecs = [kfs[qi] * jnp.exp(gces[i] - gcs[i]) for i, (qi, _) in enumerate(heads)]
    for ci in range(n_chunks):
        cc = n_chunks - 1 - ci if flip else ci
        rows = slice(cc * CHUNK, (cc + 1) * CHUNK)
        bgs = [_mm(kdecs[i][rows].T, uws[i][rows]) for i in range(len(heads))]
        for i, (qi, jj) in enumerate(heads):
            s_prev = s_st[i]
            col = (qi * rep + jj) * hd
            o_ref[rows, col:col + hd] = _mm(qes[i][rows], s_prev) + aws[i][rows, :hd]
            g_end = jnp.exp(gces[i][cc * CHUNK:cc * CHUNK + 1])
            s_st[i] = g_end * s_prev - _mm(bgs[i][:, hd:], s_prev) + bgs[i][:, :hd]


def _gdn_scan(cfg, qkv, ba):
    rep = GDN_V_HEADS // GDN_QK_HEADS
    hd = GDN_HEAD_DIM
    g = GDN_QK_PER_STEP
    outs = []
    for d in range(2):
        flip = d == 1
        rb = _scan_block_index(cfg, flip)
        col = lambda w, off, rb=rb: pl.BlockSpec((ROW_TILE, w), lambda b, h, j: (rb(b, j), off + h))
        outs.append(pl.pallas_call(
            functools.partial(_gdn_kernel, flip=flip, d=d),
            grid=(cfg.b, GDN_QK_HEADS // g, cfg.cpb + cfg.tpb),
            in_specs=[col(g * hd, 0), col(g * hd, GDN_QK_HEADS // g),
                      col(g * rep * hd, 2 * GDN_QK_HEADS // (g * rep)), col(g * LANES, 0)],
            out_specs=col(g * rep * hd, 0),
            out_shape=jax.ShapeDtypeStruct((cfg.m, GDN_V_HEADS * hd), F32),
            scratch_shapes=[pltpu.VMEM((g * rep, hd, hd), F32)],
            compiler_params=_params(("parallel", "parallel", "arbitrary")),
            name="gdn_bwd" if flip else "gdn_fwd",
        )(qkv, qkv, qkv, ba))
    return outs


def _gdn_conv_kernel(cur_ref, prev_ref, next_ref, w_ref, o_ref, pad_ref, *, lat_tiles, tpb, cpb):
    i = pl.program_id(0)
    cb = pl.program_id(1)
    lat = i < lat_tiles
    pos = jnp.where(lat, i % tpb, (i - lat_tiles) % cpb)
    per = jnp.where(lat, tpb, cpb)
    first = pos == 0
    last = pos == per - 1
    prev = prev_ref[...].astype(F32)
    nxt = next_ref[...].astype(F32)
    pad_ref[0:HALO, :] = jnp.where(first, 0.0, prev)
    pad_ref[HALO:HALO + ROW_TILE, :] = cur_ref[...].astype(F32)
    pad_ref[HALO + ROW_TILE:, :] = jnp.where(last, 0.0, nxt)
    half = GDN_CONV // 2
    acc = 0.0
    for tap in range(GDN_CONV):
        off = HALO - half + tap
        acc = acc + pad_ref[off:off + ROW_TILE, :] * w_ref[tap:tap + 1, :]
    act = acc * _sigmoid(acc)
    qscale = jnp.where(cb == 0, GDN_HEAD_DIM ** -0.5, 1.0)
    hd = GDN_HEAD_DIM
    for hh in range(act.shape[1] // hd):
        a = act[:, hh * hd:(hh + 1) * hd]
        nrm = a * lax.rsqrt(jnp.sum(a * a, axis=-1, keepdims=True) + NORM_EPS) * qscale
        o_ref[:, hh * hd:(hh + 1) * hd] = jnp.where(cb < 2, nrm, a).astype(o_ref.dtype)


def _gdn_conv(cfg, zqkv, conv_w):
    cw = GDN_QK_HEADS * GDN_HEAD_DIM
    n_cb = zqkv.shape[1] // cw
    hpt = ROW_TILE // HALO
    n_halo = cfg.m // HALO
    return pl.pallas_call(
        functools.partial(_gdn_conv_kernel, lat_tiles=cfg.lat_tiles, tpb=cfg.tpb, cpb=cfg.cpb),
        grid=(cfg.tiles, n_cb),
        in_specs=[pl.BlockSpec((ROW_TILE, cw), lambda i, c: (i, c)),
                  pl.BlockSpec((HALO, cw), lambda i, c: (jnp.maximum(i * hpt - 1, 0), c)),
                  pl.BlockSpec((HALO, cw), lambda i, c: (jnp.minimum((i + 1) * hpt, n_halo - 1), c)),
                  pl.BlockSpec((8, cw), lambda i, c: (0, c))],
        out_specs=pl.BlockSpec((ROW_TILE, cw), lambda i, c: (i, c)),
        out_shape=jax.ShapeDtypeStruct(zqkv.shape, BF16),
        scratch_shapes=[pltpu.VMEM((ROW_TILE + 2 * HALO, cw), F32)],
        compiler_params=_params(("parallel", "parallel")),
        name="gdn_conv",
    )(zqkv, zqkv, zqkv, conv_w)


def _rank_kernel(e_ref, start_ref, slot_ref, run_ref):
    @pl.when(pl.program_id(0) == 0)
    def _():
        run_ref[...] = start_ref[...]

    ji = lax.broadcasted_iota(jnp.int32, (RANK_TILE, RANK_TILE), 0)
    si = lax.broadcasted_iota(jnp.int32, (RANK_TILE, RANK_TILE), 1)
    upper = jnp.where(ji <= si, 1.0, 0.0).astype(BF16)
    run = run_ref[:, 0:1]
    for sb in range(RANK_TILES_PER_STEP):
        cols = slice(sb * RANK_TILE, (sb + 1) * RANK_TILE)
        e = e_ref[:, cols]
        onehot = (lax.broadcasted_iota(jnp.int32, (N_EXPERTS, RANK_TILE), 0) == e)
        oh = jnp.where(onehot, 1.0, 0.0)
        cum = jnp.dot(oh.astype(BF16), upper, preferred_element_type=F32)
        slot = jnp.sum(oh * (cum - 1.0 + run), axis=0, keepdims=True)
        slot_ref[:, cols] = slot.astype(jnp.int32)
        run = run + jnp.sum(oh, axis=1, keepdims=True)
    run_ref[...] = jnp.broadcast_to(run, run_ref.shape)


def _slot_pairs(expert_of_pair, group_start):
    n_pairs = expert_of_pair.shape[1]
    step = RANK_TILE * RANK_TILES_PER_STEP
    return pl.pallas_call(
        _rank_kernel,
        grid=(n_pairs // step,),
        in_specs=[pl.BlockSpec((1, step), lambda i: (0, i)),
                  pl.BlockSpec((N_EXPERTS, LANES), lambda i: (0, 0))],
        out_specs=pl.BlockSpec((1, step), lambda i: (0, i)),
        out_shape=jax.ShapeDtypeStruct((1, n_pairs), jnp.int32),
        scratch_shapes=[pltpu.VMEM((N_EXPERTS, LANES), F32)],
        compiler_params=_params(("arbitrary",)),
        name="moe_rank",
    )(expert_of_pair, group_start)


def _moe_kernel(te_ref, nu_ref, x_ref, w1_ref, b1g_ref, b1l_ref, w2_ref, b2_ref, o_ref, w1g_s, w1l_s, w2_s):
    j = pl.program_id(0)
    used = j < nu_ref[0]
    new_expert = jnp.logical_or(j == 0, te_ref[j] != te_ref[jnp.maximum(j - 1, 0)])

    @pl.when(jnp.logical_and(used, new_expert))
    def _():
        slab = 2 * LANES
        src = lax.broadcasted_iota(jnp.int32, (slab, slab), 0)
        dst = lax.broadcasted_iota(jnp.int32, (slab, slab), 1)
        want = jnp.where(dst < LANES, 2 * dst, 2 * (dst - LANES) + 1)
        perm = jnp.where(src == want, 1.0, 0.0).astype(BF16)
        for sb in range(w1_ref.shape[1] // slab):
            sorted_cols = jnp.dot(w1_ref[:, sb * slab:(sb + 1) * slab].astype(BF16), perm,
                                  preferred_element_type=F32)
            w1g_s[:, sb * LANES:(sb + 1) * LANES] = sorted_cols[:, :LANES].astype(BF16)
            w1l_s[:, sb * LANES:(sb + 1) * LANES] = sorted_cols[:, LANES:].astype(BF16)
        w2_s[...] = w2_ref[...].astype(BF16)

    @pl.when(used)
    def _():
        x = x_ref[...]
        glu = jnp.minimum(jnp.dot(x, w1g_s[...], preferred_element_type=F32) + b1g_ref[...], SWIGLU_LIMIT)
        lin = jnp.clip(jnp.dot(x, w1l_s[...], preferred_element_type=F32) + b1l_ref[...],
                       -SWIGLU_LIMIT, SWIGLU_LIMIT)
        act = glu * _sigmoid(SWIGLU_ALPHA * glu) * (lin + 1.0)
        y = jnp.dot(act.astype(BF16), w2_s[...], preferred_element_type=F32) + b2_ref[...]
        o_ref[...] = y.astype(o_ref.dtype)

    @pl.when(jnp.logical_not(used))
    def _():
        o_ref[...] = jnp.zeros_like(o_ref)


def _moe_grouped(xs, tile_expert, n_used, layer, w1_all, b1g, b1l, w2_all, b2):
    n_slots, d = xs.shape
    de = w2_all.shape[2]
    wspec = lambda r, c: pl.BlockSpec((None, r, c), lambda j, te, nu: (te[j], 0, 0))
    lspec = lambda r, c: pl.BlockSpec((None, None, r, c), lambda j, te, nu: (layer, te[j], 0, 0))
    return pl.pallas_call(
        _moe_kernel,
        grid_spec=pltpu.PrefetchScalarGridSpec(
            num_scalar_prefetch=2,
            grid=(n_slots // GROUP_TILE,),
            in_specs=[pl.BlockSpec((GROUP_TILE, d), lambda j, te, nu: (j, 0)),
                      lspec(d, 2 * de), wspec(1, de), wspec(1, de), lspec(de, d), wspec(1, d)],
            out_specs=pl.BlockSpec((GROUP_TILE, d), lambda j, te, nu: (j, 0)),
            scratch_shapes=[pltpu.VMEM((d, de), BF16), pltpu.VMEM((d, de), BF16), pltpu.VMEM((de, d), BF16)]),
        out_shape=jax.ShapeDtypeStruct((n_slots, d), BF16),
        compiler_params=_params(("arbitrary",)),
        name="moe_grouped",
    )(tile_expert, n_used, xs, w1_all, b1g, b1l, w2_all, b2)


def _moe_ffn(h2, topi, topw, tile_counts, layer, w1_all, b1, w2_all, b2):
    n_tok = h2.shape[0]
    n_pairs = TOP_K * n_tok
    assert n_pairs % (RANK_TILE * RANK_TILES_PER_STEP) == 0
    n_slots = -(-(n_pairs + N_EXPERTS * (GROUP_TILE - 1)) // GROUP_TILE) * GROUP_TILE
    n_tiles = n_slots // GROUP_TILE

    counts = jnp.sum(tile_counts.reshape(N_EXPERTS, -1, LANES)[:, :, 0], axis=1).astype(jnp.int32)
    padded = (counts + GROUP_TILE - 1) // GROUP_TILE * GROUP_TILE
    ends = jnp.cumsum(padded)
    starts = ends - padded
    group_start = jnp.broadcast_to(starts.astype(F32)[:, None], (N_EXPERTS, LANES))
    slot = _slot_pairs(topi.reshape(1, n_pairs), group_start)[0]
    token = jnp.tile(jnp.arange(n_tok, dtype=jnp.int32), TOP_K)
    src = (jnp.arange(n_slots, dtype=jnp.int32) % n_tok).at[slot].set(
        token, unique_indices=True, mode="promise_in_bounds")
    tile_start = jnp.arange(n_tiles, dtype=jnp.int32) * GROUP_TILE
    tile_expert = jnp.sum((ends[None, :] <= tile_start[:, None]).astype(jnp.int32), axis=1)
    tile_expert = jnp.minimum(tile_expert, N_EXPERTS - 1)
    n_used = (ends[-1] // GROUP_TILE).astype(jnp.int32).reshape(1)

    xs = jnp.take(h2, src, axis=0, mode="clip")
    b1g = b1[:, None, 0::2]
    b1l = b1[:, None, 1::2]
    ys = _moe_grouped(xs, tile_expert, n_used, layer, w1_all, b1g, b1l, w2_all, b2[:, None, :])

    pos = slot.reshape(TOP_K, n_tok)
    return [jnp.take(ys, pos[k], axis=0, mode="clip") for k in range(TOP_K)] + [topw.T]


def _axial_angles(n_tokens, rot_dim):
    t = jnp.arange(n_tokens, dtype=jnp.int32)
    rows = (t // GRID_W).astype(F32)
    cols = (t % GRID_W).astype(F32)
    d_axis = rot_dim // 2
    inv_freq = ROPE_BASE ** (-jnp.arange(0, d_axis, 2, dtype=F32) / d_axis)
    ang = jnp.concatenate([rows[:, None] * inv_freq, cols[:, None] * inv_freq], axis=-1)
    return jnp.cos(ang), jnp.sin(ang)


def _seq_table(cfg, lat_rows, ctx_value):
    c = lat_rows.shape[-1]
    ctx = jnp.full((1, ROW_TILE, c), ctx_value, F32)
    return jnp.concatenate([ctx, lat_rows.reshape(cfg.tpb, ROW_TILE, c)], axis=0)


def _slab_rope_tables(cfg, rot_dim, lane0):
    cos, sin = _axial_angles(cfg.s, rot_dim)
    half = rot_dim // 2
    s = cfg.s
    ones = lambda n: jnp.ones((s, n), F32)
    zeros = lambda n: jnp.zeros((s, n), F32)
    tail = LANES - lane0 - rot_dim
    cos_t = jnp.concatenate([ones(lane0), cos, cos, ones(tail)], axis=1)
    sin_a = jnp.concatenate([zeros(lane0 + half), sin, zeros(tail)], axis=1)
    sin_b = jnp.concatenate([zeros(lane0), -sin, zeros(half + tail)], axis=1)
    return _seq_table(cfg, cos_t, 1.0), _seq_table(cfg, sin_a, 0.0), _seq_table(cfg, sin_b, 0.0)


def _mod_slices(table, n_rows):
    d = D_MODEL
    return [table[:, k * d:(k + 1) * d].reshape(n_rows, 1, d) for k in range(N_MOD)]


def kernel(x, c, ctx, c_ctx, ada_w, ada_b, norm1_g, norm2_g, router_w, router_b, moe_w1, moe_b1, moe_w2, moe_b2,
           mla_w_in, mla_q_norm_g, mla_kv_norm_g, mla_w_uq, mla_w_ukv, mla_w_o, gqa_w_in, gqa_q_norm_g,
           gqa_k_norm_g, gqa_w_o, mlstm_w_in, mlstm_gate_b, mlstm_norm_g, mlstm_w_o, gdn_w_in, gdn_conv_w,
           gdn_a_log, gdn_dt_bias, gdn_norm_g, gdn_w_o, final_norm_g):
    b, s, d = x.shape
    nc = ctx.shape[1]
    depth = ada_w.shape[0]
    cfg = _Cfg(b, s, nc)
    m = cfg.m
    row = lambda a: a.reshape(1, -1).astype(F32)

    cond = jnp.concatenate([c_ctx[None, :], c, jnp.zeros((MOD_ROWS - 1 - b, d), F32)], axis=0)
    mod_all = _ada_table(cond, ada_w, ada_b)

    stream = jnp.concatenate([x.reshape(b * s, d), ctx.reshape(b * nc, d)], axis=0)
    y_prev = None
    gate2_prev = None
    for i in range(depth):
        kind, j = i % 4, i // 4
        shift1, scale1, gate1, shift2, scale2, gate2 = _mod_slices(mod_all[i], MOD_ROWS)
        has_prev = y_prev is not None
        rows_in = [stream] + y_prev if has_prev else [stream]
        mods_in = ([gate2_prev] if has_prev else []) + [shift1, scale1]
        g1 = row(norm1_g[i])
        route_consts = [row(norm2_g[i]), router_w[i].T, router_b[i].reshape(N_EXPERTS, 1)]
        route_mods = [gate1, shift2, scale2]
        route_outs = [((m, d), F32), ((m, d), BF16), ((TOP_K, m), jnp.int32), ((TOP_K, m), F32),
                      ((N_EXPERTS, cfg.tiles * LANES), F32)]

        if kind == 0:
            qk = MLA_NOPE + MLA_ROPE
            w_in = jnp.pad(mla_w_in[j], ((0, 0), (0, LANES - MLA_ROPE))).astype(BF16)
            wq = jnp.pad(mla_w_uq[j].reshape(MLA_Q_RANK, MLA_HEADS, qk), ((0, 0), (0, 0), (0, LANES - qk)))
            wq = wq.reshape(MLA_Q_RANK, MLA_HEADS * LANES).astype(BF16)
            wkv = mla_w_ukv[j].reshape(MLA_KV_RANK, MLA_HEADS, MLA_NOPE + MLA_V)
            wk = jnp.pad(wkv[:, :, :MLA_NOPE], ((0, 0), (0, 0), (0, LANES - MLA_NOPE)))
            wk = wk.reshape(MLA_KV_RANK, MLA_HEADS * LANES).astype(BF16)
            wv = jnp.pad(wkv[:, :, MLA_NOPE:], ((0, 0), (0, 0), (0, MLA_V)))
            wv = wv.reshape(MLA_KV_RANK, MLA_HEADS * 2 * MLA_V).astype(BF16)
            v_ones = jnp.tile(jnp.concatenate([jnp.zeros((MLA_V,), F32), jnp.ones((MLA_V,), F32)]), MLA_HEADS)
            v_ones = v_ones.reshape(1, MLA_HEADS * 2 * MLA_V)
            place = jnp.zeros((LANES, LANES), F32).at[jnp.arange(MLA_ROPE), MLA_NOPE + jnp.arange(MLA_ROPE)].set(1.0)
            place = jnp.tile(place, (1, MLA_HEADS)).astype(BF16)
            tabs = list(_slab_rope_tables(cfg, MLA_ROPE, MLA_NOPE)) + list(_slab_rope_tables(cfg, MLA_ROPE, 0))
            stream, q, k, v = _row_call(
                functools.partial(_mla_pre_body, has_prev), cfg, rows_in,
                [g1, w_in, row(mla_q_norm_g[j]), row(mla_kv_norm_g[j]), wq, wk, wv, place, v_ones], mods_in, tabs,
                [((m, d), F32), ((m, MLA_HEADS * LANES), BF16), ((m, MLA_HEADS * LANES), BF16),
                 ((m, MLA_HEADS * 2 * MLA_V), BF16)], name="mla_pre")
            a_lat = _attention(cfg, q, k, v, MLA_HEADS // 2, LANES, MLA_V, True)
            a_ctx = _attention(cfg, q, k, v, MLA_HEADS // 2, LANES, MLA_V, False)
            a = jnp.concatenate([a_lat, a_ctx], axis=0)
            stream, h2, topi, topw, tile_counts = _row_call(
                _attn_post_body, cfg, [stream, a], [mla_w_o[j].astype(BF16)] + route_consts, route_mods, [],
                route_outs, name="mla_post")
        elif kind == 1:
            cos, sin = _axial_angles(cfg.s, GQA_HEAD_DIM)
            tabs = [_seq_table(cfg, jnp.concatenate([cos, cos], axis=1), 1.0),
                    _seq_table(cfg, jnp.concatenate([-sin, sin], axis=1), 0.0)]
            stream, q, k, v = _row_call(
                functools.partial(_gqa_pre_body, has_prev), cfg, rows_in,
                [g1, gqa_w_in[j].astype(BF16), row(gqa_q_norm_g[j]), row(gqa_k_norm_g[j])], mods_in, tabs,
                [((m, d), F32), ((m, GQA_Q_HEADS * GQA_HEAD_DIM), BF16), ((m, GQA_KV_HEADS * GQA_HEAD_DIM), BF16),
                 ((m, GQA_KV_HEADS * GQA_HEAD_DIM), BF16)], name="gqa_pre")
            a_lat = _attention(cfg, q, k, v, GQA_KV_HEADS, 0, GQA_HEAD_DIM, True)
            a_ctx = _attention(cfg, q, k, v, GQA_KV_HEADS, 0, GQA_HEAD_DIM, False)
            a = jnp.concatenate([a_lat, a_ctx], axis=0)
            stream, h2, topi, topw, tile_counts = _row_call(
                _attn_post_body, cfg, [stream, a], [gqa_w_o[j].astype(BF16)] + route_consts, route_mods, [],
                route_outs, name="gqa_post")
        elif kind == 2:
            n_gate = 4 * MLSTM_HEADS
            w_in = jnp.pad(mlstm_w_in[j], ((0, 0), (0, LANES - n_gate))).astype(BF16)
            gate_b = jnp.pad(mlstm_gate_b[j].reshape(1, n_gate).astype(F32), ((0, 0), (0, LANES - n_gate)))
            col = jnp.arange(LANES)
            fmask = (((col // MLSTM_HEADS) % 2 == 1) & (col < n_gate)).astype(F32).reshape(1, LANES)
            nqk, nv = MLSTM_HEADS * MLSTM_QK, MLSTM_HEADS * MLSTM_V
            stream, q, k, v, og, gates = _row_call(
                functools.partial(_mlstm_pre_body, has_prev), cfg, rows_in, [g1, w_in, gate_b, fmask], mods_in, [],
                [((m, d), F32), ((m, nqk), BF16), ((m, nqk), BF16), ((m, nv), BF16), ((m, nv), BF16),
                 ((m, LANES), F32)], name="mlstm_pre")
            hf, hb = _mlstm_scan(cfg, q, k, v, gates)
            stream, h2, topi, topw, tile_counts = _row_call(
                _mlstm_post_body, cfg, [stream, hf, hb, og],
                [mlstm_w_o[j].astype(BF16), row(mlstm_norm_g[j])] + route_consts, route_mods, [], route_outs,
                name="mlstm_post")
        else:
            rep = GDN_V_HEADS // GDN_QK_HEADS
            qk_w = GDN_QK_HEADS * GDN_HEAD_DIM
            v_w = GDN_V_HEADS * GDN_HEAD_DIM
            ncv = 2 * qk_w + v_w
            w_main = gdn_w_in[j][:, :ncv + v_w].astype(BF16)
            w_ba = gdn_w_in[j][:, ncv + v_w:].reshape(d, 2, 2, GDN_QK_HEADS, rep).transpose(0, 3, 1, 2, 4)
            w_ba = jnp.pad(w_ba.reshape(d, GDN_QK_HEADS, 4 * rep), ((0, 0), (0, 0), (0, LANES - 4 * rep)))
            w_ba = w_ba.reshape(d, GDN_QK_HEADS * LANES).astype(BF16)

            def per_lane(p, fill):
                pv = p.astype(F32).reshape(2, GDN_QK_HEADS, rep).transpose(1, 0, 2)
                full = jnp.full((GDN_QK_HEADS, 2, 2, rep), fill, F32).at[:, :, 1, :].set(pv)
                full = jnp.pad(full.reshape(GDN_QK_HEADS, 4 * rep), ((0, 0), (0, LANES - 4 * rep)),
                               constant_values=fill)
                return full.reshape(1, GDN_QK_HEADS * LANES)

            a_scale = per_lane(jnp.exp(gdn_a_log[j].astype(F32)), 0.0)
            dt_b = per_lane(gdn_dt_bias[j], 0.0)
            dmask = per_lane(jnp.ones((2, GDN_V_HEADS), F32), 0.0)
            stream, zqkv, zg, ba = _row_call(
                functools.partial(_gdn_pre_body, has_prev), cfg, rows_in, [g1, w_main, w_ba, a_scale, dt_b, dmask],
                mods_in, [], [((m, d), F32), ((m, ncv), BF16), ((m, v_w), BF16), ((m, GDN_QK_HEADS * LANES), F32)],
                name="gdn_pre")
            conv_w = jnp.pad(gdn_conv_w[j].astype(F32), ((0, 8 - GDN_CONV), (0, 0)))
            qkv = _gdn_conv(cfg, zqkv, conv_w)
            of, ob = _gdn_scan(cfg, qkv, ba)
            stream, h2, topi, topw, tile_counts = _row_call(
                _gdn_post_body, cfg, [stream, of, ob, zg],
                [gdn_w_o[j].astype(BF16), row(gdn_norm_g[j])] + route_consts, route_mods, [], route_outs,
                name="gdn_post")

        y_prev = _moe_ffn(h2, topi, topw, tile_counts, i, moe_w1, moe_b1[i], moe_w2, moe_b2[i])
        gate2_prev = gate2

    out = _row_call(_final_body, cfg, [stream] + y_prev, [row(final_norm_g)], [gate2_prev], [],
                    [((cfg.n_lat, d), F32)], n_tiles=cfg.lat_tiles, name="final_norm")[0]
    return out.reshape(b, s, d)
```

```python
import functools
import math

import jax
import jax.numpy as jnp
from jax import lax
from jax.experimental import pallas as pl
from jax.experimental.pallas import tpu as pltpu

F32 = jnp.float32
BF16 = jnp.bfloat16
HIGHEST = lax.Precision.HIGHEST

D_MODEL = 1024
N_MOD = 6
GRID_W = 64
ROPE_BASE = 10000.0
NORM_EPS = 1e-6
LOG2_E = math.log2(math.e)

MLA_HEADS = 16
MLA_Q_RANK = 512
MLA_KV_RANK = 256
MLA_NOPE = 64
MLA_ROPE = 32
MLA_V = 64

GQA_Q_HEADS = 8
GQA_KV_HEADS = 4
GQA_HEAD_DIM = 128

MLSTM_HEADS = 4
MLSTM_QK = 128
MLSTM_V = 256

GDN_QK_HEADS = 8
GDN_V_HEADS = 16
GDN_HEAD_DIM = 128
GDN_CONV = 5
GDN_QK_PER_STEP = 4

N_EXPERTS = 32
TOP_K = 4
D_EXPERT = 1024
SWIGLU_ALPHA = 1.702
SWIGLU_LIMIT = 7.0

LANES = 128
ROW_TILE = 256
CHUNK = 64
Q_TILE = 512
ATTN_GROUPS_PER_STEP = 2
GROUP_TILE = 512
RANK_TILE = 512
RANK_TILES_PER_STEP = 4
MOD_ROWS = 16
HALO = 16
VMEM_LIMIT = 56 * 1024 * 1024


def _params(sem):
    return pltpu.CompilerParams(dimension_semantics=sem, vmem_limit_bytes=VMEM_LIMIT)


class _Cfg:
    def __init__(self, b, s, nc):
        assert s % ROW_TILE == 0 and nc % ROW_TILE == 0 and (b * s) % nc == 0 and b + 1 <= MOD_ROWS
        self.b, self.s, self.nc = b, s, nc
        self.n_lat = b * s
        self.m = b * s + b * nc
        self.lat_tiles = self.n_lat // ROW_TILE
        self.tiles = self.m // ROW_TILE
        self.tpb = s // ROW_TILE
        self.cpb = nc // ROW_TILE


def _rms(x, g):
    return x * lax.rsqrt(jnp.mean(x * x, axis=-1, keepdims=True) + NORM_EPS) * g


def _norm_mod(x, g, scale, shift):
    return _rms(x, g) * (1.0 + scale) + shift


def _sigmoid(x):
    return 1.0 / (1.0 + jnp.exp(-x))


def _softplus(x):
    return jnp.maximum(x, 0.0) + jnp.log(1.0 + jnp.exp(-jnp.abs(x)))


def _mm(a, b):
    return jnp.dot(a.astype(BF16), b.astype(BF16), preferred_element_type=F32)


def _mm_nt(a, b):
    return lax.dot_general(a.astype(BF16), b.astype(BF16), (((1,), (1,)), ((), ())), preferred_element_type=F32)


def _ada_kernel(c_ref, w_ref, b_ref, o_ref):
    cond = c_ref[...]
    a = cond * _sigmoid(cond)
    o_ref[...] = _mm(a, w_ref[...]) + b_ref[...]


def _ada_table(cond, ada_w, ada_b):
    depth, d, nd = ada_w.shape
    return pl.pallas_call(
        _ada_kernel,
        grid=(depth, nd // d),
        in_specs=[pl.BlockSpec((MOD_ROWS, d), lambda l, n: (0, 0)),
                  pl.BlockSpec((None, d, d), lambda l, n: (l, 0, n)),
                  pl.BlockSpec((None, 1, d), lambda l, n: (l, 0, n))],
        out_specs=pl.BlockSpec((None, MOD_ROWS, d), lambda l, n: (l, 0, n)),
        out_shape=jax.ShapeDtypeStruct((depth, MOD_ROWS, nd), F32),
        compiler_params=_params(("parallel", "parallel")),
        name="ada_table",
    )(cond, ada_w, ada_b.reshape(depth, 1, nd))


def _row_call(body, cfg, rows, consts, mods, tabs, outs, n_tiles=None, name=None):
    n_tiles = cfg.tiles if n_tiles is None else n_tiles
    lat_tiles, tpb = cfg.lat_tiles, cfg.tpb
    arrays, specs = [], []
    for r in rows:
        if isinstance(r, tuple):
            arr, g = r
            arrays.append(arr)
            specs.append(pl.BlockSpec((None, ROW_TILE, arr.shape[-1]), lambda i, g=g: (g, i, 0)))
        else:
            arrays.append(r)
            specs.append(pl.BlockSpec((ROW_TILE, r.shape[-1]), lambda i: (i, 0)))
    for a in consts:
        arrays.append(a)
        specs.append(pl.BlockSpec(a.shape, lambda i, nd=a.ndim: (0,) * nd))
    for a in mods:
        arrays.append(a)
        specs.append(pl.BlockSpec((None, 1, a.shape[-1]),
                                  lambda i: (jnp.where(i < lat_tiles, 1 + i // tpb, 0), 0, 0)))
    for a in tabs:
        arrays.append(a)
        specs.append(pl.BlockSpec((None, ROW_TILE, a.shape[-1]),
                                  lambda i: (jnp.where(i < lat_tiles, 1 + i % tpb, 0), 0, 0)))
    out_shapes, out_specs = [], []
    for shape, dtype in outs:
        out_shapes.append(jax.ShapeDtypeStruct(shape, dtype))
        if shape[0] == TOP_K:
            out_specs.append(pl.BlockSpec((shape[0], ROW_TILE), lambda i: (0, i)))
        elif shape[0] == N_EXPERTS:
            out_specs.append(pl.BlockSpec((shape[0], LANES), lambda i: (0, i)))
        else:
            out_specs.append(pl.BlockSpec((ROW_TILE, shape[-1]), lambda i: (i, 0)))
    n_r, n_c, n_m, n_t = len(rows), len(consts), len(mods), len(tabs)

    def kern(*refs):
        r = refs[:n_r]
        c = refs[n_r:n_r + n_c]
        m = refs[n_r + n_c:n_r + n_c + n_m]
        t = refs[n_r + n_c + n_m:n_r + n_c + n_m + n_t]
        o = refs[n_r + n_c + n_m + n_t:]
        body(r, c, m, t, o)

    return pl.pallas_call(
        kern, grid=(n_tiles,), in_specs=specs, out_specs=out_specs, out_shape=out_shapes,
        compiler_params=_params(("parallel",)), name=name,
    )(*arrays)


def _residual_in(r, m, has_prev):
    if not has_prev:
        return r[0][...]
    wt = r[1 + TOP_K][...]
    y = wt[:, 0:1] * r[1][...].astype(F32)
    for k in range(1, TOP_K):
        y = y + wt[:, k:k + 1] * r[1 + k][...].astype(F32)
    return r[0][...] + m[0][...] * y


def _rope_pair(x, cos, sin_a, sin_b, shift):
    return x * cos + pltpu.roll(x, shift, 1) * sin_a + pltpu.roll(x, LANES - shift, 1) * sin_b


def _mla_pre_body(has_prev, r, c, m, t, o):
    g1, w_in, qg, kvg, wq, wk, wv, place, v_ones = c
    shift1, scale1 = m[-2], m[-1]
    cos_q, sa_q, sb_q, cos_k, sa_k, sb_k = [a[...] for a in t]
    x = _residual_in(r, m, has_prev)
    h = _norm_mod(x, g1[...], scale1[...], shift1[...])
    z = _mm(h, w_in[...])
    cq = _rms(z[:, :MLA_Q_RANK], qg[...]).astype(BF16)
    ckv = _rms(z[:, MLA_Q_RANK:MLA_Q_RANK + MLA_KV_RANK], kvg[...]).astype(BF16)
    kr = z[:, MLA_Q_RANK + MLA_KV_RANK:]
    kr = _rope_pair(kr, cos_k, sa_k, sb_k, MLA_ROPE // 2)
    scale = (MLA_NOPE + MLA_ROPE) ** -0.5 * LOG2_E
    q = _mm(cq, wq[...])
    for hh in range(MLA_HEADS):
        qs = q[:, hh * LANES:(hh + 1) * LANES]
        qs = _rope_pair(qs, cos_q, sa_q, sb_q, MLA_ROPE // 2)
        o[1][:, hh * LANES:(hh + 1) * LANES] = (qs * scale).astype(BF16)
    o[0][...] = x
    o[2][...] = (_mm(ckv, wk[...]) + _mm(kr, place[...])).astype(BF16)
    o[3][...] = (_mm(ckv, wv[...]) + v_ones[...]).astype(BF16)


def _gqa_pre_body(has_prev, r, c, m, t, o):
    g1, w_in, qg, kg = c
    shift1, scale1 = m[-2], m[-1]
    cos, sin = t[0][...], t[1][...]
    x = _residual_in(r, m, has_prev)
    h = _norm_mod(x, g1[...], scale1[...], shift1[...])
    z = _mm(h, w_in[...])
    scale = GQA_HEAD_DIM ** -0.5 * LOG2_E
    hd = GQA_HEAD_DIM
    for hh in range(GQA_Q_HEADS + GQA_KV_HEADS):
        zs = z[:, hh * hd:(hh + 1) * hd]
        is_q = hh < GQA_Q_HEADS
        zs = _rms(zs, qg[...] if is_q else kg[...])
        zs = zs * cos + pltpu.roll(zs, hd // 2, 1) * sin
        if is_q:
            o[1][:, hh * hd:(hh + 1) * hd] = (zs * scale).astype(BF16)
        else:
            kk = hh - GQA_Q_HEADS
            o[2][:, kk * hd:(kk + 1) * hd] = zs.astype(BF16)
    o[0][...] = x
    o[3][...] = z[:, (GQA_Q_HEADS + GQA_KV_HEADS) * hd:].astype(BF16)


def _mlstm_pre_body(has_prev, r, c, m, t, o):
    g1, w_in, gate_b, fmask = c
    shift1, scale1 = m[-2], m[-1]
    x = _residual_in(r, m, has_prev)
    h = _norm_mod(x, g1[...], scale1[...], shift1[...])
    z = _mm(h, w_in[...])
    nqk = MLSTM_HEADS * MLSTM_QK
    nv = MLSTM_HEADS * MLSTM_V
    o[0][...] = x
    o[1][...] = z[:, :nqk].astype(BF16)
    o[2][...] = (z[:, nqk:2 * nqk] * MLSTM_QK ** -0.5).astype(BF16)
    o[3][...] = z[:, 2 * nqk:2 * nqk + nv].astype(BF16)
    o[4][...] = z[:, 2 * nqk + nv:2 * nqk + 2 * nv].astype(BF16)
    gt = z[:, 2 * nqk + 2 * nv:] + gate_b[...]
    log_sig = jnp.minimum(gt, 0.0) - jnp.log(1.0 + jnp.exp(-jnp.abs(gt)))
    o[5][...] = jnp.where(fmask[...] > 0.0, log_sig, gt)


def _gdn_pre_body(has_prev, r, c, m, t, o):
    g1, w_in, w_ba, a_scale, dt_b, dmask = c
    shift1, scale1 = m[-2], m[-1]
    x = _residual_in(r, m, has_prev)
    h = _norm_mod(x, g1[...], scale1[...], shift1[...]).astype(BF16)
    ncv = 2 * GDN_QK_HEADS * GDN_HEAD_DIM + GDN_V_HEADS * GDN_HEAD_DIM
    z = _mm(h, w_in[...])
    o[0][...] = x
    o[1][...] = z[:, :ncv].astype(BF16)
    o[2][...] = z[:, ncv:].astype(BF16)
    ba = _mm(h, w_ba[...])
    o[3][...] = jnp.where(dmask[...] > 0.0, -a_scale[...] * _softplus(ba + dt_b[...]), _sigmoid(ba))


def _route_tail(x, y, c, m, o):
    g2, rw_t, rb = c
    gate1, shift2, scale2 = m
    x1 = x + gate1[...] * y
    h2 = _norm_mod(x1, g2[...], scale2[...], shift2[...])
    o[0][...] = x1
    o[1][...] = h2.astype(BF16)
    logits = lax.dot_general(rw_t[...], h2, (((1,), (1,)), ((), ())), preferred_element_type=F32,
                             precision=HIGHEST) + rb[...]
    e_iota = lax.broadcasted_iota(jnp.int32, logits.shape, 0)
    vals = []
    picked = jnp.zeros(logits.shape, F32)
    for k in range(TOP_K):
        mx = jnp.max(logits, axis=0, keepdims=True)
        idx = jnp.min(jnp.where(logits == mx, e_iota, N_EXPERTS), axis=0, keepdims=True)
        o[2][k:k + 1, :] = idx
        vals.append(mx)
        hit = e_iota == idx
        picked = picked + jnp.where(hit, 1.0, 0.0)
        logits = jnp.where(hit, -jnp.inf, logits)
    o[4][...] = jnp.broadcast_to(jnp.sum(picked, axis=1, keepdims=True), o[4].shape)
    es = [jnp.exp(v - vals[0]) for v in vals]
    tot = es[0] + es[1] + es[2] + es[3]
    for k in range(TOP_K):
        o[3][k:k + 1, :] = es[k] / tot


def _attn_post_body(r, c, m, t, o):
    x, a = r
    y = _mm(a[...], c[0][...])
    _route_tail(x[...], y, c[1:], m, o)


def _mlstm_post_body(r, c, m, t, o):
    x, hf, hb, og = r
    w_o, ng = c[0], c[1]
    hs = hf[...] + hb[...]
    gate = _sigmoid(og[...].astype(F32))
    parts = []
    for hh in range(MLSTM_HEADS):
        sl = slice(hh * MLSTM_V, (hh + 1) * MLSTM_V)
        parts.append((_rms(hs[:, sl], ng[...]) * gate[:, sl]).astype(BF16))
    y = _mm(jnp.concatenate(parts, axis=1), w_o[...])
    _route_tail(x[...], y, c[2:], m, o)


def _gdn_post_body(r, c, m, t, o):
    x, of, ob, zg = r
    w_o, ng = c[0], c[1]
    os_ = of[...] + ob[...]
    g = zg[...].astype(F32)
    gate = g * _sigmoid(g)
    parts = []
    for hh in range(GDN_V_HEADS):
        sl = slice(hh * GDN_HEAD_DIM, (hh + 1) * GDN_HEAD_DIM)
        parts.append((_rms(os_[:, sl], ng[...]) * gate[:, sl]).astype(BF16))
    y = _mm(jnp.concatenate(parts, axis=1), w_o[...])
    _route_tail(x[...], y, c[2:], m, o)


def _final_body(r, c, m, t, o):
    o[0][...] = _rms(_residual_in(r, m, True), c[0][...])


def _attn_kernel(*refs, n_seg, k_stride, v_width):
    q_ref = refs[0]
    segs = [(refs[1 + 2 * i], refs[2 + 2 * i]) for i in range(n_seg)]
    o_ref = refs[1 + 2 * n_seg]
    n_heads = 2 * ATTN_GROUPS_PER_STEP

    def scores(j):
        q = q_ref[:, j * LANES:(j + 1) * LANES]
        koff = j * LANES if k_stride else (j // 2) * LANES
        return [_mm_nt(q, k_ref[:, koff:koff + LANES]) for k_ref, _ in segs]

    ss_next = scores(0)
    for j in range(n_heads):
        ss = ss_next
        if j + 1 < n_heads:
            ss_next = scores(j + 1)
        mx = ss[0].max(axis=-1, keepdims=True)
        for s in ss[1:]:
            mx = jnp.maximum(mx, s.max(axis=-1, keepdims=True))
        acc, den = 0.0, 0.0
        for s, (_, v_ref) in zip(ss, segs):
            p = jnp.exp2(s - mx)
            if k_stride:
                acc = acc + _mm(p, v_ref[:, j * 2 * v_width:(j + 1) * 2 * v_width])
            else:
                den = den + p.sum(axis=-1, keepdims=True)
                acc = acc + _mm(p, v_ref[:, (j // 2) * v_width:(j // 2 + 1) * v_width])
        if k_stride:
            den = acc[:, v_width:v_width + 1]
            acc = acc[:, :v_width]
        o_ref[:, j * v_width:(j + 1) * v_width] = (acc / den).astype(o_ref.dtype)


def _attention(cfg, q, k, v, n_groups, k_stride, v_width, latent):
    gps = ATTN_GROUPS_PER_STEP
    kw = gps * (2 * LANES if k_stride else LANES)
    vw = gps * (4 * v_width if k_stride else v_width)
    qw = gps * 2 * LANES
    ow = gps * 2 * v_width
    ctx_blk0 = cfg.n_lat // cfg.nc
    k_ctx = pl.BlockSpec((cfg.nc, kw), lambda b, g, i: (ctx_blk0 + b, g))
    v_ctx = pl.BlockSpec((cfg.nc, vw), lambda b, g, i: (ctx_blk0 + b, g))
    if latent:
        qpb = cfg.s // Q_TILE
        q_spec = pl.BlockSpec((Q_TILE, qw), lambda b, g, i: (b * qpb + i, g))
        o_spec = pl.BlockSpec((Q_TILE, ow), lambda b, g, i: (b * qpb + i, g))
        in_specs = [q_spec, pl.BlockSpec((cfg.s, kw), lambda b, g, i: (b, g)),
                    pl.BlockSpec((cfg.s, vw), lambda b, g, i: (b, g)), k_ctx, v_ctx]
        args = (q, k, v, k, v)
        n_seg = 2
        rows = cfg.n_lat
    else:
        tq = min(Q_TILE, cfg.nc)
        qpb = cfg.nc // tq
        q0 = cfg.n_lat // tq
        q_spec = pl.BlockSpec((tq, qw), lambda b, g, i: (q0 + b * qpb + i, g))
        o_spec = pl.BlockSpec((tq, ow), lambda b, g, i: (b * qpb + i, g))
        in_specs = [q_spec, k_ctx, v_ctx]
        args = (q, k, v)
        n_seg = 1
        rows = cfg.b * cfg.nc
    grid = (cfg.b, n_groups // gps, qpb)
    return pl.pallas_call(
        functools.partial(_attn_kernel, n_seg=n_seg, k_stride=k_stride, v_width=v_width),
        grid=grid, in_specs=in_specs, out_specs=o_spec,
        out_shape=jax.ShapeDtypeStruct((rows, n_groups * 2 * v_width), BF16),
        compiler_params=_params(("parallel", "parallel", "arbitrary")),
        name="attention_lat" if latent else "attention_ctx",
    )(*args)


def _block_masks(flip):
    li = lax.broadcasted_iota(jnp.int32, (ROW_TILE, ROW_TILE), 0)
    si = lax.broadcasted_iota(jnp.int32, (ROW_TILE, ROW_TILE), 1)
    same = (li // CHUNK) == (si // CHUNK)
    eye = li == si
    incl = same & ((si >= li) if flip else (si <= li))
    strict = same & ((si > li) if flip else (si < li))
    return eye, same, incl, strict


def _scan_block_index(cfg, flip):
    cpb, tpb, lat_tiles = cfg.cpb, cfg.tpb, cfg.lat_tiles

    def row_block(b, j):
        if flip:
            ctx = lat_tiles + b * cpb + (cpb - 1 - j)
            lat = b * tpb + (tpb - 1 - (j - cpb))
        else:
            ctx = lat_tiles + b * cpb + j
            lat = b * tpb + (j - cpb)
        return jnp.where(j < cpb, ctx, lat)

    return row_block


def _mlstm_kernel(q_ref, k_ref, v_ref, g_ref, o_ref, c_st, n_st, m_st, *, flip, d):
    @pl.when(pl.program_id(1) == 0)
    def _():
        c_st[...] = jnp.zeros_like(c_st)
        n_st[...] = jnp.zeros_like(n_st)
        m_st[...] = jnp.zeros_like(m_st)

    eye, same, incl, _ = _block_masks(flip)
    n_chunks = ROW_TILE // CHUNK
    order = [n_chunks - 1 - ci if flip else ci for ci in range(n_chunks)]
    to_row = lambda col: jnp.sum(jnp.where(eye, col, 0.0), axis=0, keepdims=True)
    per_chunk_col = lambda vals: jnp.concatenate(
        [jnp.broadcast_to(vals[c], (CHUNK, 1)) for c in range(n_chunks)], axis=0)
    hs = range(MLSTM_HEADS)
    qs = [q_ref[:, hh * MLSTM_QK:(hh + 1) * MLSTM_QK] for hh in hs]
    ks = [k_ref[:, hh * MLSTM_QK:(hh + 1) * MLSTM_QK] for hh in hs]
    vs = [v_ref[:, hh * MLSTM_V:(hh + 1) * MLSTM_V] for hh in hs]
    qks = [_mm_nt(qs[hh], ks[hh]) for hh in hs]
    i_cols = [g_ref[:, d * 2 * MLSTM_HEADS + hh:d * 2 * MLSTM_HEADS + hh + 1] for hh in hs]
    f_cols = [g_ref[:, (d * 2 + 1) * MLSTM_HEADS + hh:(d * 2 + 1) * MLSTM_HEADS + hh + 1] for hh in hs]
    f_rows = [to_row(f) for f in f_cols]
    i_rows = [to_row(i) for i in i_cols]
    bcum_cols = [jnp.sum(jnp.where(incl, f, 0.0), axis=1, keepdims=True) for f in f_rows]
    bcum_rows = [to_row(b) for b in bcum_cols]
    bend_cols = [jnp.sum(jnp.where(same, f, 0.0), axis=1, keepdims=True) for f in f_rows]
    w_logs = [bend_cols[hh] - bcum_cols[hh] + i_cols[hh] for hh in hs]
    wmax_cols = [jnp.max(jnp.where(same, to_row(w), -jnp.inf), axis=1, keepdims=True) for w in w_logs]
    m_prevs = [[None] * n_chunks for _ in hs]
    m_news = [[None] * n_chunks for _ in hs]
    m_runs = [m_st[hh] for hh in hs]
    for cc in order:
        r0 = cc * CHUNK
        for hh in hs:
            m_prevs[hh][cc] = m_runs[hh]
            m_runs[hh] = jnp.maximum(bend_cols[hh][r0:r0 + 1] + m_runs[hh], wmax_cols[hh][r0:r0 + 1])
            m_news[hh][cc] = m_runs[hh]
    d_logs = [jnp.where(incl, bcum_cols[hh] - bcum_rows[hh] + i_rows[hh], -jnp.inf) for hh in hs]
    g_logs = [bcum_cols[hh] + per_chunk_col(m_prevs[hh]) for hh in hs]
    m_ts = [jnp.maximum(g_logs[hh], jnp.max(d_logs[hh], axis=1, keepdims=True)) for hh in hs]
    ss = [qks[hh] * jnp.exp(d_logs[hh] - m_ts[hh]) for hh in hs]
    svs = [_mm(ss[hh], vs[hh]) for hh in hs]
    kws = [ks[hh].astype(F32) * jnp.exp(w_logs[hh] - per_chunk_col(m_news[hh])) for hh in hs]
    prep = []
    for hh in hs:
        carries = [jnp.exp(bend_cols[hh][cc * CHUNK:cc * CHUNK + 1] + m_prevs[hh][cc] - m_news[hh][cc])
                   for cc in range(n_chunks)]
        kvs = [_mm(kws[hh][cc * CHUNK:(cc + 1) * CHUNK].T, vs[hh][cc * CHUNK:(cc + 1) * CHUNK])
               for cc in range(n_chunks)]
        ksum = [jnp.sum(kws[hh][cc * CHUNK:(cc + 1) * CHUNK], axis=0, keepdims=True) for cc in range(n_chunks)]
        prep.append((qs[hh], jnp.exp(g_logs[hh] - m_ts[hh]), svs[hh], jnp.sum(ss[hh], axis=1, keepdims=True),
                     jnp.exp(-m_ts[hh]), kvs, ksum, carries, m_runs[hh]))
    for cc in order:
        rows = slice(cc * CHUNK, (cc + 1) * CHUNK)
        for hh in range(MLSTM_HEADS):
            q, inter, sv, ssum, floor, kvs, ksum, carries, _ = prep[hh]
            c_prev, n_prev = c_st[hh], n_st[hh]
            qc = q[rows]
            num = inter[rows] * _mm(qc, c_prev) + sv[rows]
            den = inter[rows] * jnp.sum(qc.astype(F32) * n_prev, axis=1, keepdims=True) + ssum[rows]
            o_ref[rows, hh * MLSTM_V:(hh + 1) * MLSTM_V] = num / jnp.maximum(jnp.abs(den), floor[rows])
            c_st[hh] = carries[cc] * c_prev + kvs[cc]
            n_st[hh] = carries[cc] * n_prev + ksum[cc]
    for hh in range(MLSTM_HEADS):
        m_st[hh] = prep[hh][-1]


def _mlstm_scan(cfg, q, k, v, gates):
    nqk, nv = MLSTM_HEADS * MLSTM_QK, MLSTM_HEADS * MLSTM_V
    outs = []
    for d in range(2):
        flip = d == 1
        rb = _scan_block_index(cfg, flip)
        row = lambda w, rb=rb: pl.BlockSpec((ROW_TILE, w), lambda b, j: (rb(b, j), 0))
        outs.append(pl.pallas_call(
            functools.partial(_mlstm_kernel, flip=flip, d=d),
            grid=(cfg.b, cfg.cpb + cfg.tpb),
            in_specs=[row(nqk), row(nqk), row(nv), row(LANES)],
            out_specs=row(nv),
            out_shape=jax.ShapeDtypeStruct((cfg.m, nv), F32),
            scratch_shapes=[pltpu.VMEM((MLSTM_HEADS, MLSTM_QK, MLSTM_V), F32),
                            pltpu.VMEM((MLSTM_HEADS, 1, MLSTM_QK), F32),
                            pltpu.VMEM((MLSTM_HEADS, 1, 1), F32)],
            compiler_params=_params(("parallel", "arbitrary")),
            name="mlstm_bwd" if flip else "mlstm_fwd",
        )(q, k, v, gates))
    return outs


def _gdn_kernel(q_ref, k_ref, v_ref, ba_ref, o_ref, s_st, *, flip, d):
    @pl.when(pl.program_id(2) == 0)
    def _():
        s_st[...] = jnp.zeros_like(s_st)

    eye, same, incl, strict = _block_masks(flip)
    eye_f = jnp.where(eye, 1.0, 0.0)
    rep = GDN_V_HEADS // GDN_QK_HEADS
    hd = GDN_HEAD_DIM
    n_chunks = ROW_TILE // CHUNK
    heads = [(qi, jj) for qi in range(GDN_QK_PER_STEP) for jj in range(rep)]
    ks = [k_ref[:, qi * hd:(qi + 1) * hd] for qi in range(GDN_QK_PER_STEP)]
    kfs = [k.astype(F32) for k in ks]
    qfs = [q_ref[:, qi * hd:(qi + 1) * hd].astype(F32) for qi in range(GDN_QK_PER_STEP)]
    qks = [_mm_nt(q_ref[:, qi * hd:(qi + 1) * hd], ks[qi]) for qi in range(GDN_QK_PER_STEP)]
    betas, gcs, gces, decays, kbs = [], [], [], [], []
    for qi, jj in heads:
        bcol = qi * LANES + d * 2 * rep + jj
        beta = ba_ref[:, bcol:bcol + 1]
        ld = ba_ref[:, bcol + rep:bcol + rep + 1]
        ld_row = jnp.sum(jnp.where(eye, ld, 0.0), axis=0, keepdims=True)
        gc_col = jnp.sum(jnp.where(incl, ld_row, 0.0), axis=1, keepdims=True)
        gc_row = jnp.sum(jnp.where(eye, gc_col, 0.0), axis=0, keepdims=True)
        gces.append(jnp.sum(jnp.where(same, ld_row, 0.0), axis=1, keepdims=True))
        decays.append(jnp.exp(jnp.where(incl, gc_col - gc_row, -jnp.inf)))
        betas.append(beta)
        gcs.append(gc_col)
        kbs.append(kfs[qi] * beta)
    def pack(bd):
        out = bd[0:CHUNK]
        for c in range(1, n_chunks):
            out = out + bd[c * CHUNK:(c + 1) * CHUNK]
        return out

    def block_diag(packed):
        return jnp.where(same, jnp.concatenate([packed] * n_chunks, axis=0), 0.0)

    pws = [pack(jnp.where(strict, _mm_nt(kbs[i], ks[qi]) * decays[i], 0.0)) for i, (qi, _) in enumerate(heads)]
    eye_p = pack(eye_f)
    invs = [eye_p - p for p in pws]
    bds = [block_diag(p) for p in pws]
    for _ in range(int(math.log2(CHUNK)) - 1):
        pws = [_mm(p, bd) for p, bd in zip(pws, bds)]
        bds = [block_diag(p) for p in pws]
        invs = [iv + _mm(iv, bd) for iv, bd in zip(invs, bds)]
    invs = [block_diag(iv) for iv in invs]
    egs = [jnp.exp(g) for g in gcs]
    uws = []
    for i, (qi, jj) in enumerate(heads):
        v = v_ref[:, (qi * rep + jj) * hd:(qi * rep + jj + 1) * hd].astype(F32)
        uws.append(_mm(invs[i], jnp.concatenate([v * betas[i], kbs[i] * egs[i]], axis=1)))
    aws = [_mm(qks[qi] * decays[i], uws[i]) for i, (qi, _) in enumerate(heads)]
    qes = [qfs[qi] * egs[i] - aws[i][:, hd:] for i, (qi, _) in enumerate(heads)]
    kdecs = [kfs[qi] * jnp.exp(gces[i] - gcs[i]) for i, (qi, _) in enumerate(heads)]
    for ci in range(n_chunks):
        cc = n_chunks - 1 - ci if flip else ci
        rows = slice(cc * CHUNK, (cc + 1) * CHUNK)
        bgs = [_mm(kdecs[i][rows].T, uws[i][rows]) for i in range(len(heads))]
        for i, (qi, jj) in enumerate(heads):
            s_prev = s_st[i]
            col = (qi * rep + jj) * hd
            o_ref[rows, col:col + hd] = _mm(qes[i][rows], s_prev) + aws[i][rows, :hd]
            g_end = jnp.exp(gces[i][cc * CHUNK:cc * CHUNK + 1])
            s_st[i] = g_end * s_prev - _mm(bgs[i][:, hd:], s_prev) + bgs[i][:, :hd]


def _gdn_scan(cfg, qkv, ba):
    rep = GDN_V_HEADS // GDN_QK_HEADS
    hd = GDN_HEAD_DIM
    g = GDN_QK_PER_STEP
    outs = []
    for d in range(2):
        flip = d == 1
        rb = _scan_block_index(cfg, flip)
        col = lambda w, off, rb=rb: pl.BlockSpec((ROW_TILE, w), lambda b, h, j: (rb(b, j), off + h))
        outs.append(pl.pallas_call(
            functools.partial(_gdn_kernel, flip=flip, d=d),
            grid=(cfg.b, GDN_QK_HEADS // g, cfg.cpb + cfg.tpb),
            in_specs=[col(g * hd, 0), col(g * hd, GDN_QK_HEADS // g),
                      col(g * rep * hd, 2 * GDN_QK_HEADS // (g * rep)), col(g * LANES, 0)],
            out_specs=col(g * rep * hd, 0),
            out_shape=jax.ShapeDtypeStruct((cfg.m, GDN_V_HEADS * hd), F32),
            scratch_shapes=[pltpu.VMEM((g * rep, hd, hd), F32)],
            compiler_params=_params(("parallel", "parallel", "arbitrary")),
            name="gdn_bwd" if flip else "gdn_fwd",
        )(qkv, qkv, qkv, ba))
    return outs


def _gdn_conv_kernel(cur_ref, prev_ref, next_ref, w_ref, o_ref, pad_ref, *, lat_tiles, tpb, cpb):
    i = pl.program_id(0)
    cb = pl.program_id(1)
    lat = i < lat_tiles
    pos = jnp.where(lat, i % tpb, (i - lat_tiles) % cpb)
    per = jnp.where(lat, tpb, cpb)
    first = pos == 0
    last = pos == per - 1
    prev = prev_ref[...].astype(F32)
    nxt = next_ref[...].astype(F32)
    pad_ref[0:HALO, :] = jnp.where(first, 0.0, prev)
    pad_ref[HALO:HALO + ROW_TILE, :] = cur_ref[...].astype(F32)
    pad_ref[HALO + ROW_TILE:, :] = jnp.where(last, 0.0, nxt)
    half = GDN_CONV // 2
    acc = 0.0
    for tap in range(GDN_CONV):
        off = HALO - half + tap
        acc = acc + pad_ref[off:off + ROW_TILE, :] * w_ref[tap:tap + 1, :]
    act = acc * _sigmoid(acc)
    qscale = jnp.where(cb == 0, GDN_HEAD_DIM ** -0.5, 1.0)
    hd = GDN_HEAD_DIM
    for hh in range(act.shape[1] // hd):
        a = act[:, hh * hd:(hh + 1) * hd]
        nrm = a * lax.rsqrt(jnp.sum(a * a, axis=-1, keepdims=True) + NORM_EPS) * qscale
        o_ref[:, hh * hd:(hh + 1) * hd] = jnp.where(cb < 2, nrm, a).astype(o_ref.dtype)


def _gdn_conv(cfg, zqkv, conv_w):
    cw = GDN_QK_HEADS * GDN_HEAD_DIM
    n_cb = zqkv.shape[1] // cw
    hpt = ROW_TILE // HALO
    n_halo = cfg.m // HALO
    return pl.pallas_call(
        functools.partial(_gdn_conv_kernel, lat_tiles=cfg.lat_tiles, tpb=cfg.tpb, cpb=cfg.cpb),
        grid=(cfg.tiles, n_cb),
        in_specs=[pl.BlockSpec((ROW_TILE, cw), lambda i, c: (i, c)),
                  pl.BlockSpec((HALO, cw), lambda i, c: (jnp.maximum(i * hpt - 1, 0), c)),
                  pl.BlockSpec((HALO, cw), lambda i, c: (jnp.minimum((i + 1) * hpt, n_halo - 1), c)),
                  pl.BlockSpec((8, cw), lambda i, c: (0, c))],
        out_specs=pl.BlockSpec((ROW_TILE, cw), lambda i, c: (i, c)),
        out_shape=jax.ShapeDtypeStruct(zqkv.shape, BF16),
        scratch_shapes=[pltpu.VMEM((ROW_TILE + 2 * HALO, cw), F32)],
        compiler_params=_params(("parallel", "parallel")),
        name="gdn_conv",
    )(zqkv, zqkv, zqkv, conv_w)


def _rank_kernel(e_ref, start_ref, slot_ref, run_ref):
    @pl.when(pl.program_id(0) == 0)
    def _():
        run_ref[...] = start_ref[...]

    ji = lax.broadcasted_iota(jnp.int32, (RANK_TILE, RANK_TILE), 0)
    si = lax.broadcasted_iota(jnp.int32, (RANK_TILE, RANK_TILE), 1)
    upper = jnp.where(ji <= si, 1.0, 0.0).astype(BF16)
    run = run_ref[:, 0:1]
    for sb in range(RANK_TILES_PER_STEP):
        cols = slice(sb * RANK_TILE, (sb + 1) * RANK_TILE)
        e = e_ref[:, cols]
        onehot = (lax.broadcasted_iota(jnp.int32, (N_EXPERTS, RANK_TILE), 0) == e)
        oh = jnp.where(onehot, 1.0, 0.0)
        cum = jnp.dot(oh.astype(BF16), upper, preferred_element_type=F32)
        slot = jnp.sum(oh * (cum - 1.0 + run), axis=0, keepdims=True)
        slot_ref[:, cols] = slot.astype(jnp.int32)
        run = run + jnp.sum(oh, axis=1, keepdims=True)
    run_ref[...] = jnp.broadcast_to(run, run_ref.shape)


def _slot_pairs(expert_of_pair, group_start):
    n_pairs = expert_of_pair.shape[1]
    step = RANK_TILE * RANK_TILES_PER_STEP
    return pl.pallas_call(
        _rank_kernel,
        grid=(n_pairs // step,),
        in_specs=[pl.BlockSpec((1, step), lambda i: (0, i)),
                  pl.BlockSpec((N_EXPERTS, LANES), lambda i: (0, 0))],
        out_specs=pl.BlockSpec((1, step), lambda i: (0, i)),
        out_shape=jax.ShapeDtypeStruct((1, n_pairs), jnp.int32),
        scratch_shapes=[pltpu.VMEM((N_EXPERTS, LANES), F32)],
        compiler_params=_params(("arbitrary",)),
        name="moe_rank",
    )(expert_of_pair, group_start)


def _moe_kernel(te_ref, nu_ref, x_ref, w1_ref, b1g_ref, b1l_ref, w2_ref, b2_ref, o_ref, w1g_s, w1l_s, w2_s):
    j = pl.program_id(0)
    used = j < nu_ref[0]
    new_expert = jnp.logical_or(j == 0, te_ref[j] != te_ref[jnp.maximum(j - 1, 0)])

    @pl.when(jnp.logical_and(used, new_expert))
    def _():
        slab = 2 * LANES
        src = lax.broadcasted_iota(jnp.int32, (slab, slab), 0)
        dst = lax.broadcasted_iota(jnp.int32, (slab, slab), 1)
        want = jnp.where(dst < LANES, 2 * dst, 2 * (dst - LANES) + 1)
        perm = jnp.where(src == want, 1.0, 0.0).astype(BF16)
        for sb in range(w1_ref.shape[1] // slab):
            sorted_cols = jnp.dot(w1_ref[:, sb * slab:(sb + 1) * slab].astype(BF16), perm,
                                  preferred_element_type=F32)
            w1g_s[:, sb * LANES:(sb + 1) * LANES] = sorted_cols[:, :LANES].astype(BF16)
            w1l_s[:, sb * LANES:(sb + 1) * LANES] = sorted_cols[:, LANES:].astype(BF16)
        w2_s[...] = w2_ref[...].astype(BF16)

    @pl.when(used)
    def _():
        x = x_ref[...]
        glu = jnp.minimum(jnp.dot(x, w1g_s[...], preferred_element_type=F32) + b1g_ref[...], SWIGLU_LIMIT)
        lin = jnp.clip(jnp.dot(x, w1l_s[...], preferred_element_type=F32) + b1l_ref[...],
                       -SWIGLU_LIMIT, SWIGLU_LIMIT)
        act = glu * _sigmoid(SWIGLU_ALPHA * glu) * (lin + 1.0)
        y = jnp.dot(act.astype(BF16), w2_s[...], preferred_element_type=F32) + b2_ref[...]
        o_ref[...] = y.astype(o_ref.dtype)

    @pl.when(jnp.logical_not(used))
    def _():
        o_ref[...] = jnp.zeros_like(o_ref)


def _moe_grouped(xs, tile_expert, n_used, layer, w1_all, b1g, b1l, w2_all, b2):
    n_slots, d = xs.shape
    de = w2_all.shape[2]
    wspec = lambda r, c: pl.BlockSpec((None, r, c), lambda j, te, nu: (te[j], 0, 0))
    lspec = lambda r, c: pl.BlockSpec((None, None, r, c), lambda j, te, nu: (layer, te[j], 0, 0))
    return pl.pallas_call(
        _moe_kernel,
        grid_spec=pltpu.PrefetchScalarGridSpec(
            num_scalar_prefetch=2,
            grid=(n_slots // GROUP_TILE,),
            in_specs=[pl.BlockSpec((GROUP_TILE, d), lambda j, te, nu: (j, 0)),
                      lspec(d, 2 * de), wspec(1, de), wspec(1, de), lspec(de, d), wspec(1, d)],
            out_specs=pl.BlockSpec((GROUP_TILE, d), lambda j, te, nu: (j, 0)),
            scratch_shapes=[pltpu.VMEM((d, de), BF16), pltpu.VMEM((d, de), BF16), pltpu.VMEM((de, d), BF16)]),
        out_shape=jax.ShapeDtypeStruct((n_slots, d), BF16),
        compiler_params=_params(("arbitrary",)),
        name="moe_grouped",
    )(tile_expert, n_used, xs, w1_all, b1g, b1l, w2_all, b2)


def _moe_ffn(h2, topi, topw, tile_counts, layer, w1_all, b1, w2_all, b2):
    n_tok = h2.shape[0]
    n_pairs = TOP_K * n_tok
    assert n_pairs % (RANK_TILE * RANK_TILES_PER_STEP) == 0
    n_slots = -(-(n_pairs + N_EXPERTS * (GROUP_TILE - 1)) // GROUP_TILE) * GROUP_TILE
    n_tiles = n_slots // GROUP_TILE

    counts = jnp.sum(tile_counts.reshape(N_EXPERTS, -1, LANES)[:, :, 0], axis=1).astype(jnp.int32)
    padded = (counts + GROUP_TILE - 1) // GROUP_TILE * GROUP_TILE
    ends = jnp.cumsum(padded)
    starts = ends - padded
    group_start = jnp.broadcast_to(starts.astype(F32)[:, None], (N_EXPERTS, LANES))
    slot = _slot_pairs(topi.reshape(1, n_pairs), group_start)[0]
    token = jnp.tile(jnp.arange(n_tok, dtype=jnp.int32), TOP_K)
    src = (jnp.arange(n_slots, dtype=jnp.int32) % n_tok).at[slot].set(
        token, unique_indices=True, mode="promise_in_bounds")
    tile_start = jnp.arange(n_tiles, dtype=jnp.int32) * GROUP_TILE
    tile_expert = jnp.sum((ends[None, :] <= tile_start[:, None]).astype(jnp.int32), axis=1)
    tile_expert = jnp.minimum(tile_expert, N_EXPERTS - 1)
    n_used = (ends[-1] // GROUP_TILE).astype(jnp.int32).reshape(1)

    xs = jnp.take(h2, src, axis=0, mode="clip")
    b1g = b1[:, None, 0::2]
    b1l = b1[:, None, 1::2]
    ys = _moe_grouped(xs, tile_expert, n_used, layer, w1_all, b1g, b1l, w2_all, b2[:, None, :])

    pos = slot.reshape(TOP_K, n_tok)
    return [jnp.take(ys, pos[k], axis=0, mode="clip") for k in range(TOP_K)] + [topw.T]


def _axial_angles(n_tokens, rot_dim):
    t = jnp.arange(n_tokens, dtype=jnp.int32)
    rows = (t // GRID_W).astype(F32)
    cols = (t % GRID_W).astype(F32)
    d_axis = rot_dim // 2
    inv_freq = ROPE_BASE ** (-jnp.arange(0, d_axis, 2, dtype=F32) / d_axis)
    ang = jnp.concatenate([rows[:, None] * inv_freq, cols[:, None] * inv_freq], axis=-1)
    return jnp.cos(ang), jnp.sin(ang)


def _seq_table(cfg, lat_rows, ctx_value):
    c = lat_rows.shape[-1]
    ctx = jnp.full((1, ROW_TILE, c), ctx_value, F32)
    return jnp.concatenate([ctx, lat_rows.reshape(cfg.tpb, ROW_TILE, c)], axis=0)


def _slab_rope_tables(cfg, rot_dim, lane0):
    cos, sin = _axial_angles(cfg.s, rot_dim)
    half = rot_dim // 2
    s = cfg.s
    ones = lambda n: jnp.ones((s, n), F32)
    zeros = lambda n: jnp.zeros((s, n), F32)
    tail = LANES - lane0 - rot_dim
    cos_t = jnp.concatenate([ones(lane0), cos, cos, ones(tail)], axis=1)
    sin_a = jnp.concatenate([zeros(lane0 + half), sin, zeros(tail)], axis=1)
    sin_b = jnp.concatenate([zeros(lane0), -sin, zeros(half + tail)], axis=1)
    return _seq_table(cfg, cos_t, 1.0), _seq_table(cfg, sin_a, 0.0), _seq_table(cfg, sin_b, 0.0)


def _mod_slices(table, n_rows):
    d = D_MODEL
    return [table[:, k * d:(k + 1) * d].reshape(n_rows, 1, d) for k in range(N_MOD)]


def kernel(x, c, ctx, c_ctx, ada_w, ada_b, norm1_g, norm2_g, router_w, router_b, moe_w1, moe_b1, moe_w2, moe_b2,
           mla_w_in, mla_q_norm_g, mla_kv_norm_g, mla_w_uq, mla_w_ukv, mla_w_o, gqa_w_in, gqa_q_norm_g,
           gqa_k_norm_g, gqa_w_o, mlstm_w_in, mlstm_gate_b, mlstm_norm_g, mlstm_w_o, gdn_w_in, gdn_conv_w,
           gdn_a_log, gdn_dt_bias, gdn_norm_g, gdn_w_o, final_norm_g):
    b, s, d = x.shape
    nc = ctx.shape[1]
    depth = ada_w.shape[0]
    cfg = _Cfg(b, s, nc)
    m = cfg.m
    row = lambda a: a.reshape(1, -1).astype(F32)

    cond = jnp.concatenate([c_ctx[None, :], c, jnp.zeros((MOD_ROWS - 1 - b, d), F32)], axis=0)
    mod_all = _ada_table(cond, ada_w, ada_b)

    stream = jnp.concatenate([x.reshape(b * s, d), ctx.reshape(b * nc, d)], axis=0)
    y_prev = None
    gate2_prev = None
    for i in range(depth):
        kind, j = i % 4, i // 4
        shift1, scale1, gate1, shift2, scale2, gate2 = _mod_slices(mod_all[i], MOD_ROWS)
        has_prev = y_prev is not None
        rows_in = [stream] + y_prev if has_prev else [stream]
        mods_in = ([gate2_prev] if has_prev else []) + [shift1, scale1]
        g1 = row(norm1_g[i])
        route_consts = [row(norm2_g[i]), router_w[i].T, router_b[i].reshape(N_EXPERTS, 1)]
        route_mods = [gate1, shift2, scale2]
        route_outs = [((m, d), F32), ((m, d), BF16), ((TOP_K, m), jnp.int32), ((TOP_K, m), F32),
                      ((N_EXPERTS, cfg.tiles * LANES), F32)]

        if kind == 0:
            qk = MLA_NOPE + MLA_ROPE
            w_in = jnp.pad(mla_w_in[j], ((0, 0), (0, LANES - MLA_ROPE))).astype(BF16)
            wq = jnp.pad(mla_w_uq[j].reshape(MLA_Q_RANK, MLA_HEADS, qk), ((0, 0), (0, 0), (0, LANES - qk)))
            wq = wq.reshape(MLA_Q_RANK, MLA_HEADS * LANES).astype(BF16)
            wkv = mla_w_ukv[j].reshape(MLA_KV_RANK, MLA_HEADS, MLA_NOPE + MLA_V)
            wk = jnp.pad(wkv[:, :, :MLA_NOPE], ((0, 0), (0, 0), (0, LANES - MLA_NOPE)))
            wk = wk.reshape(MLA_KV_RANK, MLA_HEADS * LANES).astype(BF16)
            wv = jnp.pad(wkv[:, :, MLA_NOPE:], ((0, 0), (0, 0), (0, MLA_V)))
            wv = wv.reshape(MLA_KV_RANK, MLA_HEADS * 2 * MLA_V).astype(BF16)
            v_ones = jnp.tile(jnp.concatenate([jnp.zeros((MLA_V,), F32), jnp.ones((MLA_V,), F32)]), MLA_HEADS)
            v_ones = v_ones.reshape(1, MLA_HEADS * 2 * MLA_V)
            place = jnp.zeros((LANES, LANES), F32).at[jnp.arange(MLA_ROPE), MLA_NOPE + jnp.arange(MLA_ROPE)].set(1.0)
            place = jnp.tile(place, (1, MLA_HEADS)).astype(BF16)
            tabs = list(_slab_rope_tables(cfg, MLA_ROPE, MLA_NOPE)) + list(_slab_rope_tables(cfg, MLA_ROPE, 0))
            stream, q, k, v = _row_call(
                functools.partial(_mla_pre_body, has_prev), cfg, rows_in,
                [g1, w_in, row(mla_q_norm_g[j]), row(mla_kv_norm_g[j]), wq, wk, wv, place, v_ones], mods_in, tabs,
                [((m, d), F32), ((m, MLA_HEADS * LANES), BF16), ((m, MLA_HEADS * LANES), BF16),
                 ((m, MLA_HEADS * 2 * MLA_V), BF16)], name="mla_pre")
            a_lat = _attention(cfg, q, k, v, MLA_HEADS // 2, LANES, MLA_V, True)
            a_ctx = _attention(cfg, q, k, v, MLA_HEADS // 2, LANES, MLA_V, False)
            a = jnp.concatenate([a_lat, a_ctx], axis=0)
            stream, h2, topi, topw, tile_counts = _row_call(
                _attn_post_body, cfg, [stream, a], [mla_w_o[j].astype(BF16)] + route_consts, route_mods, [],
                route_outs, name="mla_post")
        elif kind == 1:
            cos, sin = _axial_angles(cfg.s, GQA_HEAD_DIM)
            tabs = [_seq_table(cfg, jnp.concatenate([cos, cos], axis=1), 1.0),
                    _seq_table(cfg, jnp.concatenate([-sin, sin], axis=1), 0.0)]
            stream, q, k, v = _row_call(
                functools.partial(_gqa_pre_body, has_prev), cfg, rows_in,
                [g1, gqa_w_in[j].astype(BF16), row(gqa_q_norm_g[j]), row(gqa_k_norm_g[j])], mods_in, tabs,
                [((m, d), F32), ((m, GQA_Q_HEADS * GQA_HEAD_DIM), BF16), ((m, GQA_KV_HEADS * GQA_HEAD_DIM), BF16),
                 ((m, GQA_KV_HEADS * GQA_HEAD_DIM), BF16)], name="gqa_pre")
            a_lat = _attention(cfg, q, k, v, GQA_KV_HEADS, 0, GQA_HEAD_DIM, True)
            a_ctx = _attention(cfg, q, k, v, GQA_KV_HEADS, 0, GQA_HEAD_DIM, False)
            a = jnp.concatenate([a_lat, a_ctx], axis=0)
            stream, h2, topi, topw, tile_counts = _row_call(
                _attn_post_body, cfg, [stream, a], [gqa_w_o[j].astype(BF16)] + route_consts, route_mods, [],
                route_outs, name="gqa_post")
        elif kind == 2:
            n_gate = 4 * MLSTM_HEADS
            w_in = jnp.pad(mlstm_w_in[j], ((0, 0), (0, LANES - n_gate))).astype(BF16)
            gate_b = jnp.pad(mlstm_gate_b[j].reshape(1, n_gate).astype(F32), ((0, 0), (0, LANES - n_gate)))
            col = jnp.arange(LANES)
            fmask = (((col // MLSTM_HEADS) % 2 == 1) & (col < n_gate)).astype(F32).reshape(1, LANES)
            nqk, nv = MLSTM_HEADS * MLSTM_QK, MLSTM_HEADS * MLSTM_V
            stream, q, k, v, og, gates = _row_call(
                functools.partial(_mlstm_pre_body, has_prev), cfg, rows_in, [g1, w_in, gate_b, fmask], mods_in, [],
                [((m, d), F32), ((m, nqk), BF16), ((m, nqk), BF16), ((m, nv), BF16), ((m, nv), BF16),
                 ((m, LANES), F32)], name="mlstm_pre")
            hf, hb = _mlstm_scan(cfg, q, k, v, gates)
            stream, h2, topi, topw, tile_counts = _row_call(
                _mlstm_post_body, cfg, [stream, hf, hb, og],
                [mlstm_w_o[j].astype(BF16), row(mlstm_norm_g[j])] + route_consts, route_mods, [], route_outs,
                name="mlstm_post")
        else:
            rep = GDN_V_HEADS // GDN_QK_HEADS
            qk_w = GDN_QK_HEADS * GDN_HEAD_DIM
            v_w = GDN_V_HEADS * GDN_HEAD_DIM
            ncv = 2 * qk_w + v_w
            w_main = gdn_w_in[j][:, :ncv + v_w].astype(BF16)
            w_ba = gdn_w_in[j][:, ncv + v_w:].reshape(d, 2, 2, GDN_QK_HEADS, rep).transpose(0, 3, 1, 2, 4)
            w_ba = jnp.pad(w_ba.reshape(d, GDN_QK_HEADS, 4 * rep), ((0, 0), (0, 0), (0, LANES - 4 * rep)))
            w_ba = w_ba.reshape(d, GDN_QK_HEADS * LANES).astype(BF16)

            def per_lane(p, fill):
                pv = p.astype(F32).reshape(2, GDN_QK_HEADS, rep).transpose(1, 0, 2)
                full = jnp.full((GDN_QK_HEADS, 2, 2, rep), fill, F32).at[:, :, 1, :].set(pv)
                full = jnp.pad(full.reshape(GDN_QK_HEADS, 4 * rep), ((0, 0), (0, LANES - 4 * rep)),
                               constant_values=fill)
                return full.reshape(1, GDN_QK_HEADS * LANES)

            a_scale = per_lane(jnp.exp(gdn_a_log[j].astype(F32)), 0.0)
            dt_b = per_lane(gdn_dt_bias[j], 0.0)
            dmask = per_lane(jnp.ones((2, GDN_V_HEADS), F32), 0.0)
            stream, zqkv, zg, ba = _row_call(
                functools.partial(_gdn_pre_body, has_prev), cfg, rows_in, [g1, w_main, w_ba, a_scale, dt_b, dmask],
                mods_in, [], [((m, d), F32), ((m, ncv), BF16), ((m, v_w), BF16), ((m, GDN_QK_HEADS * LANES), F32)],
                name="gdn_pre")
            conv_w = jnp.pad(gdn_conv_w[j].astype(F32), ((0, 8 - GDN_CONV), (0, 0)))
            qkv = _gdn_conv(cfg, zqkv, conv_w)
            of, ob = _gdn_scan(cfg, qkv, ba)
            stream, h2, topi, topw, tile_counts = _row_call(
                _gdn_post_body, cfg, [stream, of, ob, zg],
                [gdn_w_o[j].astype(BF16), row(gdn_norm_g[j])] + route_consts, route_mods, [], route_outs,
                name="gdn_post")

        y_prev = _moe_ffn(h2, topi, topw, tile_counts, i, moe_w1, moe_b1[i], moe_w2, moe_b2[i])
        gate2_prev = gate2

    out = _row_call(_final_body, cfg, [stream] + y_prev, [row(final_norm_g)], [gate2_prev], [],
                    [((cfg.n_lat, d), F32)], n_tiles=cfg.lat_tiles, name="final_norm")[0]
    return out.reshape(b, s, d)
```
